```python
import math
import jax, jax.numpy as jnp
from jax import lax
import numpy as np

D_MODEL = 2048
BATCH = 4
SEQ = 8192
DEPTH = 1
DEC_BATCH = 32
DEC_SEQ = 64
PAST_LEN = 1024

CHUNK = 64
Q_BLOCK = 128
N_A = 8
DA_HEAD = 64
DA_V = 2 * DA_HEAD
N_B = 8
DB_HEAD = 128
ROT_DIM = DA_HEAD // 4
ROPE_THETA = 500000.0
D_FF = 5504
EPS = 1e-6
W_QA = N_A * 2 * DA_HEAD
W_VA = N_A * DA_V
W_B = N_B * DB_HEAD
IN_COLS = 2 * W_QA + W_VA + 3 * W_B + 2 * D_MODEL

kernel_name = "hybrid_diffattn_stickbreaking_streaming_step"


def rmsnorm(x, g):
    xf = x.astype(jnp.float32)
    y = xf * lax.rsqrt(jnp.mean(xf * xf, axis=-1, keepdims=True) + EPS)
    return (y * g.astype(jnp.float32)).astype(x.dtype)


def swiglu(x, wg, wu, wd):
    return (jax.nn.silu(x @ wg) * (x @ wu)) @ wd


def rope_partial(x, pos):
    half = ROT_DIM // 2
    inv_freq = jnp.power(ROPE_THETA, -jnp.arange(0, ROT_DIM, 2, dtype=jnp.float32) / ROT_DIM)
    ang = pos.astype(jnp.float32)[:, None] * inv_freq[None, :]
    cos = jnp.cos(ang)[None, :, None, None, :]
    sin = jnp.sin(ang)[None, :, None, None, :]
    xr = x[..., :ROT_DIM].astype(jnp.float32)
    x1, x2 = xr[..., :half], xr[..., half:]
    rot = jnp.concatenate([x1 * cos - x2 * sin, x2 * cos + x1 * sin], axis=-1)
    return jnp.concatenate([rot.astype(x.dtype), x[..., ROT_DIM:]], axis=-1)


def diff_attn(q, k, v, q_pos, k_pos, lam, subln_g, lam_init):
    b, tq = q.shape[0], q.shape[1]
    s = jnp.einsum('bqhmd,bkhmd->bhmqk', q, k).astype(jnp.float32) * (DA_HEAD ** -0.5)
    mask = (k_pos[None, :] // CHUNK) <= (q_pos[:, None] // CHUNK)
    s = jnp.where(mask[None, None, None], s, -1e30)
    p = jax.nn.softmax(s, axis=-1)
    w = p[:, :, 0] - lam * p[:, :, 1]
    o = jnp.einsum('bhqk,bkhe->bqhe', w.astype(v.dtype), v)
    o = rmsnorm(o, subln_g) * (1.0 - lam_init)
    return o.reshape(b, tq, W_VA)


def stick_breaking_attn(q, k, v, q_pos, k_pos):
    b, tq = q.shape[0], q.shape[1]
    z = jnp.einsum('bqhd,bkhd->bhqk', q, k).astype(jnp.float32) * (DB_HEAD ** -0.5)
    mask = (k_pos[None, :] < q_pos[:, None])[None, None]
    u = jnp.where(mask, jax.nn.log_sigmoid(-z), 0.0)
    rest = lax.cumsum(u, axis=3, reverse=True) - u
    a = jnp.where(mask, jnp.exp(jax.nn.log_sigmoid(z) + rest), 0.0)
    o = jnp.einsum('bhqk,bkhd->bqhd', a.astype(v.dtype), v)
    return o.reshape(b, tq, W_B)


def sweep_queries(fn, q, q_pos):
    b, t = q.shape[0], q.shape[1]
    nb = t // Q_BLOCK
    qb = q.reshape((b, nb, Q_BLOCK) + q.shape[2:]).swapaxes(0, 1)
    pb = q_pos.reshape(nb, Q_BLOCK)
    out = lax.map(lambda a: fn(a[0], a[1]), (qb, pb))
    return out.swapaxes(0, 1).reshape(b, t, out.shape[-1])


def layer(x, pos, past, blocked, lam_init, w_in, w_up_a, w_up_b, w_o, lq1, lk1, lq2, lk2, subln_g,
          n1a, n1b, nma, nmb, n2a, n2b, f1g, f1u, f1d, f2g, f2u, f2d):
    b, t, _ = x.shape
    h = x + 0.5 * rmsnorm(swiglu(rmsnorm(x, n1a), f1g, f1u, f1d), n1b)
    n = rmsnorm(h, nma)
    proj = n @ w_in
    cuts = [W_QA, 2 * W_QA, 2 * W_QA + W_VA, 2 * W_QA + W_VA + W_B,
            2 * W_QA + W_VA + 2 * W_B, 2 * W_QA + W_VA + 3 * W_B, 2 * W_QA + W_VA + 3 * W_B + D_MODEL]
    q_a, k_a, v_a, q_b, k_b, v_b, g_a, g_b = jnp.split(proj, cuts, axis=-1)
    q_a = rope_partial(q_a.reshape(b, t, N_A, 2, DA_HEAD), pos)
    k_a = rope_partial(k_a.reshape(b, t, N_A, 2, DA_HEAD), pos)
    v_a = v_a.reshape(b, t, N_A, DA_V)
    q_b = q_b.reshape(b, t, N_B, DB_HEAD)
    k_b = k_b.reshape(b, t, N_B, DB_HEAD)
    v_b = v_b.reshape(b, t, N_B, DB_HEAD)
    new_rows = (k_a, v_a, k_b, v_b)
    if past is None:
        ka_all, va_all, kb_all, vb_all, k_pos = k_a, v_a, k_b, v_b, pos
    else:
        pka, pva, pkb, pvb = past
        ka_all = jnp.concatenate([pka, k_a], axis=1)
        va_all = jnp.concatenate([pva, v_a], axis=1)
        kb_all = jnp.concatenate([pkb, k_b], axis=1)
        vb_all = jnp.concatenate([pvb, v_b], axis=1)
        k_pos = jnp.concatenate([jnp.arange(pka.shape[1], dtype=jnp.int32), pos])
    lq1f, lk1f, lq2f, lk2f = (a.astype(jnp.float32) for a in (lq1, lk1, lq2, lk2))
    lam = jnp.exp(jnp.sum(lq1f * lk1f)) - jnp.exp(jnp.sum(lq2f * lk2f)) + lam_init
    fa = lambda qq, pp: diff_attn(qq, ka_all, va_all, pp, k_pos, lam, subln_g, lam_init)
    fb = lambda qq, pp: stick_breaking_attn(qq, kb_all, vb_all, pp, k_pos)
    if blocked:
        o_a = sweep_queries(fa, q_a, pos)
        o_b = sweep_queries(fb, q_b, pos)
    else:
        o_a = fa(q_a, pos)
        o_b = fb(q_b, pos)
    merged = jax.nn.sigmoid(g_a) * (o_a @ w_up_a) + jax.nn.sigmoid(g_b) * (o_b @ w_up_b)
    h = h + rmsnorm(merged @ w_o, nmb)
    y = h + 0.5 * rmsnorm(swiglu(rmsnorm(h, n2a), f2g, f2u, f2d), n2b)
    return y, new_rows


def setup_inputs(seed: int = 0) -> dict:
    key = jax.random.key(seed)
    ks = jax.random.split(key, 32)
    f32 = jnp.float32
    nrm = lambda k, shape, s: jax.random.normal(k, shape, f32) * s
    gain = lambda k, shape: 1.0 + 0.02 * jax.random.normal(k, shape, f32)
    return {
        "x_prompt": nrm(ks[0], (BATCH, SEQ, D_MODEL), 1.0),
        "x_sample": nrm(ks[1], (DEC_BATCH, DEC_SEQ, D_MODEL), 1.0),
        "cache_diff_k": nrm(ks[2], (DEPTH, DEC_BATCH, PAST_LEN, N_A, 2, DA_HEAD), 1.0),
        "cache_diff_v": nrm(ks[3], (DEPTH, DEC_BATCH, PAST_LEN, N_A, DA_V), 1.0),
        "cache_sb_k": nrm(ks[4], (DEPTH, DEC_BATCH, PAST_LEN, N_B, DB_HEAD), 1.0),
        "cache_sb_v": nrm(ks[5], (DEPTH, DEC_BATCH, PAST_LEN, N_B, DB_HEAD), 1.0),
        "w_in": nrm(ks[6], (DEPTH, D_MODEL, IN_COLS), D_MODEL ** -0.5),
        "w_up_a": nrm(ks[7], (DEPTH, W_VA, D_MODEL), W_VA ** -0.5),
        "w_up_b": nrm(ks[8], (DEPTH, W_B, D_MODEL), W_B ** -0.5),
        "w_o": nrm(ks[9], (DEPTH, D_MODEL, D_MODEL), D_MODEL ** -0.5),
        "lam_q1": nrm(ks[10], (DEPTH, DA_HEAD), 0.1),
        "lam_k1": nrm(ks[11], (DEPTH, DA_HEAD), 0.1),
        "lam_q2": nrm(ks[12], (DEPTH, DA_HEAD), 0.1),
        "lam_k2": nrm(ks[13], (DEPTH, DA_HEAD), 0.1),
        "subln_g": gain(ks[14], (DEPTH, DA_V)),
        "norm_ffn1_pre": gain(ks[15], (DEPTH, D_MODEL)),
        "norm_ffn1_post": gain(ks[16], (DEPTH, D_MODEL)),
        "norm_mix_pre": gain(ks[17], (DEPTH, D_MODEL)),
        "norm_mix_post": gain(ks[18], (DEPTH, D_MODEL)),
        "norm_ffn2_pre": gain(ks[19], (DEPTH, D_MODEL)),
        "norm_ffn2_post": gain(ks[20], (DEPTH, D_MODEL)),
        "ffn1_w_gate": nrm(ks[21], (DEPTH, D_MODEL, D_FF), D_MODEL ** -0.5),
        "ffn1_w_up": nrm(ks[22], (DEPTH, D_MODEL, D_FF), D_MODEL ** -0.5),
        "ffn1_w_down": nrm(ks[23], (DEPTH, D_FF, D_MODEL), D_FF ** -0.5),
        "ffn2_w_gate": nrm(ks[24], (DEPTH, D_MODEL, D_FF), D_MODEL ** -0.5),
        "ffn2_w_up": nrm(ks[25], (DEPTH, D_MODEL, D_FF), D_MODEL ** -0.5),
        "ffn2_w_down": nrm(ks[26], (DEPTH, D_FF, D_MODEL), D_FF ** -0.5),
    }


def reference(x_prompt, x_sample, cache_diff_k, cache_diff_v, cache_sb_k, cache_sb_v,
              w_in, w_up_a, w_up_b, w_o, lam_q1, lam_k1, lam_q2, lam_k2, subln_g,
              norm_ffn1_pre, norm_ffn1_post, norm_mix_pre, norm_mix_post, norm_ffn2_pre, norm_ffn2_post,
              ffn1_w_gate, ffn1_w_up, ffn1_w_down, ffn2_w_gate, ffn2_w_up, ffn2_w_down):
    pos_p = jnp.arange(x_prompt.shape[1], dtype=jnp.int32)
    pos_s = cache_diff_k.shape[2] + jnp.arange(x_sample.shape[1], dtype=jnp.int32)
    hp, hs = x_prompt, x_sample
    rows_p, rows_s = [], []
    for l in range(DEPTH):
        lam_init = 0.8 - 0.6 * math.exp(-0.3 * l)
        params = (w_in[l], w_up_a[l], w_up_b[l], w_o[l], lam_q1[l], lam_k1[l], lam_q2[l], lam_k2[l], subln_g[l],
                  norm_ffn1_pre[l], norm_ffn1_post[l], norm_mix_pre[l], norm_mix_post[l],
                  norm_ffn2_pre[l], norm_ffn2_post[l],
                  ffn1_w_gate[l], ffn1_w_up[l], ffn1_w_down[l], ffn2_w_gate[l], ffn2_w_up[l], ffn2_w_down[l])
        hp, rp = layer(hp, pos_p, None, True, lam_init, *params)
        past = (cache_diff_k[l], cache_diff_v[l], cache_sb_k[l], cache_sb_v[l])
        hs, rs = layer(hs, pos_s, past, False, lam_init, *params)
        rows_p.append(rp)
        rows_s.append(rs)
    pk_a = jnp.stack([r[0] for r in rows_p])
    pv_a = jnp.stack([r[1] for r in rows_p])
    pk_b = jnp.stack([r[2] for r in rows_p])
    pv_b = jnp.stack([r[3] for r in rows_p])
    sk_a = jnp.stack([r[0] for r in rows_s])
    sv_a = jnp.stack([r[1] for r in rows_s])
    sk_b = jnp.stack([r[2] for r in rows_s])
    sv_b = jnp.stack([r[3] for r in rows_s])
    return (hp, hs, pk_a, pv_a, pk_b, pv_b, sk_a, sv_a, sk_b, sv_b)
```

```python
import functools
import math

import jax
import jax.numpy as jnp
from jax import lax
from jax.experimental import pallas as pl
from jax.experimental.pallas import tpu as pltpu

F32 = jnp.float32
BF16 = jnp.bfloat16

EPS = 1e-6
CHUNK = 64
CHUNK_SHIFT = CHUNK.bit_length() - 1
assert CHUNK == 1 << CHUNK_SHIFT
N_HEADS = 8
HEAD_W = 128
MAP_W = 64
ROT_DIM = 16
ROPE_THETA = 500000.0
NEG = -1e30

V7X_LANES = 128
V7X_MXU_DIM = 256
V7X_VMEM_BYTES = 64 * 1024 * 1024
VMEM_LIMIT_BYTES = V7X_VMEM_BYTES - 8 * 1024 * 1024

FFN_TOKEN_TILE = 512
FFN_HIDDEN_TILE = 512
PROJ_TOKEN_TILE = 1024
GATE_TOKEN_TILE = 512
OUT_TOKEN_TILE = 256
ATTN_Q_TILE = 512
ATTN_K_TILE = 256


def _tile(n, target):
    t = min(n, target)
    while n % t:
        t -= 1
    return t


def _params(*sem):
    return pltpu.CompilerParams(dimension_semantics=sem, vmem_limit_bytes=VMEM_LIMIT_BYTES)


def _rms(x, g):
    return x * lax.rsqrt(jnp.mean(x * x, axis=-1, keepdims=True) + EPS) * g


def _ffn_kernel(n_f, has_next, x_ref, npre_ref, npost_ref, *refs):
    if has_next:
        nnext_ref, wg_ref, wu_ref, wd_ref, h_ref, xn_ref, xs_ref, acc_ref = refs
    else:
        wg_ref, wu_ref, wd_ref, h_ref, xs_ref, acc_ref = refs
    f = pl.program_id(1)

    @pl.when(f == 0)
    def _():
        xs_ref[...] = _rms(x_ref[...], npre_ref[...]).astype(BF16)

    xs = xs_ref[...]
    g = jnp.dot(xs, wg_ref[...], preferred_element_type=F32)
    u = jnp.dot(xs, wu_ref[...], preferred_element_type=F32)
    a = (g * jax.nn.sigmoid(g) * u).astype(BF16)
    d = jnp.dot(a, wd_ref[...], preferred_element_type=F32)

    @pl.when(f == 0)
    def _():
        acc_ref[...] = d

    @pl.when(f > 0)
    def _():
        acc_ref[...] += d

    @pl.when(f == n_f - 1)
    def _():
        h = x_ref[...] + 0.5 * _rms(acc_ref[...], npost_ref[...])
        h_ref[...] = h
        if has_next:
            xn_ref[...] = _rms(h, nnext_ref[...]).astype(BF16)


def _ffn(x, n_pre, n_post, n_next, wg, wu, wd):
    n, d = x.shape
    fp = wg.shape[1]
    tm = _tile(n, FFN_TOKEN_TILE)
    tf = _tile(fp, FFN_HIDDEN_TILE)
    n_f = fp // tf
    has_next = n_next is not None
    row = pl.BlockSpec((tm, d), lambda i, f: (i, 0))
    vec = pl.BlockSpec((1, d), lambda i, f: (0, 0))
    in_specs = [row, vec, vec] + ([vec] if has_next else []) + [
        pl.BlockSpec((d, tf), lambda i, f: (0, f)),
        pl.BlockSpec((d, tf), lambda i, f: (0, f)),
        pl.BlockSpec((tf, d), lambda i, f: (f, 0)),
    ]
    out_shape = [jax.ShapeDtypeStruct((n, d), F32)]
    out_specs = [row]
    if has_next:
        out_shape.append(jax.ShapeDtypeStruct((n, d), BF16))
        out_specs.append(row)
    args = [x, n_pre, n_post] + ([n_next] if has_next else []) + [wg, wu, wd]
    res = pl.pallas_call(
        functools.partial(_ffn_kernel, n_f, has_next),
        grid=(n // tm, n_f),
        in_specs=in_specs,
        out_specs=out_specs,
        out_shape=out_shape,
        scratch_shapes=[pltpu.VMEM((tm, d), BF16), pltpu.VMEM((tm, d), F32)],
        compiler_params=_params("parallel", "arbitrary"),
        name="ffn_next" if has_next else "ffn",
    )(*args)
    return res if has_next else res[0]


def _rope(r, c, sa, sb):
    heads = []
    for h in range(r.shape[1] // HEAD_W):
        x = r[:, h * HEAD_W:(h + 1) * HEAD_W]
        up = pltpu.roll(x, HEAD_W - ROT_DIM // 2, 1)
        dn = pltpu.roll(x, ROT_DIM // 2, 1)
        heads.append(x * c + up * sa + dn * sb)
    return jnp.concatenate(heads, axis=1)


def _proj_kernel(mode, scale, tk, x_ref, w_ref, *refs):
    r = jnp.dot(x_ref[...], w_ref[...], preferred_element_type=F32)
    if mode in ("q_rope", "k_rope"):
        c_ref, sa_ref, sb_ref = refs[:3]
        refs = refs[3:]
        r = _rope(r, c_ref[...], sa_ref[...], sb_ref[...])
    if mode in ("q", "q_rope"):
        (q_ref,) = refs
        q_ref[...] = (r * scale).astype(BF16)
    elif mode == "gate":
        (g_ref,) = refs
        g_ref[...] = jax.nn.sigmoid(r)
    elif mode in ("k", "k_rope"):
        k32_ref, k16_ref = refs
        k32_ref[...] = r
        k16_ref[...] = r.astype(BF16)
    elif mode == "v":
        (v32_ref,) = refs
        v32_ref[...] = r
    else:
        v32_ref, vt_ref = refs
        v32_ref[...] = r
        for h in range(N_HEADS):
            for jb in range(r.shape[0] // tk):
                blk = r[jb * tk:(jb + 1) * tk, h * HEAD_W:(h + 1) * HEAD_W]
                vt_ref[0, h, jb] = blk.T.astype(BF16)


def _proj(xn, w, mode, *, tables=None, scale=1.0, seq=None, tk=None, tile=PROJ_TOKEN_TILE):
    n, d = xn.shape
    width = w.shape[1]
    tm = _tile(n if seq is None else seq, tile)
    row = lambda wd: pl.BlockSpec((tm, wd), lambda i: (i, 0))
    in_specs = [row(d), pl.BlockSpec((d, width), lambda i: (0, 0))]
    args = [xn, w]
    if mode in ("q_rope", "k_rope"):
        in_specs += [row(HEAD_W)] * 3
        args += list(tables)
    if mode in ("q", "q_rope"):
        out_shape = [jax.ShapeDtypeStruct((n, width), BF16)]
        out_specs = [row(width)]
    elif mode in ("gate", "v"):
        out_shape = [jax.ShapeDtypeStruct((n, width), F32)]
        out_specs = [row(width)]
    elif mode in ("k", "k_rope"):
        out_shape = [jax.ShapeDtypeStruct((n, width), F32), jax.ShapeDtypeStruct((n, width), BF16)]
        out_specs = [row(width), row(width)]
    else:
        assert mode == "v_t" and tm % tk == 0 and seq % tm == 0
        per_seq = seq // tm
        out_shape = [jax.ShapeDtypeStruct((n, width), F32),
                     jax.ShapeDtypeStruct((n // seq, N_HEADS, seq // tk, HEAD_W, tk), BF16)]
        out_specs = [row(width),
                     pl.BlockSpec((1, N_HEADS, tm // tk, HEAD_W, tk),
                                  lambda i: (i // per_seq, 0, i % per_seq, 0, 0))]
    res = pl.pallas_call(
        functools.partial(_proj_kernel, mode, scale, tk),
        grid=(n // tm,),
        in_specs=in_specs,
        out_specs=out_specs,
        out_shape=out_shape,
        compiler_params=_params("parallel"),
        name="proj_" + mode,
    )(*args)
    return res[0] if len(res) == 1 else res


def _kv_range(kind, i, tq, tk, n_k, q_off):
    qmin = q_off + i * tq
    qmax = qmin + tq - 1
    if kind == "diff":
        n_int = ((qmin // CHUNK + 1) * CHUNK) // tk
        last = ((qmax // CHUNK + 1) * CHUNK - 1) // tk
    else:
        n_int = qmin // tk
        last = (qmax - 1) // tk
    return jnp.minimum(n_int, n_k), jnp.minimum(last + 1, n_k), qmin


def _diff_attn_kernel(tq, tk, n_k, q_off, lam_init,
                      q_ref, k_ref, vt_ref, lq1_ref, lk1_ref, lq2_ref, lk2_ref, g_ref,
                      o_ref, acc0_ref, acc1_ref):
    n_int, n_tot, qmin = _kv_range("diff", pl.program_id(2), tq, tk, n_k, q_off)
    q = q_ref[0]
    lane = lax.broadcasted_iota(jnp.int32, (1, HEAD_W), 1)
    zero = jnp.zeros_like(q)
    q_maps = (jnp.where(lane < MAP_W, q, zero), jnp.where(lane >= MAP_W, q, zero))
    q_chunk = (qmin + lax.broadcasted_iota(jnp.int32, (1, tq), 1)) >> CHUNK_SHIFT
    accs = (acc0_ref, acc1_ref)
    acc0_ref[...] = jnp.zeros_like(acc0_ref)
    acc1_ref[...] = jnp.zeros_like(acc1_ref)

    def step(masked, j, carry):
        k0 = pl.multiple_of(j * tk, tk)
        kb = k_ref[0, pl.ds(k0, tk), :]
        vtb = vt_ref[0, 0, j]
        if masked:
            k_chunk = (k0 + lax.broadcasted_iota(jnp.int32, (tk, 1), 0)) >> CHUNK_SHIFT
            valid = k_chunk <= q_chunk
        new = []
        for mi in range(2):
            m, l = carry[2 * mi], carry[2 * mi + 1]
            s = lax.dot_general(kb, q_maps[mi], (((1,), (1,)), ((), ())),
                                preferred_element_type=F32)
            if masked:
                s = jnp.where(valid, s, NEG)
            m_new = jnp.maximum(m, jnp.max(s, axis=0, keepdims=True))
            alpha = jnp.exp(m - m_new)
            p = jnp.exp(s - m_new)
            l_new = alpha * l + jnp.sum(p, axis=0, keepdims=True)
            accs[mi][...] = alpha * accs[mi][...] + jnp.dot(
                vtb, p.astype(BF16), preferred_element_type=F32)
            new += [m_new, l_new]
        return tuple(new)

    m_init = jnp.full((1, tq), NEG, F32)
    l_init = jnp.zeros((1, tq), F32)
    carry = (m_init, l_init, m_init, l_init)
    carry = lax.fori_loop(0, n_int, functools.partial(step, False), carry)
    _, l0, _, l1 = lax.fori_loop(n_int, n_tot, functools.partial(step, True), carry)

    lam = (jnp.exp(jnp.sum(lq1_ref[...] * lk1_ref[...], keepdims=True))
           - jnp.exp(jnp.sum(lq2_ref[...] * lk2_ref[...], keepdims=True)) + lam_init)
    ot = acc0_ref[...] * (1.0 / l0) - lam * (acc1_ref[...] * (1.0 / l1))
    ms = jnp.mean(ot * ot, axis=0, keepdims=True)
    ot = ot * lax.rsqrt(ms + EPS) * g_ref[...] * (1.0 - lam_init)
    o_ref[0] = ot.T.astype(BF16)


def _sb_attn_kernel(tq, tk, n_k, q_off, scale, q_ref, k_ref, vt_ref, o_ref, acc_ref):
    n_int, n_tot, qmin = _kv_range("sb", pl.program_id(2), tq, tk, n_k, q_off)
    q = q_ref[0]
    q_pos = qmin + lax.broadcasted_iota(jnp.int32, (1, tq), 1)
    suffix = (lax.broadcasted_iota(jnp.int32, (tk, tk), 1)
              >= lax.broadcasted_iota(jnp.int32, (tk, tk), 0)).astype(BF16)
    acc_ref[...] = jnp.zeros_like(acc_ref)

    def step(masked, t, later):
        j = n_tot - 1 - t
        k0 = pl.multiple_of(j * tk, tk)
        kb = k_ref[0, pl.ds(k0, tk), :]
        vtb = vt_ref[0, 0, j]
        z = lax.dot_general(kb, q, (((1,), (1,)), ((), ())), preferred_element_type=F32) * scale
        sp = jnp.maximum(z, 0.0) + jnp.log(1.0 + jnp.exp(-jnp.abs(z)))
        if masked:
            valid = (k0 + lax.broadcasted_iota(jnp.int32, (tk, 1), 0)) < q_pos
            sp = jnp.where(valid, sp, 0.0)
        hi = sp.astype(BF16)
        lo = (sp - hi.astype(F32)).astype(BF16)
        tail = (jnp.dot(suffix, hi, preferred_element_type=F32)
                + jnp.dot(suffix, lo, preferred_element_type=F32))
        logit = z - tail - later
        if masked:
            logit = jnp.where(valid, logit, NEG)
        a = jnp.exp(logit)
        acc_ref[...] += jnp.dot(vtb, a.astype(BF16), preferred_element_type=F32)
        return later + tail[0:1, :]

    later = jnp.zeros((1, tq), F32)
    later = lax.fori_loop(0, n_tot - n_int, functools.partial(step, True), later)
    lax.fori_loop(n_tot - n_int, n_tot, functools.partial(step, False), later)
    o_ref[0] = acc_ref[...].T.astype(BF16)


def _attention(kind, q, k, vt, q_off, extra, cfg):
    b, t_q, width = q.shape
    n_k, tk = vt.shape[2], vt.shape[4]
    t_k = k.shape[1]
    assert t_k == n_k * tk
    tq = _tile(t_q, ATTN_Q_TILE)
    q_spec = pl.BlockSpec((1, tq, HEAD_W), lambda bi, h, i: (bi, i, h))
    in_specs = [q_spec,
                pl.BlockSpec((1, t_k, HEAD_W), lambda bi, h, i: (bi, 0, h)),
                pl.BlockSpec((1, 1, n_k, HEAD_W, tk), lambda bi, h, i: (bi, h, 0, 0, 0))]
    for e in extra:
        in_specs.append(pl.BlockSpec(e.shape, lambda bi, h, i: (0, 0)))
    if kind == "diff":
        body = functools.partial(_diff_attn_kernel, tq, tk, n_k, q_off, cfg)
        scratch = [pltpu.VMEM((HEAD_W, tq), F32), pltpu.VMEM((HEAD_W, tq), F32)]
    else:
        body = functools.partial(_sb_attn_kernel, tq, tk, n_k, q_off, cfg)
        scratch = [pltpu.VMEM((HEAD_W, tq), F32)]
    return pl.pallas_call(
        body,
        grid=(b, N_HEADS, t_q // tq),
        in_specs=in_specs,
        out_specs=q_spec,
        out_shape=jax.ShapeDtypeStruct((b, t_q, width), BF16),
        scratch_shapes=scratch,
        compiler_params=_params("parallel", "parallel", "arbitrary"),
        name=kind + "_attn",
    )(q, k, vt, *extra)


def _out_kernel(oa_ref, ob_ref, ga_ref, gb_ref, h_ref, wua_ref, wub_ref, wo_ref, n_ref, o_ref):
    pa = jnp.dot(oa_ref[...], wua_ref[...], preferred_element_type=F32)
    pb = jnp.dot(ob_ref[...], wub_ref[...], preferred_element_type=F32)
    merged = (ga_ref[...] * pa + gb_ref[...] * pb).astype(BF16)
    r = jnp.dot(merged, wo_ref[...], preferred_element_type=F32)
    o_ref[...] = h_ref[...] + _rms(r, n_ref[...])


def _merge_out(oa, ob, ga, gb, h, wua, wub, wo, n_post):
    n, d = h.shape
    w_attn = oa.shape[1]
    tm = _tile(n, OUT_TOKEN_TILE)
    row = lambda wd: pl.BlockSpec((tm, wd), lambda i: (i, 0))
    const = lambda shape: pl.BlockSpec(shape, lambda i: (0, 0), pipeline_mode=pl.Buffered(1))
    return pl.pallas_call(
        _out_kernel,
        grid=(n // tm,),
        in_specs=[row(w_attn), row(w_attn), row(d), row(d), row(d),
                  const(wua.shape), const(wub.shape), const(wo.shape), const(n_post.shape)],
        out_specs=row(d),
        out_shape=jax.ShapeDtypeStruct((n, d), F32),
        compiler_params=_params("parallel"),
        name="merge_out",
    )(oa, ob, ga, gb, h, wua, wub, wo, n_post)


def _rope_tables(pos):
    half = ROT_DIM // 2
    inv_freq = jnp.power(ROPE_THETA, -jnp.arange(0, ROT_DIM, 2, dtype=F32) / ROT_DIM)
    ang = pos.astype(F32)[:, None] * inv_freq[None, :]
    cos, sin = jnp.cos(ang), jnp.sin(ang)
    t = pos.shape[0]
    ones = jnp.ones((t, MAP_W - ROT_DIM), F32)
    zeros = jnp.zeros((t, MAP_W - ROT_DIM), F32)
    zh = jnp.zeros((t, half), F32)
    c = jnp.concatenate([cos, cos, ones], axis=1)
    sa = jnp.concatenate([-sin, zh, zeros], axis=1)
    sb = jnp.concatenate([zh, sin, zeros], axis=1)
    return tuple(jnp.concatenate([m, m], axis=1) for m in (c, sa, sb))


def _pad_ffn(w, axis):
    f = w.shape[axis]
    fp = -(-f // FFN_HIDDEN_TILE) * FFN_HIDDEN_TILE
    pad = [(0, 0), (0, 0)]
    pad[axis] = (0, fp - f)
    return jnp.pad(w.astype(BF16), pad)


def _layer(x, pos, past, lam_init, p):
    b, t, d = x.shape
    n = b * t
    w_a = N_HEADS * HEAD_W
    vec = lambda v: v.reshape(1, -1)
    w_in = p["w_in"].astype(BF16)
    seg = lambda lo, hi: w_in[:, lo:hi]

    h1, xn = _ffn(x.reshape(n, d), vec(p["n1a"]), vec(p["n1b"]), vec(p["nma"]),
                  _pad_ffn(p["f1g"], 1), _pad_ffn(p["f1u"], 1), _pad_ffn(p["f1d"], 0))

    tables = tuple(jnp.tile(m, (b, 1)) for m in _rope_tables(pos))
    q_a = _proj(xn, seg(0, w_a), "q_rope", tables=tables, scale=MAP_W ** -0.5)
    ka32, ka16 = _proj(xn, seg(w_a, 2 * w_a), "k_rope", tables=tables)
    q_b = _proj(xn, seg(3 * w_a, 4 * w_a), "q")
    kb32, kb16 = _proj(xn, seg(4 * w_a, 5 * w_a), "k")
    g_a = _proj(xn, seg(6 * w_a, 6 * w_a + d), "gate", tile=GATE_TOKEN_TILE)
    g_b = _proj(xn, seg(6 * w_a + d, 6 * w_a + 2 * d), "gate", tile=GATE_TOKEN_TILE)

    shape3 = lambda a: a.reshape(b, t, w_a)
    if past is None:
        tk = _tile(t, ATTN_K_TILE)
        va32, vat = _proj(xn, seg(2 * w_a, 3 * w_a), "v_t", seq=t, tk=tk)
        vb32, vbt = _proj(xn, seg(5 * w_a, 6 * w_a), "v_t", seq=t, tk=tk)
        q_off = 0
        qa3, qb3, ka3, kb3 = shape3(q_a), shape3(q_b), shape3(ka16), shape3(kb16)
    else:
        va32 = _proj(xn, seg(2 * w_a, 3 * w_a), "v")
        vb32 = _proj(xn, seg(5 * w_a, 6 * w_a), "v")
        pka, pva, pkb, pvb = past
        q_off = pka.shape[1]
        t_k = q_off + t
        t_kp = -(-t_k // V7X_LANES) * V7X_LANES
        t_qp = -(-t // V7X_LANES) * V7X_LANES

        def keys(pk, k16):
            full = jnp.concatenate([pk.reshape(b, q_off, w_a).astype(BF16), shape3(k16)], axis=1)
            return jnp.pad(full, ((0, 0), (0, t_kp - t_k), (0, 0)))

        def values_t(pv, v32):
            full = jnp.concatenate([pv.reshape(b, q_off, N_HEADS, HEAD_W),
                                    v32.reshape(b, t, N_HEADS, HEAD_W)], axis=1).astype(BF16)
            full = jnp.pad(full.transpose(0, 2, 3, 1), ((0, 0), (0, 0), (0, 0), (0, t_kp - t_k)))
            return full[:, :, None]

        padq = lambda a: jnp.pad(shape3(a), ((0, 0), (0, t_qp - t), (0, 0)))
        qa3, qb3 = padq(q_a), padq(q_b)
        ka3, kb3 = keys(pka, ka16), keys(pkb, kb16)
        vat, vbt = values_t(pva, va32), values_t(pvb, vb32)

    o_a = _attention("diff", qa3, ka3, vat, q_off,
                     [vec(p["lq1"]), vec(p["lk1"]), vec(p["lq2"]), vec(p["lk2"]),
                      p["subln_g"].reshape(-1, 1)], lam_init)
    o_b = _attention("sb", qb3, kb3, vbt, q_off, [], HEAD_W ** -0.5)
    o_a = o_a[:, :t].reshape(n, w_a)
    o_b = o_b[:, :t].reshape(n, w_a)

    h2 = _merge_out(o_a, o_b, g_a, g_b, h1, p["w_up_a"].astype(BF16), p["w_up_b"].astype(BF16),
                    p["w_o"].astype(BF16), vec(p["nmb"]))
    y = _ffn(h2, vec(p["n2a"]), vec(p["n2b"]), None,
             _pad_ffn(p["f2g"], 1), _pad_ffn(p["f2u"], 1), _pad_ffn(p["f2d"], 0))
    rows = (ka32.reshape(b, t, N_HEADS, 2, MAP_W), va32.reshape(b, t, N_HEADS, HEAD_W),
            kb32.reshape(b, t, N_HEADS, HEAD_W), vb32.reshape(b, t, N_HEADS, HEAD_W))
    return y.reshape(b, t, d), rows


def _stack(xs):
    return xs[0][None] if len(xs) == 1 else jnp.stack(xs)


def kernel(x_prompt, x_sample, cache_diff_k, cache_diff_v, cache_sb_k, cache_sb_v, w_in, w_up_a, w_up_b, w_o, lam_q1, lam_k1, lam_q2, lam_k2, subln_g, norm_ffn1_pre, norm_ffn1_post, norm_mix_pre, norm_mix_post, norm_ffn2_pre, norm_ffn2_post, ffn1_w_gate, ffn1_w_up, ffn1_w_down, ffn2_w_gate, ffn2_w_up, ffn2_w_down):
    depth = w_in.shape[0]
    pos_p = jnp.arange(x_prompt.shape[1], dtype=jnp.int32)
    pos_s = cache_diff_k.shape[2] + jnp.arange(x_sample.shape[1], dtype=jnp.int32)
    hp, hs = x_prompt, x_sample
    rows_p, rows_s = [], []
    for l in range(depth):
        lam_init = 0.8 - 0.6 * math.exp(-0.3 * l)
        p = dict(w_in=w_in[l], w_up_a=w_up_a[l], w_up_b=w_up_b[l], w_o=w_o[l],
                 lq1=lam_q1[l], lk1=lam_k1[l], lq2=lam_q2[l], lk2=lam_k2[l], subln_g=subln_g[l],
                 n1a=norm_ffn1_pre[l], n1b=norm_ffn1_post[l], nma=norm_mix_pre[l], nmb=norm_mix_post[l],
                 n2a=norm_ffn2_pre[l], n2b=norm_ffn2_post[l],
                 f1g=ffn1_w_gate[l], f1u=ffn1_w_up[l], f1d=ffn1_w_down[l],
                 f2g=ffn2_w_gate[l], f2u=ffn2_w_up[l], f2d=ffn2_w_down[l])
        hp, rp = _layer(hp, pos_p, None, lam_init, p)
        past = (cache_diff_k[l], cache_diff_v[l], cache_sb_k[l], cache_sb_v[l])
        hs, rs = _layer(hs, pos_s, past, lam_init, p)
        rows_p.append(rp)
        rows_s.append(rs)
    outs_p = [_stack([r[i] for r in rows_p]) for i in range(4)]
    outs_s = [_stack([r[i] for r in rows_s]) for i in range(4)]
    return (hp, hs, *outs_p, *outs_s)
```

```python
import functools
import math

import jax
import jax.numpy as jnp
from jax import lax
from jax.experimental import pallas as pl
from jax.experimental.pallas import tpu as pltpu

F32 = jnp.float32
BF16 = jnp.bfloat16

EPS = 1e-6
CHUNK = 64
CHUNK_SHIFT = CHUNK.bit_length() - 1
assert CHUNK == 1 << CHUNK_SHIFT
N_HEADS = 8
HEAD_W = 128
MAP_W = 64
ROT_DIM = 16
ROPE_THETA = 500000.0
NEG = -1e30

V7X_LANES = 128
V7X_MXU_DIM = 256
V7X_VMEM_BYTES = 64 * 1024 * 1024
VMEM_LIMIT_BYTES = V7X_VMEM_BYTES - 8 * 1024 * 1024

FFN_TOKEN_TILE = 512
FFN_HIDDEN_TILE = 512
PROJ_TOKEN_TILE = 1024
GATE_TOKEN_TILE = 512
OUT_TOKEN_TILE = 256
ATTN_Q_TILE = 512
ATTN_K_TILE = 256
ATTN_HEADS_PER_STEP = 4
LOG2E = math.log2(math.e)


def _tile(n, target):
    t = min(n, target)
    while n % t:
        t -= 1
    return t


def _params(*sem):
    return pltpu.CompilerParams(dimension_semantics=sem, vmem_limit_bytes=VMEM_LIMIT_BYTES)


def _rms(x, g):
    return x * lax.rsqrt(jnp.mean(x * x, axis=-1, keepdims=True) + EPS) * g


def _ffn_kernel(n_f, has_next, x_ref, npre_ref, npost_ref, *refs):
    if has_next:
        nnext_ref, wg_ref, wu_ref, wd_ref, h_ref, xn_ref, xs_ref, acc_ref = refs
    else:
        wg_ref, wu_ref, wd_ref, h_ref, xs_ref, acc_ref = refs
    f = pl.program_id(1)

    @pl.when(f == 0)
    def _():
        xs_ref[...] = _rms(x_ref[...], npre_ref[...]).astype(BF16)

    xs = xs_ref[...]
    g = jnp.dot(xs, wg_ref[...], preferred_element_type=F32)
    u = jnp.dot(xs, wu_ref[...], preferred_element_type=F32)
    a = (g * jax.nn.sigmoid(g) * u).astype(BF16)
    d = jnp.dot(a, wd_ref[...], preferred_element_type=F32)

    @pl.when(f == 0)
    def _():
        acc_ref[...] = d

    @pl.when(f > 0)
    def _():
        acc_ref[...] += d

    @pl.when(f == n_f - 1)
    def _():
        h = x_ref[...] + 0.5 * _rms(acc_ref[...], npost_ref[...])
        h_ref[...] = h
        if has_next:
            xn_ref[...] = _rms(h, nnext_ref[...]).astype(BF16)


def _ffn(x, n_pre, n_post, n_next, wg, wu, wd):
    n, d = x.shape
    fp = wg.shape[1]
    tm = _tile(n, FFN_TOKEN_TILE)
    tf = _tile(fp, FFN_HIDDEN_TILE)
    n_f = fp // tf
    has_next = n_next is not None
    row = pl.BlockSpec((tm, d), lambda i, f: (i, 0))
    vec = pl.BlockSpec((1, d), lambda i, f: (0, 0))
    in_specs = [row, vec, vec] + ([vec] if has_next else []) + [
        pl.BlockSpec((d, tf), lambda i, f: (0, f)),
        pl.BlockSpec((d, tf), lambda i, f: (0, f)),
        pl.BlockSpec((tf, d), lambda i, f: (f, 0)),
    ]
    out_shape = [jax.ShapeDtypeStruct((n, d), F32)]
    out_specs = [row]
    if has_next:
        out_shape.append(jax.ShapeDtypeStruct((n, d), BF16))
        out_specs.append(row)
    args = [x, n_pre, n_post] + ([n_next] if has_next else []) + [wg, wu, wd]
    res = pl.pallas_call(
        functools.partial(_ffn_kernel, n_f, has_next),
        grid=(n // tm, n_f),
        in_specs=in_specs,
        out_specs=out_specs,
        out_shape=out_shape,
        scratch_shapes=[pltpu.VMEM((tm, d), BF16), pltpu.VMEM((tm, d), F32)],
        compiler_params=_params("parallel", "arbitrary"),
        name="ffn_next" if has_next else "ffn",
    )(*args)
    return res if has_next else res[0]


def _rope(r, c, sa, sb):
    heads = []
    for h in range(r.shape[1] // HEAD_W):
        x = r[:, h * HEAD_W:(h + 1) * HEAD_W]
        up = pltpu.roll(x, HEAD_W - ROT_DIM // 2, 1)
        dn = pltpu.roll(x, ROT_DIM // 2, 1)
        heads.append(x * c + up * sa + dn * sb)
    return jnp.concatenate(heads, axis=1)


def _proj_kernel(mode, scale, tk, x_ref, w_ref, *refs):
    r = jnp.dot(x_ref[...], w_ref[...], preferred_element_type=F32)
    if mode in ("q_rope", "k_rope"):
        c_ref, sa_ref, sb_ref = refs[:3]
        refs = refs[3:]
        r = _rope(r, c_ref[...], sa_ref[...], sb_ref[...])
    if mode in ("q", "q_rope"):
        (q_ref,) = refs
        q_ref[...] = (r * scale).astype(BF16)
    elif mode == "gate":
        (g_ref,) = refs
        g_ref[...] = jax.nn.sigmoid(r)
    elif mode in ("k", "k_rope"):
        k32_ref, k16_ref = refs
        k32_ref[...] = r
        k16_ref[...] = r.astype(BF16)
    elif mode == "v":
        (v32_ref,) = refs
        v32_ref[...] = r
    else:
        v32_ref, vt_ref = refs
        v32_ref[...] = r
        for h in range(N_HEADS):
            for jb in range(r.shape[0] // tk):
                blk = r[jb * tk:(jb + 1) * tk, h * HEAD_W:(h + 1) * HEAD_W]
                vt_ref[0, h, jb] = blk.T.astype(BF16)


def _proj(xn, w, mode, *, tables=None, scale=1.0, seq=None, tk=None, tile=PROJ_TOKEN_TILE):
    n, d = xn.shape
    width = w.shape[1]
    tm = _tile(n if seq is None else seq, tile)
    row = lambda wd: pl.BlockSpec((tm, wd), lambda i: (i, 0))
    in_specs = [row(d), pl.BlockSpec((d, width), lambda i: (0, 0))]
    args = [xn, w]
    if mode in ("q_rope", "k_rope"):
        in_specs += [row(HEAD_W)] * 3
        args += list(tables)
    if mode in ("q", "q_rope"):
        out_shape = [jax.ShapeDtypeStruct((n, width), BF16)]
        out_specs = [row(width)]
    elif mode in ("gate", "v"):
        out_shape = [jax.ShapeDtypeStruct((n, width), F32)]
        out_specs = [row(width)]
    elif mode in ("k", "k_rope"):
        out_shape = [jax.ShapeDtypeStruct((n, width), F32), jax.ShapeDtypeStruct((n, width), BF16)]
        out_specs = [row(width), row(width)]
    else:
        assert mode == "v_t" and tm % tk == 0 and seq % tm == 0
        per_seq = seq // tm
        out_shape = [jax.ShapeDtypeStruct((n, width), F32),
                     jax.ShapeDtypeStruct((n // seq, N_HEADS, seq // tk, HEAD_W, tk), BF16)]
        out_specs = [row(width),
                     pl.BlockSpec((1, N_HEADS, tm // tk, HEAD_W, tk),
                                  lambda i: (i // per_seq, 0, i % per_seq, 0, 0))]
    res = pl.pallas_call(
        functools.partial(_proj_kernel, mode, scale, tk),
        grid=(n // tm,),
        in_specs=in_specs,
        out_specs=out_specs,
        out_shape=out_shape,
        compiler_params=_params("parallel"),
        name="proj_" + mode,
    )(*args)
    return res[0] if len(res) == 1 else res


def _kv_range(kind, i, tq, tk, n_k, q_off):
    qmin = q_off + i * tq
    qmax = qmin + tq - 1
    if kind == "diff":
        n_int = ((qmin // CHUNK + 1) * CHUNK) // tk
        last = ((qmax // CHUNK + 1) * CHUNK - 1) // tk
    else:
        n_int = qmin // tk
        last = (qmax - 1) // tk
    return jnp.minimum(n_int, n_k), jnp.minimum(last + 1, n_k), qmin


def _diff_attn_kernel(tq, tk, n_k, q_off, lam_init, hb,
                      q_ref, k_ref, vt_ref, lq1_ref, lk1_ref, lq2_ref, lk2_ref, g_ref,
                      o_ref, qm_ref, acc_ref):
    n_int, n_tot, qmin = _kv_range("diff", pl.program_id(2), tq, tk, n_k, q_off)
    lane = lax.broadcasted_iota(jnp.int32, (1, HEAD_W), 1)
    for h in range(hb):
        q = q_ref[0, :, h * HEAD_W:(h + 1) * HEAD_W]
        zero = jnp.zeros_like(q)
        qm_ref[2 * h] = jnp.where(lane < MAP_W, q, zero)
        qm_ref[2 * h + 1] = jnp.where(lane >= MAP_W, q, zero)
    q_chunk = (qmin + lax.broadcasted_iota(jnp.int32, (1, tq), 1)) >> CHUNK_SHIFT
    acc_ref[...] = jnp.zeros_like(acc_ref)

    def step(masked, j, carry):
        k0 = pl.multiple_of(j * tk, tk)
        if masked:
            k_chunk = (k0 + lax.broadcasted_iota(jnp.int32, (tk, 1), 0)) >> CHUNK_SHIFT
            valid = k_chunk <= q_chunk
        chains = range(2 * hb)
        scores = [lax.dot_general(k_ref[0, pl.ds(k0, tk), (c // 2) * HEAD_W:(c // 2 + 1) * HEAD_W],
                                  qm_ref[c], (((1,), (1,)), ((), ())),
                                  preferred_element_type=F32) for c in chains]
        new, probs, alphas = [], [], []
        for c in chains:
            m, l = carry[2 * c], carry[2 * c + 1]
            s = jnp.where(valid, scores[c], NEG) if masked else scores[c]
            m_new = jnp.maximum(m, jnp.max(s, axis=0, keepdims=True))
            alpha = jnp.exp2(m - m_new)
            p = jnp.exp2(s - m_new)
            new += [m_new, alpha * l + jnp.sum(p, axis=0, keepdims=True)]
            probs.append(p.astype(BF16))
            alphas.append(alpha)
        for c in chains:
            acc_ref[c] = alphas[c] * acc_ref[c] + jnp.dot(
                vt_ref[0, c // 2, j], probs[c], preferred_element_type=F32)
        return tuple(new)

    m_init = jnp.full((1, tq), NEG, F32)
    l_init = jnp.zeros((1, tq), F32)
    carry = (m_init, l_init) * (2 * hb)
    carry = lax.fori_loop(0, n_int, functools.partial(step, False), carry)
    carry = lax.fori_loop(n_int, n_tot, functools.partial(step, True), carry)

    lam = (jnp.exp(jnp.sum(lq1_ref[...] * lk1_ref[...], keepdims=True))
           - jnp.exp(jnp.sum(lq2_ref[...] * lk2_ref[...], keepdims=True)) + lam_init)
    for h in range(hb):
        l0, l1 = carry[4 * h + 1], carry[4 * h + 3]
        ot = acc_ref[2 * h] * (1.0 / l0) - lam * (acc_ref[2 * h + 1] * (1.0 / l1))
        ms = jnp.mean(ot * ot, axis=0, keepdims=True)
        ot = ot * lax.rsqrt(ms + EPS) * g_ref[...] * (1.0 - lam_init)
        o_ref[0, :, h * HEAD_W:(h + 1) * HEAD_W] = ot.T.astype(BF16)


def _sb_attn_kernel(tq, tk, n_k, q_off, scale, hb, q_ref, k_ref, vt_ref, o_ref, acc_ref):
    n_int, n_tot, qmin = _kv_range("sb", pl.program_id(2), tq, tk, n_k, q_off)
    q_pos = qmin + lax.broadcasted_iota(jnp.int32, (1, tq), 1)
    suffix = (lax.broadcasted_iota(jnp.int32, (tk, tk), 1)
              >= lax.broadcasted_iota(jnp.int32, (tk, tk), 0)).astype(BF16)
    suffix2 = jnp.concatenate([suffix, suffix], axis=1)
    acc_ref[...] = jnp.zeros_like(acc_ref)

    def step(masked, t, later):
        j = n_tot - 1 - t
        k0 = pl.multiple_of(j * tk, tk)
        if masked:
            valid = (k0 + lax.broadcasted_iota(jnp.int32, (tk, 1), 0)) < q_pos
        heads = range(hb)
        ws = [lax.dot_general(k_ref[0, pl.ds(k0, tk), h * HEAD_W:(h + 1) * HEAD_W],
                              q_ref[0, :, h * HEAD_W:(h + 1) * HEAD_W], (((1,), (1,)), ((), ())),
                              preferred_element_type=F32) * (scale * LOG2E) for h in heads]
        tails = []
        for h in heads:
            sp = jnp.maximum(ws[h], 0.0) + jnp.log(1.0 + jnp.exp2(-jnp.abs(ws[h]))) * LOG2E
            if masked:
                sp = jnp.where(valid, sp, 0.0)
            hi = sp.astype(BF16)
            lo = (sp - hi.astype(F32)).astype(BF16)
            tails.append(jnp.dot(suffix2, jnp.concatenate([hi, lo], axis=0),
                                 preferred_element_type=F32))
        weights = []
        for h in heads:
            logit = ws[h] - tails[h] - later[h]
            if masked:
                logit = jnp.where(valid, logit, NEG)
            weights.append(jnp.exp2(logit).astype(BF16))
        for h in heads:
            acc_ref[h] += jnp.dot(vt_ref[0, h, j], weights[h], preferred_element_type=F32)
        return tuple(later[h] + tails[h][0:1, :] for h in heads)

    later = (jnp.zeros((1, tq), F32),) * hb
    later = lax.fori_loop(0, n_tot - n_int, functools.partial(step, True), later)
    lax.fori_loop(n_tot - n_int, n_tot, functools.partial(step, False), later)
    for h in range(hb):
        o_ref[0, :, h * HEAD_W:(h + 1) * HEAD_W] = acc_ref[h].T.astype(BF16)


def _attention(kind, q, k, vt, q_off, extra, cfg):
    b, t_q, width = q.shape
    n_k, tk = vt.shape[2], vt.shape[4]
    t_k = k.shape[1]
    assert t_k == n_k * tk
    tq = _tile(t_q, ATTN_Q_TILE)
    hb = ATTN_HEADS_PER_STEP
    q_spec = pl.BlockSpec((1, tq, hb * HEAD_W), lambda bi, h, i: (bi, i, h))
    in_specs = [q_spec,
                pl.BlockSpec((1, t_k, hb * HEAD_W), lambda bi, h, i: (bi, 0, h)),
                pl.BlockSpec((1, hb, n_k, HEAD_W, tk), lambda bi, h, i: (bi, h, 0, 0, 0))]
    for e in extra:
        in_specs.append(pl.BlockSpec(e.shape, lambda bi, h, i: (0, 0)))
    if kind == "diff":
        body = functools.partial(_diff_attn_kernel, tq, tk, n_k, q_off, cfg, hb)
        scratch = [pltpu.VMEM((2 * hb, tq, HEAD_W), BF16), pltpu.VMEM((2 * hb, HEAD_W, tq), F32)]
    else:
        body = functools.partial(_sb_attn_kernel, tq, tk, n_k, q_off, cfg, hb)
        scratch = [pltpu.VMEM((hb, HEAD_W, tq), F32)]
    return pl.pallas_call(
        body,
        grid=(b, N_HEADS // hb, t_q // tq),
        in_specs=in_specs,
        out_specs=q_spec,
        out_shape=jax.ShapeDtypeStruct((b, t_q, width), BF16),
        scratch_shapes=scratch,
        compiler_params=_params("parallel", "parallel", "arbitrary"),
        name=kind + "_attn",
    )(q, k, vt, *extra)


def _out_kernel(oa_ref, ob_ref, ga_ref, gb_ref, h_ref, wua_ref, wub_ref, wo_ref, n_ref, o_ref):
    pa = jnp.dot(oa_ref[...], wua_ref[...], preferred_element_type=F32)
    pb = jnp.dot(ob_ref[...], wub_ref[...], preferred_element_type=F32)
    merged = (ga_ref[...] * pa + gb_ref[...] * pb).astype(BF16)
    r = jnp.dot(merged, wo_ref[...], preferred_element_type=F32)
    o_ref[...] = h_ref[...] + _rms(r, n_ref[...])


def _merge_out(oa, ob, ga, gb, h, wua, wub, wo, n_post):
    n, d = h.shape
    w_attn = oa.shape[1]
    tm = _tile(n, OUT_TOKEN_TILE)
    row = lambda wd: pl.BlockSpec((tm, wd), lambda i: (i, 0))
    const = lambda shape: pl.BlockSpec(shape, lambda i: (0, 0), pipeline_mode=pl.Buffered(1))
    return pl.pallas_call(
        _out_kernel,
        grid=(n // tm,),
        in_specs=[row(w_attn), row(w_attn), row(d), row(d), row(d),
                  const(wua.shape), const(wub.shape), const(wo.shape), const(n_post.shape)],
        out_specs=row(d),
        out_shape=jax.ShapeDtypeStruct((n, d), F32),
        compiler_params=_params("parallel"),
        name="merge_out",
    )(oa, ob, ga, gb, h, wua, wub, wo, n_post)


def _rope_tables(pos):
    half = ROT_DIM // 2
    inv_freq = jnp.power(ROPE_THETA, -jnp.arange(0, ROT_DIM, 2, dtype=F32) / ROT_DIM)
    ang = pos.astype(F32)[:, None] * inv_freq[None, :]
    cos, sin = jnp.cos(ang), jnp.sin(ang)
    t = pos.shape[0]
    ones = jnp.ones((t, MAP_W - ROT_DIM), F32)
    zeros = jnp.zeros((t, MAP_W - ROT_DIM), F32)
    zh = jnp.zeros((t, half), F32)
    c = jnp.concatenate([cos, cos, ones], axis=1)
    sa = jnp.concatenate([-sin, zh, zeros], axis=1)
    sb = jnp.concatenate([zh, sin, zeros], axis=1)
    return tuple(jnp.concatenate([m, m], axis=1) for m in (c, sa, sb))


def _pad_ffn(w, axis):
    f = w.shape[axis]
    fp = -(-f // FFN_HIDDEN_TILE) * FFN_HIDDEN_TILE
    pad = [(0, 0), (0, 0)]
    pad[axis] = (0, fp - f)
    return jnp.pad(w.astype(BF16), pad)


def _layer(x, pos, past, lam_init, p):
    b, t, d = x.shape
    n = b * t
    w_a = N_HEADS * HEAD_W
    vec = lambda v: v.reshape(1, -1)
    w_in = p["w_in"].astype(BF16)
    seg = lambda lo, hi: w_in[:, lo:hi]

    h1, xn = _ffn(x.reshape(n, d), vec(p["n1a"]), vec(p["n1b"]), vec(p["nma"]),
                  _pad_ffn(p["f1g"], 1), _pad_ffn(p["f1u"], 1), _pad_ffn(p["f1d"], 0))

    tables = tuple(jnp.tile(m, (b, 1)) for m in _rope_tables(pos))
    q_a = _proj(xn, seg(0, w_a), "q_rope", tables=tables, scale=MAP_W ** -0.5 * LOG2E)
    ka32, ka16 = _proj(xn, seg(w_a, 2 * w_a), "k_rope", tables=tables)
    q_b = _proj(xn, seg(3 * w_a, 4 * w_a), "q")
    kb32, kb16 = _proj(xn, seg(4 * w_a, 5 * w_a), "k")
    g_a = _proj(xn, seg(6 * w_a, 6 * w_a + d), "gate", tile=GATE_TOKEN_TILE)
    g_b = _proj(xn, seg(6 * w_a + d, 6 * w_a + 2 * d), "gate", tile=GATE_TOKEN_TILE)

    shape3 = lambda a: a.reshape(b, t, w_a)
    if past is None:
        tk = _tile(t, ATTN_K_TILE)
        va32, vat = _proj(xn, seg(2 * w_a, 3 * w_a), "v_t", seq=t, tk=tk)
        vb32, vbt = _proj(xn, seg(5 * w_a, 6 * w_a), "v_t", seq=t, tk=tk)
        q_off = 0
        qa3, qb3, ka3, kb3 = shape3(q_a), shape3(q_b), shape3(ka16), shape3(kb16)
    else:
        va32 = _proj(xn, seg(2 * w_a, 3 * w_a), "v")
        vb32 = _proj(xn, seg(5 * w_a, 6 * w_a), "v")
        pka, pva, pkb, pvb = past
        q_off = pka.shape[1]
        t_k = q_off + t
        t_kp = -(-t_k // V7X_LANES) * V7X_LANES
        t_qp = -(-t // V7X_LANES) * V7X_LANES

        def keys(pk, k16):
            full = jnp.concatenate([pk.reshape(b, q_off, w_a).astype(BF16), shape3(k16)], axis=1)
            return jnp.pad(full, ((0, 0), (0, t_kp - t_k), (0, 0)))

        def values_t(pv, v32):
            full = jnp.concatenate([pv.reshape(b, q_off, N_HEADS, HEAD_W),
                                    v32.reshape(b, t, N_HEADS, HEAD_W)], axis=1).astype(BF16)
            full = jnp.pad(full.transpose(0, 2, 3, 1), ((0, 0), (0, 0), (0, 0), (0, t_kp - t_k)))
            return full[:, :, None]

        padq = lambda a: jnp.pad(shape3(a), ((0, 0), (0, t_qp - t), (0, 0)))
        qa3, qb3 = padq(q_a), padq(q_b)
        ka3, kb3 = keys(pka, ka16), keys(pkb, kb16)
        vat, vbt = values_t(pva, va32), values_t(pvb, vb32)

    o_a = _attention("diff", qa3, ka3, vat, q_off,
                     [vec(p["lq1"]), vec(p["lk1"]), vec(p["lq2"]), vec(p["lk2"]),
                      p["subln_g"].reshape(-1, 1)], lam_init)
    o_b = _attention("sb", qb3, kb3, vbt, q_off, [], HEAD_W ** -0.5)
    o_a = o_a[:, :t].reshape(n, w_a)
    o_b = o_b[:, :t].reshape(n, w_a)

    h2 = _merge_out(o_a, o_b, g_a, g_b, h1, p["w_up_a"].astype(BF16), p["w_up_b"].astype(BF16),
                    p["w_o"].astype(BF16), vec(p["nmb"]))
    y = _ffn(h2, vec(p["n2a"]), vec(p["n2b"]), None,
             _pad_ffn(p["f2g"], 1), _pad_ffn(p["f2u"], 1), _pad_ffn(p["f2d"], 0))
    rows = (ka32.reshape(b, t, N_HEADS, 2, MAP_W), va32.reshape(b, t, N_HEADS, HEAD_W),
            kb32.reshape(b, t, N_HEADS, HEAD_W), vb32.reshape(b, t, N_HEADS, HEAD_W))
    return y.reshape(b, t, d), rows


def _stack(xs):
    return xs[0][None] if len(xs) == 1 else jnp.stack(xs)


def kernel(x_prompt, x_sample, cache_diff_k, cache_diff_v, cache_sb_k, cache_sb_v, w_in, w_up_a, w_up_b, w_o, lam_q1, lam_k1, lam_q2, lam_k2, subln_g, norm_ffn1_pre, norm_ffn1_post, norm_mix_pre, norm_mix_post, norm_ffn2_pre, norm_ffn2_post, ffn1_w_gate, ffn1_w_up, ffn1_w_down, ffn2_w_gate, ffn2_w_up, ffn2_w_down):
    depth = w_in.shape[0]
    pos_p = jnp.arange(x_prompt.shape[1], dtype=jnp.int32)
    pos_s = cache_diff_k.shape[2] + jnp.arange(x_sample.shape[1], dtype=jnp.int32)
    hp, hs = x_prompt, x_sample
    rows_p, rows_s = [], []
    for l in range(depth):
        lam_init = 0.8 - 0.6 * math.exp(-0.3 * l)
        p = dict(w_in=w_in[l], w_up_a=w_up_a[l], w_up_b=w_up_b[l], w_o=w_o[l],
                 lq1=lam_q1[l], lk1=lam_k1[l], lq2=lam_q2[l], lk2=lam_k2[l], subln_g=subln_g[l],
                 n1a=norm_ffn1_pre[l], n1b=norm_ffn1_post[l], nma=norm_mix_pre[l], nmb=norm_mix_post[l],
                 n2a=norm_ffn2_pre[l], n2b=norm_ffn2_post[l],
                 f1g=ffn1_w_gate[l], f1u=ffn1_w_up[l], f1d=ffn1_w_down[l],
                 f2g=ffn2_w_gate[l], f2u=ffn2_w_up[l], f2d=ffn2_w_down[l])
        hp, rp = _layer(hp, pos_p, None, lam_init, p)
        past = (cache_diff_k[l], cache_diff_v[l], cache_sb_k[l], cache_sb_v[l])
        hs, rs = _layer(hs, pos_s, past, lam_init, p)
        rows_p.append(rp)
        rows_s.append(rs)
    outs_p = [_stack([r[i] for r in rows_p]) for i in range(4)]
    outs_s = [_stack([r[i] for r in rows_s]) for i in range(4)]
    return (hp, hs, *outs_p, *outs_s)
```

```python
import functools
import math

import jax
import jax.numpy as jnp
from jax import lax
from jax.experimental import pallas as pl
from jax.experimental.pallas import tpu as pltpu

F32 = jnp.float32
BF16 = jnp.bfloat16

EPS = 1e-6
CHUNK = 64
CHUNK_SHIFT = CHUNK.bit_length() - 1
assert CHUNK == 1 << CHUNK_SHIFT
N_HEADS = 8
HEAD_W = 128
MAP_W = 64
ROT_DIM = 16
ROPE_THETA = 500000.0
NEG = -1e30

V7X_LANES = 128
V7X_MXU_DIM = 256
V7X_VMEM_BYTES = 64 * 1024 * 1024
VMEM_LIMIT_BYTES = V7X_VMEM_BYTES - 8 * 1024 * 1024

FFN_TOKEN_TILE = 512
FFN_HIDDEN_TILE = 512
PROJ_TOKEN_TILE = 1024
GATE_TOKEN_TILE = 512
OUT_TOKEN_TILE = 256
ATTN_Q_TILE = 512
DIFF_K_TILE = 512
SB_K_TILE = V7X_MXU_DIM
DIFF_HEADS_PER_STEP = 4
SB_HEADS_PER_STEP = 8
LOG2E = math.log2(math.e)


def _tile(n, target):
    t = min(n, target)
    while n % t:
        t -= 1
    return t


def _params(*sem):
    return pltpu.CompilerParams(dimension_semantics=sem, vmem_limit_bytes=VMEM_LIMIT_BYTES)


def _rms(x, g):
    return x * lax.rsqrt(jnp.mean(x * x, axis=-1, keepdims=True) + EPS) * g


def _ffn_kernel(n_f, has_next, x_ref, npre_ref, npost_ref, *refs):
    if has_next:
        nnext_ref, wg_ref, wu_ref, wd_ref, h_ref, xn_ref, xs_ref, acc_ref = refs
    else:
        wg_ref, wu_ref, wd_ref, h_ref, xs_ref, acc_ref = refs
    f = pl.program_id(1)

    @pl.when(f == 0)
    def _():
        xs_ref[...] = _rms(x_ref[...], npre_ref[...]).astype(BF16)
        acc_ref[...] = jnp.zeros_like(acc_ref)

    xs = xs_ref[...]
    g = jnp.dot(xs, wg_ref[...], preferred_element_type=F32)
    u = jnp.dot(xs, wu_ref[...], preferred_element_type=F32)
    a = (g * jax.nn.sigmoid(g) * u).astype(BF16)
    acc_ref[...] += jnp.dot(a, wd_ref[...], preferred_element_type=F32)

    @pl.when(f == n_f - 1)
    def _():
        h = x_ref[...] + 0.5 * _rms(acc_ref[...], npost_ref[...])
        h_ref[...] = h
        if has_next:
            xn_ref[...] = _rms(h, nnext_ref[...]).astype(BF16)


def _ffn(x, n_pre, n_post, n_next, wg, wu, wd):
    n, d = x.shape
    fp = wg.shape[1]
    tm = _tile(n, FFN_TOKEN_TILE)
    tf = _tile(fp, FFN_HIDDEN_TILE)
    n_f = fp // tf
    has_next = n_next is not None
    row = pl.BlockSpec((tm, d), lambda i, f: (i, 0))
    vec = pl.BlockSpec((1, d), lambda i, f: (0, 0))
    in_specs = [row, vec, vec] + ([vec] if has_next else []) + [
        pl.BlockSpec((d, tf), lambda i, f: (0, f)),
        pl.BlockSpec((d, tf), lambda i, f: (0, f)),
        pl.BlockSpec((tf, d), lambda i, f: (f, 0)),
    ]
    out_shape = [jax.ShapeDtypeStruct((n, d), F32)]
    out_specs = [row]
    if has_next:
        out_shape.append(jax.ShapeDtypeStruct((n, d), BF16))
        out_specs.append(row)
    args = [x, n_pre, n_post] + ([n_next] if has_next else []) + [wg, wu, wd]
    res = pl.pallas_call(
        functools.partial(_ffn_kernel, n_f, has_next),
        grid=(n // tm, n_f),
        in_specs=in_specs,
        out_specs=out_specs,
        out_shape=out_shape,
        scratch_shapes=[pltpu.VMEM((tm, d), BF16), pltpu.VMEM((tm, d), F32)],
        compiler_params=_params("parallel", "arbitrary"),
        name="ffn_next" if has_next else "ffn",
    )(*args)
    return res if has_next else res[0]


def _rope(r, c, sa, sb):
    heads = []
    for h in range(r.shape[1] // HEAD_W):
        x = r[:, h * HEAD_W:(h + 1) * HEAD_W]
        up = pltpu.roll(x, HEAD_W - ROT_DIM // 2, 1)
        dn = pltpu.roll(x, ROT_DIM // 2, 1)
        heads.append(x * c + up * sa + dn * sb)
    return jnp.concatenate(heads, axis=1)


def _proj_kernel(mode, scale, tk, x_ref, w_ref, *refs):
    r = jnp.dot(x_ref[...], w_ref[...], preferred_element_type=F32)
    if mode in ("q_rope", "k_rope"):
        c_ref, sa_ref, sb_ref = refs[:3]
        refs = refs[3:]
        r = _rope(r, c_ref[...], sa_ref[...], sb_ref[...])
    if mode in ("q", "q_rope"):
        (q_ref,) = refs
        q_ref[...] = (r * scale).astype(BF16)
    elif mode == "gate":
        (g_ref,) = refs
        g_ref[...] = jax.nn.sigmoid(r)
    elif mode in ("k", "k_rope"):
        k32_ref, k16_ref = refs
        k32_ref[...] = r
        k16_ref[...] = r.astype(BF16)
    elif mode == "v":
        (v32_ref,) = refs
        v32_ref[...] = r
    else:
        v32_ref, vt_ref = refs
        v32_ref[...] = r
        for h in range(N_HEADS):
            for jb in range(r.shape[0] // tk):
                blk = r[jb * tk:(jb + 1) * tk, h * HEAD_W:(h + 1) * HEAD_W]
                vt_ref[0, h, jb] = blk.T.astype(BF16)


def _proj(xn, w, mode, *, tables=None, scale=1.0, seq=None, tk=None, tile=PROJ_TOKEN_TILE):
    n, d = xn.shape
    width = w.shape[1]
    tm = _tile(n if seq is None else seq, tile)
    row = lambda wd: pl.BlockSpec((tm, wd), lambda i: (i, 0))
    in_specs = [row(d), pl.BlockSpec((d, width), lambda i: (0, 0))]
    args = [xn, w]
    if mode in ("q_rope", "k_rope"):
        in_specs += [row(HEAD_W)] * 3
        args += list(tables)
    if mode in ("q", "q_rope"):
        out_shape = [jax.ShapeDtypeStruct((n, width), BF16)]
        out_specs = [row(width)]
    elif mode in ("gate", "v"):
        out_shape = [jax.ShapeDtypeStruct((n, width), F32)]
        out_specs = [row(width)]
    elif mode in ("k", "k_rope"):
        out_shape = [jax.ShapeDtypeStruct((n, width), F32), jax.ShapeDtypeStruct((n, width), BF16)]
        out_specs = [row(width), row(width)]
    else:
        assert mode == "v_t" and tm % tk == 0 and seq % tm == 0
        per_seq = seq // tm
        out_shape = [jax.ShapeDtypeStruct((n, width), F32),
                     jax.ShapeDtypeStruct((n // seq, N_HEADS, seq // tk, HEAD_W, tk), BF16)]
        out_specs = [row(width),
                     pl.BlockSpec((1, N_HEADS, tm // tk, HEAD_W, tk),
                                  lambda i: (i // per_seq, 0, i % per_seq, 0, 0))]
    res = pl.pallas_call(
        functools.partial(_proj_kernel, mode, scale, tk),
        grid=(n // tm,),
        in_specs=in_specs,
        out_specs=out_specs,
        out_shape=out_shape,
        compiler_params=_params("parallel"),
        name="proj_" + mode,
    )(*args)
    return res[0] if len(res) == 1 else res


def _kv_range(kind, i, tq, tk, n_k, q_off):
    qmin = q_off + i * tq
    qmax = qmin + tq - 1
    if kind == "diff":
        n_int = ((qmin // CHUNK + 1) * CHUNK) // tk
        last = ((qmax // CHUNK + 1) * CHUNK - 1) // tk
    else:
        n_int = qmin // tk
        last = (qmax - 1) // tk
    return jnp.minimum(n_int, n_k), jnp.minimum(last + 1, n_k), qmin


def _diff_attn_kernel(tq, tk, n_k, q_off, lam_init, hb,
                      q_ref, k_ref, vt_ref, lq1_ref, lk1_ref, lq2_ref, lk2_ref, g_ref,
                      o_ref, qm_ref, acc_ref):
    n_int, n_tot, qmin = _kv_range("diff", pl.program_id(2), tq, tk, n_k, q_off)
    lane = lax.broadcasted_iota(jnp.int32, (1, HEAD_W), 1)
    for h in range(hb):
        q = q_ref[0, :, h * HEAD_W:(h + 1) * HEAD_W]
        zero = jnp.zeros_like(q)
        qm_ref[2 * h] = jnp.where(lane < MAP_W, q, zero)
        qm_ref[2 * h + 1] = jnp.where(lane >= MAP_W, q, zero)
    q_chunk = (qmin + lax.broadcasted_iota(jnp.int32, (1, tq), 1)) >> CHUNK_SHIFT
    acc_ref[...] = jnp.zeros_like(acc_ref)

    def step(masked, j, carry):
        k0 = pl.multiple_of(j * tk, tk)
        if masked:
            k_chunk = (k0 + lax.broadcasted_iota(jnp.int32, (tk, 1), 0)) >> CHUNK_SHIFT
            valid = k_chunk <= q_chunk
        chains = range(2 * hb)
        scores = [lax.dot_general(k_ref[0, pl.ds(k0, tk), (c // 2) * HEAD_W:(c // 2 + 1) * HEAD_W],
                                  qm_ref[c], (((1,), (1,)), ((), ())),
                                  preferred_element_type=F32) for c in chains]
        new, probs, alphas = [], [], []
        for c in chains:
            m, l = carry[2 * c], carry[2 * c + 1]
            s = jnp.where(valid, scores[c], NEG) if masked else scores[c]
            m_new = jnp.maximum(m, jnp.max(s, axis=0, keepdims=True))
            alpha = jnp.exp2(m - m_new)
            p = jnp.exp2(s - m_new)
            new += [m_new, alpha * l + jnp.sum(p, axis=0, keepdims=True)]
            probs.append(p.astype(BF16))
            alphas.append(alpha)
        for c in chains:
            acc_ref[c] = alphas[c] * acc_ref[c] + jnp.dot(
                vt_ref[0, c // 2, j], probs[c], preferred_element_type=F32)
        return tuple(new)

    m_init = jnp.full((1, tq), NEG, F32)
    l_init = jnp.zeros((1, tq), F32)
    carry = (m_init, l_init) * (2 * hb)
    carry = lax.fori_loop(0, n_int, functools.partial(step, False), carry)
    carry = lax.fori_loop(n_int, n_tot, functools.partial(step, True), carry)

    lam = (jnp.exp(jnp.sum(lq1_ref[...] * lk1_ref[...], keepdims=True))
           - jnp.exp(jnp.sum(lq2_ref[...] * lk2_ref[...], keepdims=True)) + lam_init)
    for h in range(hb):
        l0, l1 = carry[4 * h + 1], carry[4 * h + 3]
        ot = acc_ref[2 * h] * (1.0 / l0) - lam * (acc_ref[2 * h + 1] * (1.0 / l1))
        ms = jnp.mean(ot * ot, axis=0, keepdims=True)
        ot = ot * lax.rsqrt(ms + EPS) * g_ref[...] * (1.0 - lam_init)
        o_ref[0, :, h * HEAD_W:(h + 1) * HEAD_W] = ot.T.astype(BF16)


def _sb_attn_kernel(tq, tk, n_k, q_off, hb, q_ref, k_ref, vt_ref, o_ref, acc_ref):
    n_int, n_tot, qmin = _kv_range("sb", pl.program_id(2), tq, tk, n_k, q_off)
    q_pos = qmin + lax.broadcasted_iota(jnp.int32, (1, tq), 1)
    suffix = (lax.broadcasted_iota(jnp.int32, (tk, tk), 1)
              >= lax.broadcasted_iota(jnp.int32, (tk, tk), 0)).astype(BF16)
    suffix2 = jnp.concatenate([suffix, suffix], axis=1)
    acc_ref[...] = jnp.zeros_like(acc_ref)

    def step(masked, t, later):
        j = n_tot - 1 - t
        k0 = pl.multiple_of(j * tk, tk)
        if masked:
            valid = (k0 + lax.broadcasted_iota(jnp.int32, (tk, 1), 0)) < q_pos
        heads = range(hb)
        ws = [lax.dot_general(k_ref[0, pl.ds(k0, tk), h * HEAD_W:(h + 1) * HEAD_W],
                              q_ref[0, :, h * HEAD_W:(h + 1) * HEAD_W], (((1,), (1,)), ((), ())),
                              preferred_element_type=F32) for h in heads]
        tails = []
        for h in heads:
            sp = jnp.maximum(ws[h], 0.0) + jnp.log(1.0 + jnp.exp2(-jnp.abs(ws[h]))) * LOG2E
            if masked:
                sp = jnp.where(valid, sp, 0.0)
            hi = sp.astype(BF16)
            lo = (sp - hi.astype(F32)).astype(BF16)
            tails.append(jnp.dot(suffix2, jnp.concatenate([hi, lo], axis=0),
                                 preferred_element_type=F32))
        weights = []
        for h in heads:
            logit = ws[h] - tails[h] - later[h]
            if masked:
                logit = jnp.where(valid, logit, NEG)
            weights.append(jnp.exp2(logit).astype(BF16))
        for h in heads:
            acc_ref[h] += jnp.dot(vt_ref[0, h, j], weights[h], preferred_element_type=F32)
        return tuple(later[h] + tails[h][0:1, :] for h in heads)

    later = (jnp.zeros((1, tq), F32),) * hb
    later = lax.fori_loop(0, n_tot - n_int, functools.partial(step, True), later)
    lax.fori_loop(n_tot - n_int, n_tot, functools.partial(step, False), later)
    for h in range(hb):
        o_ref[0, :, h * HEAD_W:(h + 1) * HEAD_W] = acc_ref[h].T.astype(BF16)


def _attention(kind, q, k, vt, q_off, extra, cfg):
    b, t_q, width = q.shape
    n_k, tk = vt.shape[2], vt.shape[4]
    t_k = k.shape[1]
    assert t_k == n_k * tk
    tq = _tile(t_q, ATTN_Q_TILE)
    hb = DIFF_HEADS_PER_STEP if kind == "diff" else SB_HEADS_PER_STEP
    kv_mode = dict(pipeline_mode=pl.Buffered(1)) if hb == N_HEADS else {}
    q_spec = pl.BlockSpec((1, tq, hb * HEAD_W), lambda bi, h, i: (bi, i, h))
    in_specs = [q_spec,
                pl.BlockSpec((1, t_k, hb * HEAD_W), lambda bi, h, i: (bi, 0, h), **kv_mode),
                pl.BlockSpec((1, hb, n_k, HEAD_W, tk), lambda bi, h, i: (bi, h, 0, 0, 0), **kv_mode)]
    for e in extra:
        in_specs.append(pl.BlockSpec(e.shape, lambda bi, h, i: (0, 0)))
    if kind == "diff":
        body = functools.partial(_diff_attn_kernel, tq, tk, n_k, q_off, cfg, hb)
        scratch = [pltpu.VMEM((2 * hb, tq, HEAD_W), BF16), pltpu.VMEM((2 * hb, HEAD_W, tq), F32)]
    else:
        body = functools.partial(_sb_attn_kernel, tq, tk, n_k, q_off, hb)
        scratch = [pltpu.VMEM((hb, HEAD_W, tq), F32)]
    return pl.pallas_call(
        body,
        grid=(b, N_HEADS // hb, t_q // tq),
        in_specs=in_specs,
        out_specs=q_spec,
        out_shape=jax.ShapeDtypeStruct((b, t_q, width), BF16),
        scratch_shapes=scratch,
        compiler_params=_params("parallel", "parallel", "arbitrary"),
        name=kind + "_attn",
    )(q, k, vt, *extra)


def _out_kernel(oa_ref, ob_ref, ga_ref, gb_ref, h_ref, wua_ref, wub_ref, wo_ref, n_ref, o_ref):
    pa = jnp.dot(oa_ref[...], wua_ref[...], preferred_element_type=F32)
    pb = jnp.dot(ob_ref[...], wub_ref[...], preferred_element_type=F32)
    merged = (ga_ref[...] * pa + gb_ref[...] * pb).astype(BF16)
    r = jnp.dot(merged, wo_ref[...], preferred_element_type=F32)
    o_ref[...] = h_ref[...] + _rms(r, n_ref[...])


def _merge_out(oa, ob, ga, gb, h, wua, wub, wo, n_post):
    n, d = h.shape
    w_attn = oa.shape[1]
    tm = _tile(n, OUT_TOKEN_TILE)
    row = lambda wd: pl.BlockSpec((tm, wd), lambda i: (i, 0))
    const = lambda shape: pl.BlockSpec(shape, lambda i: (0, 0), pipeline_mode=pl.Buffered(1))
    return pl.pallas_call(
        _out_kernel,
        grid=(n // tm,),
        in_specs=[row(w_attn), row(w_attn), row(d), row(d), row(d),
                  const(wua.shape), const(wub.shape), const(wo.shape), const(n_post.shape)],
        out_specs=row(d),
        out_shape=jax.ShapeDtypeStruct((n, d), F32),
        compiler_params=_params("parallel"),
        name="merge_out",
    )(oa, ob, ga, gb, h, wua, wub, wo, n_post)


def _rope_tables(pos):
    half = ROT_DIM // 2
    inv_freq = jnp.power(ROPE_THETA, -jnp.arange(0, ROT_DIM, 2, dtype=F32) / ROT_DIM)
    ang = pos.astype(F32)[:, None] * inv_freq[None, :]
    cos, sin = jnp.cos(ang), jnp.sin(ang)
    t = pos.shape[0]
    ones = jnp.ones((t, MAP_W - ROT_DIM), F32)
    zeros = jnp.zeros((t, MAP_W - ROT_DIM), F32)
    zh = jnp.zeros((t, half), F32)
    c = jnp.concatenate([cos, cos, ones], axis=1)
    sa = jnp.concatenate([-sin, zh, zeros], axis=1)
    sb = jnp.concatenate([zh, sin, zeros], axis=1)
    return tuple(jnp.concatenate([m, m], axis=1) for m in (c, sa, sb))


def _pad_ffn(w, axis):
    f = w.shape[axis]
    fp = -(-f // FFN_HIDDEN_TILE) * FFN_HIDDEN_TILE
    shape = list(w.shape)
    shape[axis] = fp - f
    return jnp.concatenate([w.astype(BF16), jnp.zeros(shape, BF16)], axis=axis)


def _layer(x, pos, past, lam_init, p):
    b, t, d = x.shape
    n = b * t
    w_a = N_HEADS * HEAD_W
    vec = lambda v: v.reshape(1, -1)
    w_in = p["w_in"].astype(BF16)
    seg = lambda lo, hi: w_in[:, lo:hi]

    h1, xn = _ffn(x.reshape(n, d), vec(p["n1a"]), vec(p["n1b"]), vec(p["nma"]),
                  _pad_ffn(p["f1g"], 1), _pad_ffn(p["f1u"], 1), _pad_ffn(p["f1d"], 0))

    tables = tuple(jnp.tile(m, (b, 1)) for m in _rope_tables(pos))
    q_a = _proj(xn, seg(0, w_a), "q_rope", tables=tables, scale=MAP_W ** -0.5 * LOG2E)
    ka32, ka16 = _proj(xn, seg(w_a, 2 * w_a), "k_rope", tables=tables)
    q_b = _proj(xn, seg(3 * w_a, 4 * w_a), "q", scale=HEAD_W ** -0.5 * LOG2E)
    kb32, kb16 = _proj(xn, seg(4 * w_a, 5 * w_a), "k")
    g_a = _proj(xn, seg(6 * w_a, 6 * w_a + d), "gate", tile=GATE_TOKEN_TILE)
    g_b = _proj(xn, seg(6 * w_a + d, 6 * w_a + 2 * d), "gate", tile=GATE_TOKEN_TILE)

    shape3 = lambda a: a.reshape(b, t, w_a)
    if past is None:
        va32, vat = _proj(xn, seg(2 * w_a, 3 * w_a), "v_t", seq=t, tk=_tile(t, DIFF_K_TILE))
        vb32, vbt = _proj(xn, seg(5 * w_a, 6 * w_a), "v_t", seq=t, tk=_tile(t, SB_K_TILE))
        q_off = 0
        qa3, qb3, ka3, kb3 = shape3(q_a), shape3(q_b), shape3(ka16), shape3(kb16)
    else:
        va32 = _proj(xn, seg(2 * w_a, 3 * w_a), "v")
        vb32 = _proj(xn, seg(5 * w_a, 6 * w_a), "v")
        pka, pva, pkb, pvb = past
        q_off = pka.shape[1]
        t_k = q_off + t
        t_kp = -(-t_k // V7X_LANES) * V7X_LANES
        t_qp = -(-t // V7X_LANES) * V7X_LANES

        def keys(pk, k16):
            full = jnp.concatenate([pk.reshape(b, q_off, w_a).astype(BF16), shape3(k16)], axis=1)
            return jnp.pad(full, ((0, 0), (0, t_kp - t_k), (0, 0)))

        def values_t(pv, v32):
            full = jnp.concatenate([pv.reshape(b, q_off, N_HEADS, HEAD_W),
                                    v32.reshape(b, t, N_HEADS, HEAD_W)], axis=1).astype(BF16)
            full = jnp.pad(full.transpose(0, 2, 3, 1), ((0, 0), (0, 0), (0, 0), (0, t_kp - t_k)))
            return full[:, :, None]

        padq = lambda a: jnp.pad(shape3(a), ((0, 0), (0, t_qp - t), (0, 0)))
        qa3, qb3 = padq(q_a), padq(q_b)
        ka3, kb3 = keys(pka, ka16), keys(pkb, kb16)
        vat, vbt = values_t(pva, va32), values_t(pvb, vb32)

    o_a = _attention("diff", qa3, ka3, vat, q_off,
                     [vec(p["lq1"]), vec(p["lk1"]), vec(p["lq2"]), vec(p["lk2"]),
                      p["subln_g"].reshape(-1, 1)], lam_init)
    o_b = _attention("sb", qb3, kb3, vbt, q_off, [], None)
    o_a = o_a[:, :t].reshape(n, w_a)
    o_b = o_b[:, :t].reshape(n, w_a)

    h2 = _merge_out(o_a, o_b, g_a, g_b, h1, p["w_up_a"].astype(BF16), p["w_up_b"].astype(BF16),
                    p["w_o"].astype(BF16), vec(p["nmb"]))
    y = _ffn(h2, vec(p["n2a"]), vec(p["n2b"]), None,
             _pad_ffn(p["f2g"], 1), _pad_ffn(p["f2u"], 1), _pad_ffn(p["f2d"], 0))
    rows = (ka32.reshape(b, t, N_HEADS, 2, MAP_W), va32.reshape(b, t, N_HEADS, HEAD_W),
            kb32.reshape(b, t, N_HEADS, HEAD_W), vb32.reshape(b, t, N_HEADS, HEAD_W))
    return y.reshape(b, t, d), rows


def _stack(xs):
    return xs[0][None] if len(xs) == 1 else jnp.stack(xs)


def kernel(x_prompt, x_sample, cache_diff_k, cache_diff_v, cache_sb_k, cache_sb_v, w_in, w_up_a, w_up_b, w_o, lam_q1, lam_k1, lam_q2, lam_k2, subln_g, norm_ffn1_pre, norm_ffn1_post, norm_mix_pre, norm_mix_post, norm_ffn2_pre, norm_ffn2_post, ffn1_w_gate, ffn1_w_up, ffn1_w_down, ffn2_w_gate, ffn2_w_up, ffn2_w_down):
    depth = w_in.shape[0]
    pos_p = jnp.arange(x_prompt.shape[1], dtype=jnp.int32)
    pos_s = cache_diff_k.shape[2] + jnp.arange(x_sample.shape[1], dtype=jnp.int32)
    hp, hs = x_prompt, x_sample
    rows_p, rows_s = [], []
    for l in range(depth):
        lam_init = 0.8 - 0.6 * math.exp(-0.3 * l)
        p = dict(w_in=w_in[l], w_up_a=w_up_a[l], w_up_b=w_up_b[l], w_o=w_o[l],
                 lq1=lam_q1[l], lk1=lam_k1[l], lq2=lam_q2[l], lk2=lam_k2[l], subln_g=subln_g[l],
                 n1a=norm_ffn1_pre[l], n1b=norm_ffn1_post[l], nma=norm_mix_pre[l], nmb=norm_mix_post[l],
                 n2a=norm_ffn2_pre[l], n2b=norm_ffn2_post[l],
                 f1g=ffn1_w_gate[l], f1u=ffn1_w_up[l], f1d=ffn1_w_down[l],
                 f2g=ffn2_w_gate[l], f2u=ffn2_w_up[l], f2d=ffn2_w_down[l])
        hp, rp = _layer(hp, pos_p, None, lam_init, p)
        past = (cache_diff_k[l], cache_diff_v[l], cache_sb_k[l], cache_sb_v[l])
        hs, rs = _layer(hs, pos_s, past, lam_init, p)
        rows_p.append(rp)
        rows_s.append(rs)
    outs_p = [_stack([r[i] for r in rows_p]) for i in range(4)]
    outs_s = [_stack([r[i] for r in rows_s]) for i in range(4)]
    return (hp, hs, *outs_p, *outs_s)
```

```python
import functools
import math

import jax
import jax.numpy as jnp
from jax import lax
from jax.experimental import pallas as pl
from jax.experimental.pallas import tpu as pltpu

F32 = jnp.float32
BF16 = jnp.bfloat16

EPS = 1e-6
CHUNK = 64
CHUNK_SHIFT = CHUNK.bit_length() - 1
assert CHUNK == 1 << CHUNK_SHIFT
N_HEADS = 8
HEAD_W = 128
MAP_W = 64
ROT_DIM = 16
ROPE_THETA = 500000.0
NEG = -1e30

V7X_LANES = 128
V7X_MXU_DIM = 256
V7X_VMEM_BYTES = 64 * 1024 * 1024
VMEM_LIMIT_BYTES = V7X_VMEM_BYTES - 8 * 1024 * 1024

FFN_TOKEN_TILE = 512
FFN_HIDDEN_TILE = 512
PROJ_TOKEN_TILE = 1024
GATE_TOKEN_TILE = 512
OUT_TOKEN_TILE = 256
ATTN_Q_TILE = 512
DIFF_K_TILE = 512
SB_K_TILE = V7X_MXU_DIM
DIFF_HEADS_PER_STEP = 4
SB_HEADS_PER_STEP = 8
LOG2E = math.log2(math.e)


def _tile(n, target):
    t = min(n, target)
    while n % t:
        t -= 1
    return t


def _params(*sem):
    return pltpu.CompilerParams(dimension_semantics=sem, vmem_limit_bytes=VMEM_LIMIT_BYTES)


def _rms(x, g):
    return x * lax.rsqrt(jnp.mean(x * x, axis=-1, keepdims=True) + EPS) * g


def _ffn_kernel(n_f, has_next, x_ref, npre_ref, npost_ref, *refs):
    if has_next:
        nnext_ref, wg_ref, wu_ref, wd_ref, h_ref, xn_ref, xs_ref, acc_ref = refs
    else:
        wg_ref, wu_ref, wd_ref, h_ref, xs_ref, acc_ref = refs
    f = pl.program_id(1)

    @pl.when(f == 0)
    def _():
        xs_ref[...] = _rms(x_ref[...], npre_ref[...]).astype(BF16)
        acc_ref[...] = jnp.zeros_like(acc_ref)

    xs = xs_ref[...]
    g = jnp.dot(xs, wg_ref[...], preferred_element_type=F32)
    u = jnp.dot(xs, wu_ref[...], preferred_element_type=F32)
    a = (g * jax.nn.sigmoid(g) * u).astype(BF16)
    acc_ref[...] += jnp.dot(a, wd_ref[...], preferred_element_type=F32)

    @pl.when(f == n_f - 1)
    def _():
        h = x_ref[...] + 0.5 * _rms(acc_ref[...], npost_ref[...])
        h_ref[...] = h
        if has_next:
            xn_ref[...] = _rms(h, nnext_ref[...]).astype(BF16)


def _ffn(x, n_pre, n_post, n_next, wg, wu, wd):
    n, d = x.shape
    fp = wg.shape[1]
    tm = _tile(n, FFN_TOKEN_TILE)
    tf = _tile(fp, FFN_HIDDEN_TILE)
    n_f = fp // tf
    has_next = n_next is not None
    row = pl.BlockSpec((tm, d), lambda i, f: (i, 0))
    vec = pl.BlockSpec((1, d), lambda i, f: (0, 0))
    in_specs = [row, vec, vec] + ([vec] if has_next else []) + [
        pl.BlockSpec((d, tf), lambda i, f: (0, f)),
        pl.BlockSpec((d, tf), lambda i, f: (0, f)),
        pl.BlockSpec((tf, d), lambda i, f: (f, 0)),
    ]
    out_shape = [jax.ShapeDtypeStruct((n, d), F32)]
    out_specs = [row]
    if has_next:
        out_shape.append(jax.ShapeDtypeStruct((n, d), BF16))
        out_specs.append(row)
    args = [x, n_pre, n_post] + ([n_next] if has_next else []) + [wg, wu, wd]
    res = pl.pallas_call(
        functools.partial(_ffn_kernel, n_f, has_next),
        grid=(n // tm, n_f),
        in_specs=in_specs,
        out_specs=out_specs,
        out_shape=out_shape,
        scratch_shapes=[pltpu.VMEM((tm, d), BF16), pltpu.VMEM((tm, d), F32)],
        compiler_params=_params("parallel", "arbitrary"),
        name="ffn_next" if has_next else "ffn",
    )(*args)
    return res if has_next else res[0]


def _rope(r, c, sa, sb):
    heads = []
    for h in range(r.shape[1] // HEAD_W):
        x = r[:, h * HEAD_W:(h + 1) * HEAD_W]
        up = pltpu.roll(x, HEAD_W - ROT_DIM // 2, 1)
        dn = pltpu.roll(x, ROT_DIM // 2, 1)
        heads.append(x * c + up * sa + dn * sb)
    return jnp.concatenate(heads, axis=1)


def _proj_kernel(mode, scale, tk, x_ref, w_ref, *refs):
    r = jnp.dot(x_ref[...], w_ref[...], preferred_element_type=F32)
    if mode in ("q_rope", "k_rope", "kv_rope"):
        c_ref, sa_ref, sb_ref = refs[:3]
        refs = refs[3:]
        r = _rope(r, c_ref[...], sa_ref[...], sb_ref[...])
    if mode in ("q", "q_rope"):
        (q_ref,) = refs
        q_ref[...] = (r * scale).astype(BF16)
    elif mode == "gate":
        (g_ref,) = refs
        g_ref[...] = jax.nn.sigmoid(r)
    elif mode in ("k", "k_rope"):
        k32_ref, k16_ref = refs
        k32_ref[...] = r
        k16_ref[...] = r.astype(BF16)
    elif mode in ("kv", "kv_rope"):
        (r32_ref,) = refs
        r32_ref[...] = r
    else:
        v32_ref, vt_ref = refs
        v32_ref[...] = r
        for h in range(N_HEADS):
            for jb in range(r.shape[0] // tk):
                blk = r[jb * tk:(jb + 1) * tk, h * HEAD_W:(h + 1) * HEAD_W]
                vt_ref[0, h, jb] = blk.T.astype(BF16)


def _proj(xn, w, mode, *, tables=None, scale=1.0, seq=None, tk=None, tile=PROJ_TOKEN_TILE):
    n, d = xn.shape
    width = w.shape[1]
    tm = _tile(n if seq is None else seq, tile)
    row = lambda wd: pl.BlockSpec((tm, wd), lambda i: (i, 0))
    in_specs = [row(d), pl.BlockSpec((d, width), lambda i: (0, 0))]
    args = [xn, w]
    if mode in ("q_rope", "k_rope", "kv_rope"):
        in_specs += [row(HEAD_W)] * 3
        args += list(tables)
    if mode in ("q", "q_rope"):
        out_shape = [jax.ShapeDtypeStruct((n, width), BF16)]
        out_specs = [row(width)]
    elif mode in ("gate", "kv", "kv_rope"):
        out_shape = [jax.ShapeDtypeStruct((n, width), F32)]
        out_specs = [row(width)]
    elif mode in ("k", "k_rope"):
        out_shape = [jax.ShapeDtypeStruct((n, width), F32), jax.ShapeDtypeStruct((n, width), BF16)]
        out_specs = [row(width), row(width)]
    else:
        assert mode == "v_t" and tm % tk == 0 and seq % tm == 0
        per_seq = seq // tm
        out_shape = [jax.ShapeDtypeStruct((n, width), F32),
                     jax.ShapeDtypeStruct((n // seq, N_HEADS, seq // tk, HEAD_W, tk), BF16)]
        out_specs = [row(width),
                     pl.BlockSpec((1, N_HEADS, tm // tk, HEAD_W, tk),
                                  lambda i: (i // per_seq, 0, i % per_seq, 0, 0))]
    res = pl.pallas_call(
        functools.partial(_proj_kernel, mode, scale, tk),
        grid=(n // tm,),
        in_specs=in_specs,
        out_specs=out_specs,
        out_shape=out_shape,
        compiler_params=_params("parallel"),
        name="proj_" + mode,
    )(*args)
    return res[0] if len(res) == 1 else res


def _kv_range(kind, i, tq, tk, n_k, q_off):
    qmin = q_off + i * tq
    qmax = qmin + tq - 1
    if kind == "diff":
        n_int = ((qmin // CHUNK + 1) * CHUNK) // tk
        last = ((qmax // CHUNK + 1) * CHUNK - 1) // tk
    else:
        n_int = qmin // tk
        last = (qmax - 1) // tk
    return jnp.minimum(n_int, n_k), jnp.minimum(last + 1, n_k), qmin


def _split_maps(q_ref, qm_ref, hb):
    lane = lax.broadcasted_iota(jnp.int32, (1, HEAD_W), 1)
    for h in range(hb):
        q = q_ref[0, :, h * HEAD_W:(h + 1) * HEAD_W]
        zero = jnp.zeros_like(q)
        qm_ref[2 * h] = jnp.where(lane < MAP_W, q, zero)
        qm_ref[2 * h + 1] = jnp.where(lane >= MAP_W, q, zero)


def _diff_block(qm_ref, acc_ref, kbs, vtbs, valid, carry):
    chains = range(2 * len(kbs))
    scores = [lax.dot_general(kbs[c // 2], qm_ref[c], (((1,), (1,)), ((), ())),
                              preferred_element_type=F32) for c in chains]
    new, probs, alphas = [], [], []
    for c in chains:
        m, l = carry[2 * c], carry[2 * c + 1]
        s = scores[c] if valid is None else jnp.where(valid, scores[c], NEG)
        m_new = jnp.maximum(m, jnp.max(s, axis=0, keepdims=True))
        alpha = jnp.exp2(m - m_new)
        p = jnp.exp2(s - m_new)
        new += [m_new, alpha * l + jnp.sum(p, axis=0, keepdims=True)]
        probs.append(p.astype(BF16))
        alphas.append(alpha)
    for c in chains:
        acc_ref[c] = alphas[c] * acc_ref[c] + jnp.dot(vtbs[c // 2], probs[c], preferred_element_type=F32)
    return tuple(new)


def _diff_init(acc_ref, tq, hb):
    acc_ref[...] = jnp.zeros_like(acc_ref)
    return (jnp.full((1, tq), NEG, F32), jnp.zeros((1, tq), F32)) * (2 * hb)


def _diff_finish(carry, acc_ref, lam_refs, g_ref, lam_init, o_ref, hb):
    lq1_ref, lk1_ref, lq2_ref, lk2_ref = lam_refs
    lam = (jnp.exp(jnp.sum(lq1_ref[...] * lk1_ref[...], keepdims=True))
           - jnp.exp(jnp.sum(lq2_ref[...] * lk2_ref[...], keepdims=True)) + lam_init)
    for h in range(hb):
        l0, l1 = carry[4 * h + 1], carry[4 * h + 3]
        ot = acc_ref[2 * h] * (1.0 / l0) - lam * (acc_ref[2 * h + 1] * (1.0 / l1))
        ms = jnp.mean(ot * ot, axis=0, keepdims=True)
        ot = ot * lax.rsqrt(ms + EPS) * g_ref[...] * (1.0 - lam_init)
        o_ref[0, :, h * HEAD_W:(h + 1) * HEAD_W] = ot.T.astype(BF16)


def _diff_attn_kernel(tq, tk, n_k, q_off, lam_init, hb,
                      q_ref, k_ref, vt_ref, lq1_ref, lk1_ref, lq2_ref, lk2_ref, g_ref,
                      o_ref, qm_ref, acc_ref):
    n_int, n_tot, qmin = _kv_range("diff", pl.program_id(2), tq, tk, n_k, q_off)
    _split_maps(q_ref, qm_ref, hb)
    q_chunk = (qmin + lax.broadcasted_iota(jnp.int32, (1, tq), 1)) >> CHUNK_SHIFT

    def step(masked, j, carry):
        k0 = pl.multiple_of(j * tk, tk)
        valid = None
        if masked:
            k_chunk = (k0 + lax.broadcasted_iota(jnp.int32, (tk, 1), 0)) >> CHUNK_SHIFT
            valid = k_chunk <= q_chunk
        kbs = [k_ref[0, pl.ds(k0, tk), h * HEAD_W:(h + 1) * HEAD_W] for h in range(hb)]
        vtbs = [vt_ref[0, h, j] for h in range(hb)]
        return _diff_block(qm_ref, acc_ref, kbs, vtbs, valid, carry)

    carry = _diff_init(acc_ref, tq, hb)
    carry = lax.fori_loop(0, n_int, functools.partial(step, False), carry)
    carry = lax.fori_loop(n_int, n_tot, functools.partial(step, True), carry)
    _diff_finish(carry, acc_ref, (lq1_ref, lk1_ref, lq2_ref, lk2_ref), g_ref, lam_init, o_ref, hb)


def _suffix_matrix(tk):
    s = (lax.broadcasted_iota(jnp.int32, (tk, tk), 1)
         >= lax.broadcasted_iota(jnp.int32, (tk, tk), 0)).astype(BF16)
    return jnp.concatenate([s, s], axis=1)


def _sb_block(q_ref, acc_ref, suffix2, kbs, vtbs, valid, later):
    heads = range(len(kbs))
    ws = [lax.dot_general(kbs[h], q_ref[0, :, h * HEAD_W:(h + 1) * HEAD_W], (((1,), (1,)), ((), ())),
                          preferred_element_type=F32) for h in heads]
    tails = []
    for h in heads:
        sp = jnp.maximum(ws[h], 0.0) + jnp.log(1.0 + jnp.exp2(-jnp.abs(ws[h]))) * LOG2E
        if valid is not None:
            sp = jnp.where(valid, sp, 0.0)
        hi = sp.astype(BF16)
        lo = (sp - hi.astype(F32)).astype(BF16)
        tails.append(jnp.dot(suffix2, jnp.concatenate([hi, lo], axis=0),
                             preferred_element_type=F32))
    weights = []
    for h in heads:
        logit = ws[h] - tails[h] - later[h]
        if valid is not None:
            logit = jnp.where(valid, logit, NEG)
        weights.append(jnp.exp2(logit).astype(BF16))
    for h in heads:
        acc_ref[h] += jnp.dot(vtbs[h], weights[h], preferred_element_type=F32)
    return tuple(later[h] + tails[h][0:1, :] for h in heads)


def _sb_finish(acc_ref, o_ref, hb):
    for h in range(hb):
        o_ref[0, :, h * HEAD_W:(h + 1) * HEAD_W] = acc_ref[h].T.astype(BF16)


def _sb_attn_kernel(tq, tk, n_k, q_off, hb, q_ref, k_ref, vt_ref, o_ref, acc_ref):
    n_int, n_tot, qmin = _kv_range("sb", pl.program_id(2), tq, tk, n_k, q_off)
    q_pos = qmin + lax.broadcasted_iota(jnp.int32, (1, tq), 1)
    suffix2 = _suffix_matrix(tk)
    acc_ref[...] = jnp.zeros_like(acc_ref)

    def step(masked, t, later):
        j = n_tot - 1 - t
        k0 = pl.multiple_of(j * tk, tk)
        valid = (k0 + lax.broadcasted_iota(jnp.int32, (tk, 1), 0)) < q_pos if masked else None
        kbs = [k_ref[0, pl.ds(k0, tk), h * HEAD_W:(h + 1) * HEAD_W] for h in range(hb)]
        vtbs = [vt_ref[0, h, j] for h in range(hb)]
        return _sb_block(q_ref, acc_ref, suffix2, kbs, vtbs, valid, later)

    later = (jnp.zeros((1, tq), F32),) * hb
    later = lax.fori_loop(0, n_tot - n_int, functools.partial(step, True), later)
    lax.fori_loop(n_tot - n_int, n_tot, functools.partial(step, False), later)
    _sb_finish(acc_ref, o_ref, hb)


def _attention(kind, q, k, vt, q_off, extra, cfg):
    b, t_q, width = q.shape
    n_k, tk = vt.shape[2], vt.shape[4]
    t_k = k.shape[1]
    assert t_k == n_k * tk
    tq = _tile(t_q, ATTN_Q_TILE)
    hb = DIFF_HEADS_PER_STEP if kind == "diff" else SB_HEADS_PER_STEP
    kv_mode = dict(pipeline_mode=pl.Buffered(1)) if hb == N_HEADS else {}
    q_spec = pl.BlockSpec((1, tq, hb * HEAD_W), lambda bi, h, i: (bi, i, h))
    in_specs = [q_spec,
                pl.BlockSpec((1, t_k, hb * HEAD_W), lambda bi, h, i: (bi, 0, h), **kv_mode),
                pl.BlockSpec((1, hb, n_k, HEAD_W, tk), lambda bi, h, i: (bi, h, 0, 0, 0), **kv_mode)]
    for e in extra:
        in_specs.append(pl.BlockSpec(e.shape, lambda bi, h, i: (0, 0)))
    if kind == "diff":
        body = functools.partial(_diff_attn_kernel, tq, tk, n_k, q_off, cfg, hb)
        scratch = [pltpu.VMEM((2 * hb, tq, HEAD_W), BF16), pltpu.VMEM((2 * hb, HEAD_W, tq), F32)]
    else:
        body = functools.partial(_sb_attn_kernel, tq, tk, n_k, q_off, hb)
        scratch = [pltpu.VMEM((hb, HEAD_W, tq), F32)]
    return pl.pallas_call(
        body,
        grid=(b, N_HEADS // hb, t_q // tq),
        in_specs=in_specs,
        out_specs=q_spec,
        out_shape=jax.ShapeDtypeStruct((b, t_q, width), BF16),
        scratch_shapes=scratch,
        compiler_params=_params("parallel", "parallel", "arbitrary"),
        name=kind + "_attn",
    )(q, k, vt, *extra)


def _past_operands(kp_ref, vp_ref, k0, tk):
    heads = range(N_HEADS)
    rows = lambda h: pl.ds(k0 * N_HEADS + h, tk, stride=N_HEADS)
    kbs = [kp_ref[0, rows(h), :].astype(BF16) for h in heads]
    vtbs = [vp_ref[0, rows(h), :].T.astype(BF16) for h in heads]
    return kbs, vtbs


def _new_operands(kn_ref, vn_ref):
    heads = range(N_HEADS)
    kbs = [kn_ref[0, :, h * HEAD_W:(h + 1) * HEAD_W].astype(BF16) for h in heads]
    vtbs = [vn_ref[0, :, h * HEAD_W:(h + 1) * HEAD_W].T.astype(BF16) for h in heads]
    return kbs, vtbs


def _new_positions(q_off, t_new, tq):
    k_idx = lax.broadcasted_iota(jnp.int32, (t_new, 1), 0)
    q_idx = lax.broadcasted_iota(jnp.int32, (1, tq), 1)
    return k_idx, q_off + k_idx, q_off + q_idx


def _diff_cached_kernel(tq, tk, n_past, q_off, t_real, lam_init,
                        q_ref, kp_ref, vp_ref, kn_ref, vn_ref, lq1_ref, lk1_ref, lq2_ref, lk2_ref, g_ref,
                        o_ref, qm_ref, acc_ref):
    _split_maps(q_ref, qm_ref, N_HEADS)
    carry = _diff_init(acc_ref, tq, N_HEADS)

    def past(j, carry):
        kbs, vtbs = _past_operands(kp_ref, vp_ref, pl.multiple_of(j * tk, tk), tk)
        return _diff_block(qm_ref, acc_ref, kbs, vtbs, None, carry)

    carry = lax.fori_loop(0, n_past, past, carry)
    k_idx, k_pos, q_pos = _new_positions(q_off, kn_ref.shape[1], tq)
    valid = ((k_pos >> CHUNK_SHIFT) <= (q_pos >> CHUNK_SHIFT)) & (k_idx < t_real)
    kbs, vtbs = _new_operands(kn_ref, vn_ref)
    carry = _diff_block(qm_ref, acc_ref, kbs, vtbs, valid, carry)
    _diff_finish(carry, acc_ref, (lq1_ref, lk1_ref, lq2_ref, lk2_ref), g_ref, lam_init, o_ref, N_HEADS)


def _sb_cached_kernel(tq, tk, n_past, q_off, t_real, q_ref, kp_ref, vp_ref, kn_ref, vn_ref, o_ref, acc_ref):
    acc_ref[...] = jnp.zeros_like(acc_ref)
    t_new = kn_ref.shape[1]
    k_idx, k_pos, q_pos = _new_positions(q_off, t_new, tq)
    valid = (k_pos < q_pos) & (k_idx < t_real)
    kbs, vtbs = _new_operands(kn_ref, vn_ref)
    later = (jnp.zeros((1, tq), F32),) * N_HEADS
    later = _sb_block(q_ref, acc_ref, _suffix_matrix(t_new), kbs, vtbs, valid, later)
    suffix2 = _suffix_matrix(tk)

    def past(t, later):
        kbs, vtbs = _past_operands(kp_ref, vp_ref, pl.multiple_of((n_past - 1 - t) * tk, tk), tk)
        return _sb_block(q_ref, acc_ref, suffix2, kbs, vtbs, None, later)

    lax.fori_loop(0, n_past, past, later)
    _sb_finish(acc_ref, o_ref, N_HEADS)


def _attention_cached(kind, q, k_past, v_past, k_new, v_new, t_real, extra, lam_init):
    b, tq, width = q.shape
    p_len, t_new = k_past.shape[1] // N_HEADS, k_new.shape[1]
    tk = _tile(p_len, SB_K_TILE)
    n_past = p_len // tk
    row3 = lambda t: pl.BlockSpec((1, t, width), lambda bi: (bi, 0, 0))
    past_spec = pl.BlockSpec((1, p_len * N_HEADS, HEAD_W), lambda bi: (bi, 0, 0))
    in_specs = [row3(tq), past_spec, past_spec, row3(t_new), row3(t_new)]
    for e in extra:
        in_specs.append(pl.BlockSpec(e.shape, lambda bi: (0, 0)))
    if kind == "diff":
        body = functools.partial(_diff_cached_kernel, tq, tk, n_past, p_len, t_real, lam_init)
        scratch = [pltpu.VMEM((2 * N_HEADS, tq, HEAD_W), BF16), pltpu.VMEM((2 * N_HEADS, HEAD_W, tq), F32)]
    else:
        body = functools.partial(_sb_cached_kernel, tq, tk, n_past, p_len, t_real)
        scratch = [pltpu.VMEM((N_HEADS, HEAD_W, tq), F32)]
    return pl.pallas_call(
        body,
        grid=(b,),
        in_specs=in_specs,
        out_specs=row3(tq),
        out_shape=jax.ShapeDtypeStruct((b, tq, width), BF16),
        scratch_shapes=scratch,
        compiler_params=_params("parallel"),
        name=kind + "_attn_cached",
    )(q, k_past, v_past, k_new, v_new, *extra)


def _out_kernel(oa_ref, ob_ref, ga_ref, gb_ref, h_ref, wua_ref, wub_ref, wo_ref, n_ref, o_ref):
    pa = jnp.dot(oa_ref[...], wua_ref[...], preferred_element_type=F32)
    pb = jnp.dot(ob_ref[...], wub_ref[...], preferred_element_type=F32)
    merged = (ga_ref[...] * pa + gb_ref[...] * pb).astype(BF16)
    r = jnp.dot(merged, wo_ref[...], preferred_element_type=F32)
    o_ref[...] = h_ref[...] + _rms(r, n_ref[...])


def _merge_out(oa, ob, ga, gb, h, wua, wub, wo, n_post):
    n, d = h.shape
    w_attn = oa.shape[1]
    tm = _tile(n, OUT_TOKEN_TILE)
    row = lambda wd: pl.BlockSpec((tm, wd), lambda i: (i, 0))
    const = lambda shape: pl.BlockSpec(shape, lambda i: (0, 0), pipeline_mode=pl.Buffered(1))
    return pl.pallas_call(
        _out_kernel,
        grid=(n // tm,),
        in_specs=[row(w_attn), row(w_attn), row(d), row(d), row(d),
                  const(wua.shape), const(wub.shape), const(wo.shape), const(n_post.shape)],
        out_specs=row(d),
        out_shape=jax.ShapeDtypeStruct((n, d), F32),
        compiler_params=_params("parallel"),
        name="merge_out",
    )(oa, ob, ga, gb, h, wua, wub, wo, n_post)


def _rope_tables(pos):
    half = ROT_DIM // 2
    inv_freq = jnp.power(ROPE_THETA, -jnp.arange(0, ROT_DIM, 2, dtype=F32) / ROT_DIM)
    ang = pos.astype(F32)[:, None] * inv_freq[None, :]
    cos, sin = jnp.cos(ang), jnp.sin(ang)
    t = pos.shape[0]
    ones = jnp.ones((t, MAP_W - ROT_DIM), F32)
    zeros = jnp.zeros((t, MAP_W - ROT_DIM), F32)
    zh = jnp.zeros((t, half), F32)
    c = jnp.concatenate([cos, cos, ones], axis=1)
    sa = jnp.concatenate([-sin, zh, zeros], axis=1)
    sb = jnp.concatenate([zh, sin, zeros], axis=1)
    return tuple(jnp.concatenate([m, m], axis=1) for m in (c, sa, sb))


def _pad_ffn(w, axis):
    f = w.shape[axis]
    fp = -(-f // FFN_HIDDEN_TILE) * FFN_HIDDEN_TILE
    shape = list(w.shape)
    shape[axis] = fp - f
    return jnp.concatenate([w.astype(BF16), jnp.zeros(shape, BF16)], axis=axis)


def _layer(x, pos, past, lam_init, p):
    b, t, d = x.shape
    n = b * t
    w_a = N_HEADS * HEAD_W
    vec = lambda v: v.reshape(1, -1)
    w_in = p["w_in"].astype(BF16)
    seg = lambda lo, hi: w_in[:, lo:hi]

    h1, xn = _ffn(x.reshape(n, d), vec(p["n1a"]), vec(p["n1b"]), vec(p["nma"]),
                  _pad_ffn(p["f1g"], 1), _pad_ffn(p["f1u"], 1), _pad_ffn(p["f1d"], 0))

    tables = tuple(jnp.tile(m, (b, 1)) for m in _rope_tables(pos))
    q_a = _proj(xn, seg(0, w_a), "q_rope", tables=tables, scale=MAP_W ** -0.5 * LOG2E)
    q_b = _proj(xn, seg(3 * w_a, 4 * w_a), "q", scale=HEAD_W ** -0.5 * LOG2E)
    g_a = _proj(xn, seg(6 * w_a, 6 * w_a + d), "gate", tile=GATE_TOKEN_TILE)
    g_b = _proj(xn, seg(6 * w_a + d, 6 * w_a + 2 * d), "gate", tile=GATE_TOKEN_TILE)

    shape3 = lambda a: a.reshape(b, t, w_a)
    diff_extra = [vec(p["lq1"]), vec(p["lk1"]), vec(p["lq2"]), vec(p["lk2"]), p["subln_g"].reshape(-1, 1)]
    if past is None:
        ka32, ka16 = _proj(xn, seg(w_a, 2 * w_a), "k_rope", tables=tables)
        kb32, kb16 = _proj(xn, seg(4 * w_a, 5 * w_a), "k")
        va32, vat = _proj(xn, seg(2 * w_a, 3 * w_a), "v_t", seq=t, tk=_tile(t, DIFF_K_TILE))
        vb32, vbt = _proj(xn, seg(5 * w_a, 6 * w_a), "v_t", seq=t, tk=_tile(t, SB_K_TILE))
        o_a = _attention("diff", shape3(q_a), shape3(ka16), vat, 0, diff_extra, lam_init)
        o_b = _attention("sb", shape3(q_b), shape3(kb16), vbt, 0, [], None)
    else:
        ka32 = _proj(xn, seg(w_a, 2 * w_a), "kv_rope", tables=tables)
        kb32 = _proj(xn, seg(4 * w_a, 5 * w_a), "kv")
        va32 = _proj(xn, seg(2 * w_a, 3 * w_a), "kv")
        vb32 = _proj(xn, seg(5 * w_a, 6 * w_a), "kv")
        pka, pva, pkb, pvb = past
        t_p = -(-t // V7X_LANES) * V7X_LANES
        pad_rows = lambda a: jnp.pad(shape3(a), ((0, 0), (0, t_p - t), (0, 0)))
        by_head = lambda c: c.reshape(c.shape[0], c.shape[1] * N_HEADS, HEAD_W)
        o_a = _attention_cached("diff", pad_rows(q_a), by_head(pka), by_head(pva), pad_rows(ka32),
                                pad_rows(va32), t, diff_extra, lam_init)
        o_b = _attention_cached("sb", pad_rows(q_b), by_head(pkb), by_head(pvb), pad_rows(kb32),
                                pad_rows(vb32), t, [], None)
    o_a = o_a[:, :t].reshape(n, w_a)
    o_b = o_b[:, :t].reshape(n, w_a)

    h2 = _merge_out(o_a, o_b, g_a, g_b, h1, p["w_up_a"].astype(BF16), p["w_up_b"].astype(BF16),
                    p["w_o"].astype(BF16), vec(p["nmb"]))
    y = _ffn(h2, vec(p["n2a"]), vec(p["n2b"]), None,
             _pad_ffn(p["f2g"], 1), _pad_ffn(p["f2u"], 1), _pad_ffn(p["f2d"], 0))
    rows = (ka32.reshape(b, t, N_HEADS, 2, MAP_W), va32.reshape(b, t, N_HEADS, HEAD_W),
            kb32.reshape(b, t, N_HEADS, HEAD_W), vb32.reshape(b, t, N_HEADS, HEAD_W))
    return y.reshape(b, t, d), rows


def _stack(xs):
    return xs[0][None] if len(xs) == 1 else jnp.stack(xs)


def kernel(x_prompt, x_sample, cache_diff_k, cache_diff_v, cache_sb_k, cache_sb_v, w_in, w_up_a, w_up_b, w_o, lam_q1, lam_k1, lam_q2, lam_k2, subln_g, norm_ffn1_pre, norm_ffn1_post, norm_mix_pre, norm_mix_post, norm_ffn2_pre, norm_ffn2_post, ffn1_w_gate, ffn1_w_up, ffn1_w_down, ffn2_w_gate, ffn2_w_up, ffn2_w_down):
    depth = w_in.shape[0]
    pos_p = jnp.arange(x_prompt.shape[1], dtype=jnp.int32)
    pos_s = cache_diff_k.shape[2] + jnp.arange(x_sample.shape[1], dtype=jnp.int32)
    hp, hs = x_prompt, x_sample
    rows_p, rows_s = [], []
    for l in range(depth):
        lam_init = 0.8 - 0.6 * math.exp(-0.3 * l)
        p = dict(w_in=w_in[l], w_up_a=w_up_a[l], w_up_b=w_up_b[l], w_o=w_o[l],
                 lq1=lam_q1[l], lk1=lam_k1[l], lq2=lam_q2[l], lk2=lam_k2[l], subln_g=subln_g[l],
                 n1a=norm_ffn1_pre[l], n1b=norm_ffn1_post[l], nma=norm_mix_pre[l], nmb=norm_mix_post[l],
                 n2a=norm_ffn2_pre[l], n2b=norm_ffn2_post[l],
                 f1g=ffn1_w_gate[l], f1u=ffn1_w_up[l], f1d=ffn1_w_down[l],
                 f2g=ffn2_w_gate[l], f2u=ffn2_w_up[l], f2d=ffn2_w_down[l])
        hp, rp = _layer(hp, pos_p, None, lam_init, p)
        past = (cache_diff_k[l], cache_diff_v[l], cache_sb_k[l], cache_sb_v[l])
        hs, rs = _layer(hs, pos_s, past, lam_init, p)
        rows_p.append(rp)
        rows_s.append(rs)
    outs_p = [_stack([r[i] for r in rows_p]) for i in range(4)]
    outs_s = [_stack([r[i] for r in rows_s]) for i in range(4)]
    return (hp, hs, *outs_p, *outs_s)
```

```python
import functools
import math

import jax
import jax.numpy as jnp
from jax import lax
from jax.experimental import pallas as pl
from jax.experimental.pallas import tpu as pltpu

F32 = jnp.float32
BF16 = jnp.bfloat16

EPS = 1e-6
CHUNK = 64
CHUNK_SHIFT = CHUNK.bit_length() - 1
assert CHUNK == 1 << CHUNK_SHIFT
N_HEADS = 8
HEAD_W = 128
MAP_W = 64
ROT_DIM = 16
ROPE_THETA = 500000.0
NEG = -1e30

V7X_LANES = 128
V7X_MXU_DIM = 256
V7X_VMEM_BYTES = 64 * 1024 * 1024
VMEM_LIMIT_BYTES = V7X_VMEM_BYTES - 8 * 1024 * 1024

FFN_TOKEN_TILE = 512
FFN_HIDDEN_TILE = 512
PROJ_TOKEN_TILE = 1024
GATE_TOKEN_TILE = 512
OUT_TOKEN_TILE = 256
ATTN_Q_TILE = 512
DIFF_K_TILE = 512
SB_K_TILE = V7X_MXU_DIM
DIFF_HEADS_PER_STEP = 4
SB_HEADS_PER_STEP = 8
LOG2E = math.log2(math.e)


def _tile(n, target):
    t = min(n, target)
    while n % t:
        t -= 1
    return t


def _params(*sem):
    return pltpu.CompilerParams(dimension_semantics=sem, vmem_limit_bytes=VMEM_LIMIT_BYTES)


def _rms(x, g):
    return x * lax.rsqrt(jnp.mean(x * x, axis=-1, keepdims=True) + EPS) * g


def _ffn_kernel(n_f, has_next, x_ref, npre_ref, npost_ref, *refs):
    if has_next:
        nnext_ref, wg_ref, wu_ref, wd_ref, h_ref, xn_ref, xs_ref, acc_ref = refs
    else:
        wg_ref, wu_ref, wd_ref, h_ref, xs_ref, acc_ref = refs
    f = pl.program_id(1)

    @pl.when(f == 0)
    def _():
        xs_ref[...] = _rms(x_ref[...], npre_ref[...]).astype(BF16)
        acc_ref[...] = jnp.zeros_like(acc_ref)

    xs = xs_ref[...]
    g = jnp.dot(xs, wg_ref[...], preferred_element_type=F32)
    u = jnp.dot(xs, wu_ref[...], preferred_element_type=F32)
    a = (g * jax.nn.sigmoid(g) * u).astype(BF16)
    acc_ref[...] += jnp.dot(a, wd_ref[...], preferred_element_type=F32)

    @pl.when(f == n_f - 1)
    def _():
        h = x_ref[...] + 0.5 * _rms(acc_ref[...], npost_ref[...])
        h_ref[...] = h
        if has_next:
            xn_ref[...] = _rms(h, nnext_ref[...]).astype(BF16)


def _ffn(x, n_pre, n_post, n_next, wg, wu, wd):
    n, d = x.shape
    fp = wg.shape[1]
    tm = _tile(n, FFN_TOKEN_TILE)
    tf = _tile(fp, FFN_HIDDEN_TILE)
    n_f = fp // tf
    has_next = n_next is not None
    row = pl.BlockSpec((tm, d), lambda i, f: (i, 0))
    vec = pl.BlockSpec((1, d), lambda i, f: (0, 0))
    in_specs = [row, vec, vec] + ([vec] if has_next else []) + [
        pl.BlockSpec((d, tf), lambda i, f: (0, f)),
        pl.BlockSpec((d, tf), lambda i, f: (0, f)),
        pl.BlockSpec((tf, d), lambda i, f: (f, 0)),
    ]
    out_shape = [jax.ShapeDtypeStruct((n, d), F32)]
    out_specs = [row]
    if has_next:
        out_shape.append(jax.ShapeDtypeStruct((n, d), BF16))
        out_specs.append(row)
    args = [x, n_pre, n_post] + ([n_next] if has_next else []) + [wg, wu, wd]
    res = pl.pallas_call(
        functools.partial(_ffn_kernel, n_f, has_next),
        grid=(n // tm, n_f),
        in_specs=in_specs,
        out_specs=out_specs,
        out_shape=out_shape,
        scratch_shapes=[pltpu.VMEM((tm, d), BF16), pltpu.VMEM((tm, d), F32)],
        compiler_params=_params("parallel", "arbitrary"),
        name="ffn_next" if has_next else "ffn",
    )(*args)
    return res if has_next else res[0]


def _rope(r, c, sa, sb):
    heads = []
    for h in range(r.shape[1] // HEAD_W):
        x = r[:, h * HEAD_W:(h + 1) * HEAD_W]
        up = pltpu.roll(x, HEAD_W - ROT_DIM // 2, 1)
        dn = pltpu.roll(x, ROT_DIM // 2, 1)
        heads.append(x * c + up * sa + dn * sb)
    return jnp.concatenate(heads, axis=1)


def _proj_kernel(mode, scale, tk, x_ref, w_ref, *refs):
    r = jnp.dot(x_ref[...], w_ref[...], preferred_element_type=F32)
    if mode in ("q_rope", "k_rope", "kv_rope"):
        c_ref, sa_ref, sb_ref = refs[:3]
        refs = refs[3:]
        r = _rope(r, c_ref[...], sa_ref[...], sb_ref[...])
    if mode in ("q", "q_rope"):
        (q_ref,) = refs
        q_ref[...] = (r * scale).astype(BF16)
    elif mode == "gate":
        (g_ref,) = refs
        g_ref[...] = jax.nn.sigmoid(r)
    elif mode == "k":
        k32_ref, k16_ref = refs
        k32_ref[...] = r
        k16_ref[...] = r.astype(BF16)
    elif mode == "k_rope":
        kt_ref, k16_ref = refs
        k16_ref[...] = r.astype(BF16)
        for h in range(N_HEADS):
            kt_ref[0, h] = r[:, h * HEAD_W:(h + 1) * HEAD_W].T
    elif mode in ("kv", "kv_rope"):
        (r32_ref,) = refs
        r32_ref[...] = r
    else:
        v32_ref, vt_ref = refs
        v32_ref[...] = r
        for h in range(N_HEADS):
            for jb in range(r.shape[0] // tk):
                blk = r[jb * tk:(jb + 1) * tk, h * HEAD_W:(h + 1) * HEAD_W]
                vt_ref[0, h, jb] = blk.T.astype(BF16)


def _proj(xn, w, mode, *, tables=None, scale=1.0, seq=None, tk=None, tile=PROJ_TOKEN_TILE):
    n, d = xn.shape
    width = w.shape[1]
    tm = _tile(n if seq is None else seq, tile)
    row = lambda wd: pl.BlockSpec((tm, wd), lambda i: (i, 0))
    in_specs = [row(d), pl.BlockSpec((d, width), lambda i: (0, 0))]
    args = [xn, w]
    if mode in ("q_rope", "k_rope", "kv_rope"):
        in_specs += [row(HEAD_W)] * 3
        args += list(tables)
    if mode in ("q", "q_rope"):
        out_shape = [jax.ShapeDtypeStruct((n, width), BF16)]
        out_specs = [row(width)]
    elif mode in ("gate", "kv", "kv_rope"):
        out_shape = [jax.ShapeDtypeStruct((n, width), F32)]
        out_specs = [row(width)]
    elif mode == "k":
        out_shape = [jax.ShapeDtypeStruct((n, width), F32), jax.ShapeDtypeStruct((n, width), BF16)]
        out_specs = [row(width), row(width)]
    elif mode == "k_rope":
        assert seq % tm == 0
        per_seq = seq // tm
        out_shape = [jax.ShapeDtypeStruct((n // seq, N_HEADS, HEAD_W, seq), F32),
                     jax.ShapeDtypeStruct((n, width), BF16)]
        out_specs = [pl.BlockSpec((1, N_HEADS, HEAD_W, tm), lambda i: (i // per_seq, 0, 0, i % per_seq)),
                     row(width)]
    else:
        assert mode == "v_t" and tm % tk == 0 and seq % tm == 0
        per_seq = seq // tm
        out_shape = [jax.ShapeDtypeStruct((n, width), F32),
                     jax.ShapeDtypeStruct((n // seq, N_HEADS, seq // tk, HEAD_W, tk), BF16)]
        out_specs = [row(width),
                     pl.BlockSpec((1, N_HEADS, tm // tk, HEAD_W, tk),
                                  lambda i: (i // per_seq, 0, i % per_seq, 0, 0))]
    res = pl.pallas_call(
        functools.partial(_proj_kernel, mode, scale, tk),
        grid=(n // tm,),
        in_specs=in_specs,
        out_specs=out_specs,
        out_shape=out_shape,
        compiler_params=_params("parallel"),
        name="proj_" + mode,
    )(*args)
    return res[0] if len(res) == 1 else res


def _kv_range(kind, i, tq, tk, n_k, q_off):
    qmin = q_off + i * tq
    qmax = qmin + tq - 1
    if kind == "diff":
        n_int = ((qmin // CHUNK + 1) * CHUNK) // tk
        last = ((qmax // CHUNK + 1) * CHUNK - 1) // tk
    else:
        n_int = qmin // tk
        last = (qmax - 1) // tk
    return jnp.minimum(n_int, n_k), jnp.minimum(last + 1, n_k), qmin


def _split_maps(q_ref, qm_ref, hb):
    lane = lax.broadcasted_iota(jnp.int32, (1, HEAD_W), 1)
    for h in range(hb):
        q = q_ref[0, :, h * HEAD_W:(h + 1) * HEAD_W]
        zero = jnp.zeros_like(q)
        qm_ref[2 * h] = jnp.where(lane < MAP_W, q, zero)
        qm_ref[2 * h + 1] = jnp.where(lane >= MAP_W, q, zero)


def _diff_block(qm_ref, acc_ref, kbs, vtbs, valid, carry):
    chains = range(2 * len(kbs))
    scores = [lax.dot_general(kbs[c // 2], qm_ref[c], (((1,), (1,)), ((), ())),
                              preferred_element_type=F32) for c in chains]
    new, probs, alphas = [], [], []
    for c in chains:
        m, l = carry[2 * c], carry[2 * c + 1]
        s = scores[c] if valid is None else jnp.where(valid, scores[c], NEG)
        m_new = jnp.maximum(m, jnp.max(s, axis=0, keepdims=True))
        alpha = jnp.exp2(m - m_new)
        p = jnp.exp2(s - m_new)
        new += [m_new, alpha * l + jnp.sum(p, axis=0, keepdims=True)]
        probs.append(p.astype(BF16))
        alphas.append(alpha)
    for c in chains:
        acc_ref[c] = alphas[c] * acc_ref[c] + jnp.dot(vtbs[c // 2], probs[c], preferred_element_type=F32)
    return tuple(new)


def _diff_init(acc_ref, tq, hb):
    acc_ref[...] = jnp.zeros_like(acc_ref)
    return (jnp.full((1, tq), NEG, F32), jnp.zeros((1, tq), F32)) * (2 * hb)


def _diff_finish(carry, acc_ref, lam_refs, g_ref, lam_init, o_ref, hb):
    lq1_ref, lk1_ref, lq2_ref, lk2_ref = lam_refs
    lam = (jnp.exp(jnp.sum(lq1_ref[...] * lk1_ref[...], keepdims=True))
           - jnp.exp(jnp.sum(lq2_ref[...] * lk2_ref[...], keepdims=True)) + lam_init)
    for h in range(hb):
        l0, l1 = carry[4 * h + 1], carry[4 * h + 3]
        ot = acc_ref[2 * h] * (1.0 / l0) - lam * (acc_ref[2 * h + 1] * (1.0 / l1))
        ms = jnp.mean(ot * ot, axis=0, keepdims=True)
        ot = ot * lax.rsqrt(ms + EPS) * g_ref[...] * (1.0 - lam_init)
        o_ref[0, :, h * HEAD_W:(h + 1) * HEAD_W] = ot.T.astype(BF16)


def _diff_attn_kernel(tq, tk, n_k, q_off, lam_init, hb,
                      q_ref, k_ref, vt_ref, lq1_ref, lk1_ref, lq2_ref, lk2_ref, g_ref,
                      o_ref, qm_ref, acc_ref):
    n_int, n_tot, qmin = _kv_range("diff", pl.program_id(2), tq, tk, n_k, q_off)
    _split_maps(q_ref, qm_ref, hb)
    q_chunk = (qmin + lax.broadcasted_iota(jnp.int32, (1, tq), 1)) >> CHUNK_SHIFT

    def step(masked, j, carry):
        k0 = pl.multiple_of(j * tk, tk)
        valid = None
        if masked:
            k_chunk = (k0 + lax.broadcasted_iota(jnp.int32, (tk, 1), 0)) >> CHUNK_SHIFT
            valid = k_chunk <= q_chunk
        kbs = [k_ref[0, pl.ds(k0, tk), h * HEAD_W:(h + 1) * HEAD_W] for h in range(hb)]
        vtbs = [vt_ref[0, h, j] for h in range(hb)]
        return _diff_block(qm_ref, acc_ref, kbs, vtbs, valid, carry)

    carry = _diff_init(acc_ref, tq, hb)
    carry = lax.fori_loop(0, n_int, functools.partial(step, False), carry)
    carry = lax.fori_loop(n_int, n_tot, functools.partial(step, True), carry)
    _diff_finish(carry, acc_ref, (lq1_ref, lk1_ref, lq2_ref, lk2_ref), g_ref, lam_init, o_ref, hb)


def _suffix_matrix(tk):
    s = (lax.broadcasted_iota(jnp.int32, (tk, tk), 1)
         >= lax.broadcasted_iota(jnp.int32, (tk, tk), 0)).astype(BF16)
    return jnp.concatenate([s, s], axis=1)


def _sb_block(q_ref, acc_ref, suffix2, kbs, vtbs, valid, later):
    heads = range(len(kbs))
    ws = [lax.dot_general(kbs[h], q_ref[0, :, h * HEAD_W:(h + 1) * HEAD_W], (((1,), (1,)), ((), ())),
                          preferred_element_type=F32) for h in heads]
    tails = []
    for h in heads:
        sp = jnp.maximum(ws[h], 0.0) + jnp.log(1.0 + jnp.exp2(-jnp.abs(ws[h]))) * LOG2E
        if valid is not None:
            sp = jnp.where(valid, sp, 0.0)
        hi = sp.astype(BF16)
        lo = (sp - hi.astype(F32)).astype(BF16)
        tails.append(jnp.dot(suffix2, jnp.concatenate([hi, lo], axis=0),
                             preferred_element_type=F32))
    weights = []
    for h in heads:
        logit = ws[h] - tails[h] - later[h]
        if valid is not None:
            logit = jnp.where(valid, logit, NEG)
        weights.append(jnp.exp2(logit).astype(BF16))
    for h in heads:
        acc_ref[h] += jnp.dot(vtbs[h], weights[h], preferred_element_type=F32)
    return tuple(later[h] + tails[h][0:1, :] for h in heads)


def _sb_finish(acc_ref, o_ref, hb):
    for h in range(hb):
        o_ref[0, :, h * HEAD_W:(h + 1) * HEAD_W] = acc_ref[h].T.astype(BF16)


def _sb_attn_kernel(tq, tk, n_k, q_off, hb, q_ref, k_ref, vt_ref, o_ref, acc_ref):
    n_int, n_tot, qmin = _kv_range("sb", pl.program_id(2), tq, tk, n_k, q_off)
    q_pos = qmin + lax.broadcasted_iota(jnp.int32, (1, tq), 1)
    suffix2 = _suffix_matrix(tk)
    acc_ref[...] = jnp.zeros_like(acc_ref)

    def step(masked, t, later):
        j = n_tot - 1 - t
        k0 = pl.multiple_of(j * tk, tk)
        valid = (k0 + lax.broadcasted_iota(jnp.int32, (tk, 1), 0)) < q_pos if masked else None
        kbs = [k_ref[0, pl.ds(k0, tk), h * HEAD_W:(h + 1) * HEAD_W] for h in range(hb)]
        vtbs = [vt_ref[0, h, j] for h in range(hb)]
        return _sb_block(q_ref, acc_ref, suffix2, kbs, vtbs, valid, later)

    later = (jnp.zeros((1, tq), F32),) * hb
    later = lax.fori_loop(0, n_tot - n_int, functools.partial(step, True), later)
    lax.fori_loop(n_tot - n_int, n_tot, functools.partial(step, False), later)
    _sb_finish(acc_ref, o_ref, hb)


def _attention(kind, q, k, vt, q_off, extra, cfg):
    b, t_q, width = q.shape
    n_k, tk = vt.shape[2], vt.shape[4]
    t_k = k.shape[1]
    assert t_k == n_k * tk
    tq = _tile(t_q, ATTN_Q_TILE)
    hb = DIFF_HEADS_PER_STEP if kind == "diff" else SB_HEADS_PER_STEP
    kv_mode = dict(pipeline_mode=pl.Buffered(1)) if hb == N_HEADS else {}
    q_spec = pl.BlockSpec((1, tq, hb * HEAD_W), lambda bi, h, i: (bi, i, h))
    in_specs = [q_spec,
                pl.BlockSpec((1, t_k, hb * HEAD_W), lambda bi, h, i: (bi, 0, h), **kv_mode),
                pl.BlockSpec((1, hb, n_k, HEAD_W, tk), lambda bi, h, i: (bi, h, 0, 0, 0), **kv_mode)]
    for e in extra:
        in_specs.append(pl.BlockSpec(e.shape, lambda bi, h, i: (0, 0)))
    if kind == "diff":
        body = functools.partial(_diff_attn_kernel, tq, tk, n_k, q_off, cfg, hb)
        scratch = [pltpu.VMEM((2 * hb, tq, HEAD_W), BF16), pltpu.VMEM((2 * hb, HEAD_W, tq), F32)]
    else:
        body = functools.partial(_sb_attn_kernel, tq, tk, n_k, q_off, hb)
        scratch = [pltpu.VMEM((hb, HEAD_W, tq), F32)]
    return pl.pallas_call(
        body,
        grid=(b, N_HEADS // hb, t_q // tq),
        in_specs=in_specs,
        out_specs=q_spec,
        out_shape=jax.ShapeDtypeStruct((b, t_q, width), BF16),
        scratch_shapes=scratch,
        compiler_params=_params("parallel", "parallel", "arbitrary"),
        name=kind + "_attn",
    )(q, k, vt, *extra)


def _past_operands(kp_ref, vp_ref, k0, tk):
    heads = range(N_HEADS)
    rows = lambda h: pl.ds(k0 * N_HEADS + h, tk, stride=N_HEADS)
    kbs = [kp_ref[0, rows(h), :].astype(BF16) for h in heads]
    vtbs = [vp_ref[0, rows(h), :].T.astype(BF16) for h in heads]
    return kbs, vtbs


def _new_operands(kn_ref, vn_ref):
    heads = range(N_HEADS)
    kbs = [kn_ref[0, :, h * HEAD_W:(h + 1) * HEAD_W].astype(BF16) for h in heads]
    vtbs = [vn_ref[0, :, h * HEAD_W:(h + 1) * HEAD_W].T.astype(BF16) for h in heads]
    return kbs, vtbs


def _new_positions(q_off, t_new, tq):
    k_idx = lax.broadcasted_iota(jnp.int32, (t_new, 1), 0)
    q_idx = lax.broadcasted_iota(jnp.int32, (1, tq), 1)
    return k_idx, q_off + k_idx, q_off + q_idx


def _diff_cached_kernel(tq, tk, n_past, q_off, t_real, lam_init,
                        q_ref, kp_ref, vp_ref, kn_ref, vn_ref, lq1_ref, lk1_ref, lq2_ref, lk2_ref, g_ref,
                        o_ref, qm_ref, acc_ref):
    _split_maps(q_ref, qm_ref, N_HEADS)
    carry = _diff_init(acc_ref, tq, N_HEADS)
    heads = range(N_HEADS)
    for j in range(n_past):
        kbs = [kp_ref[0, h, :, j * tk:(j + 1) * tk].T.astype(BF16) for h in heads]
        vtbs = [vp_ref[0, pl.ds(j * tk * N_HEADS + h, tk, stride=N_HEADS), :].T.astype(BF16) for h in heads]
        carry = _diff_block(qm_ref, acc_ref, kbs, vtbs, None, carry)
    k_idx, k_pos, q_pos = _new_positions(q_off, kn_ref.shape[1], tq)
    valid = ((k_pos >> CHUNK_SHIFT) <= (q_pos >> CHUNK_SHIFT)) & (k_idx < t_real)
    kbs, vtbs = _new_operands(kn_ref, vn_ref)
    carry = _diff_block(qm_ref, acc_ref, kbs, vtbs, valid, carry)
    _diff_finish(carry, acc_ref, (lq1_ref, lk1_ref, lq2_ref, lk2_ref), g_ref, lam_init, o_ref, N_HEADS)


def _sb_cached_kernel(tq, tk, n_past, q_off, t_real, q_ref, kp_ref, vp_ref, kn_ref, vn_ref, o_ref, acc_ref):
    acc_ref[...] = jnp.zeros_like(acc_ref)
    t_new = kn_ref.shape[1]
    k_idx, k_pos, q_pos = _new_positions(q_off, t_new, tq)
    valid = (k_pos < q_pos) & (k_idx < t_real)
    kbs, vtbs = _new_operands(kn_ref, vn_ref)
    later = (jnp.zeros((1, tq), F32),) * N_HEADS
    later = _sb_block(q_ref, acc_ref, _suffix_matrix(t_new), kbs, vtbs, valid, later)
    suffix2 = _suffix_matrix(tk)

    def past(t, later):
        kbs, vtbs = _past_operands(kp_ref, vp_ref, pl.multiple_of((n_past - 1 - t) * tk, tk), tk)
        return _sb_block(q_ref, acc_ref, suffix2, kbs, vtbs, None, later)

    lax.fori_loop(0, n_past, past, later)
    _sb_finish(acc_ref, o_ref, N_HEADS)


def _attention_cached(kind, q, k_past, v_past, k_new, v_new, t_real, extra, lam_init):
    b, tq, width = q.shape
    p_len, t_new = v_past.shape[1] // N_HEADS, k_new.shape[1]
    tk = _tile(p_len, SB_K_TILE)
    n_past = p_len // tk
    row3 = lambda t: pl.BlockSpec((1, t, width), lambda bi: (bi, 0, 0))
    past_spec = pl.BlockSpec((1, p_len * N_HEADS, HEAD_W), lambda bi: (bi, 0, 0))
    k_spec = pl.BlockSpec((1, N_HEADS, HEAD_W, p_len), lambda bi: (bi, 0, 0, 0)) if kind == "diff" else past_spec
    in_specs = [row3(tq), k_spec, past_spec, row3(t_new), row3(t_new)]
    for e in extra:
        in_specs.append(pl.BlockSpec(e.shape, lambda bi: (0, 0)))
    if kind == "diff":
        body = functools.partial(_diff_cached_kernel, tq, tk, n_past, p_len, t_real, lam_init)
        scratch = [pltpu.VMEM((2 * N_HEADS, tq, HEAD_W), BF16), pltpu.VMEM((2 * N_HEADS, HEAD_W, tq), F32)]
    else:
        body = functools.partial(_sb_cached_kernel, tq, tk, n_past, p_len, t_real)
        scratch = [pltpu.VMEM((N_HEADS, HEAD_W, tq), F32)]
    return pl.pallas_call(
        body,
        grid=(b,),
        in_specs=in_specs,
        out_specs=row3(tq),
        out_shape=jax.ShapeDtypeStruct((b, tq, width), BF16),
        scratch_shapes=scratch,
        compiler_params=_params("parallel"),
        name=kind + "_attn_cached",
    )(q, k_past, v_past, k_new, v_new, *extra)


def _out_kernel(oa_ref, ob_ref, ga_ref, gb_ref, h_ref, wua_ref, wub_ref, wo_ref, n_ref, o_ref):
    pa = jnp.dot(oa_ref[...], wua_ref[...], preferred_element_type=F32)
    pb = jnp.dot(ob_ref[...], wub_ref[...], preferred_element_type=F32)
    merged = (ga_ref[...] * pa + gb_ref[...] * pb).astype(BF16)
    r = jnp.dot(merged, wo_ref[...], preferred_element_type=F32)
    o_ref[...] = h_ref[...] + _rms(r, n_ref[...])


def _merge_out(oa, ob, ga, gb, h, wua, wub, wo, n_post):
    n, d = h.shape
    w_attn = oa.shape[1]
    tm = _tile(n, OUT_TOKEN_TILE)
    row = lambda wd: pl.BlockSpec((tm, wd), lambda i: (i, 0))
    const = lambda shape: pl.BlockSpec(shape, lambda i: (0, 0), pipeline_mode=pl.Buffered(1))
    return pl.pallas_call(
        _out_kernel,
        grid=(n // tm,),
        in_specs=[row(w_attn), row(w_attn), row(d), row(d), row(d),
                  const(wua.shape), const(wub.shape), const(wo.shape), const(n_post.shape)],
        out_specs=row(d),
        out_shape=jax.ShapeDtypeStruct((n, d), F32),
        compiler_params=_params("parallel"),
        name="merge_out",
    )(oa, ob, ga, gb, h, wua, wub, wo, n_post)


def _rope_tables(pos):
    half = ROT_DIM // 2
    inv_freq = jnp.power(ROPE_THETA, -jnp.arange(0, ROT_DIM, 2, dtype=F32) / ROT_DIM)
    ang = pos.astype(F32)[:, None] * inv_freq[None, :]
    cos, sin = jnp.cos(ang), jnp.sin(ang)
    t = pos.shape[0]
    ones = jnp.ones((t, MAP_W - ROT_DIM), F32)
    zeros = jnp.zeros((t, MAP_W - ROT_DIM), F32)
    zh = jnp.zeros((t, half), F32)
    c = jnp.concatenate([cos, cos, ones], axis=1)
    sa = jnp.concatenate([-sin, zh, zeros], axis=1)
    sb = jnp.concatenate([zh, sin, zeros], axis=1)
    return tuple(jnp.concatenate([m, m], axis=1) for m in (c, sa, sb))


def _pad_ffn(w, axis):
    f = w.shape[axis]
    fp = -(-f // FFN_HIDDEN_TILE) * FFN_HIDDEN_TILE
    shape = list(w.shape)
    shape[axis] = fp - f
    return jnp.concatenate([w.astype(BF16), jnp.zeros(shape, BF16)], axis=axis)


def _layer(x, pos, past, lam_init, p):
    b, t, d = x.shape
    n = b * t
    w_a = N_HEADS * HEAD_W
    vec = lambda v: v.reshape(1, -1)
    w_in = p["w_in"].astype(BF16)
    seg = lambda lo, hi: w_in[:, lo:hi]

    h1, xn = _ffn(x.reshape(n, d), vec(p["n1a"]), vec(p["n1b"]), vec(p["nma"]),
                  _pad_ffn(p["f1g"], 1), _pad_ffn(p["f1u"], 1), _pad_ffn(p["f1d"], 0))

    tables = tuple(jnp.tile(m, (b, 1)) for m in _rope_tables(pos))
    q_a = _proj(xn, seg(0, w_a), "q_rope", tables=tables, scale=MAP_W ** -0.5 * LOG2E)
    q_b = _proj(xn, seg(3 * w_a, 4 * w_a), "q", scale=HEAD_W ** -0.5 * LOG2E)
    g_a = _proj(xn, seg(6 * w_a, 6 * w_a + d), "gate", tile=GATE_TOKEN_TILE)
    g_b = _proj(xn, seg(6 * w_a + d, 6 * w_a + 2 * d), "gate", tile=GATE_TOKEN_TILE)

    shape3 = lambda a: a.reshape(b, t, w_a)
    diff_extra = [vec(p["lq1"]), vec(p["lk1"]), vec(p["lq2"]), vec(p["lk2"]), p["subln_g"].reshape(-1, 1)]
    if past is None:
        kat, ka16 = _proj(xn, seg(w_a, 2 * w_a), "k_rope", tables=tables, seq=t)
        k_a_rows = kat.reshape(b, N_HEADS, 2, MAP_W, t).transpose(0, 4, 1, 2, 3)
        kb32, kb16 = _proj(xn, seg(4 * w_a, 5 * w_a), "k")
        va32, vat = _proj(xn, seg(2 * w_a, 3 * w_a), "v_t", seq=t, tk=_tile(t, DIFF_K_TILE))
        vb32, vbt = _proj(xn, seg(5 * w_a, 6 * w_a), "v_t", seq=t, tk=_tile(t, SB_K_TILE))
        o_a = _attention("diff", shape3(q_a), shape3(ka16), vat, 0, diff_extra, lam_init)
        o_b = _attention("sb", shape3(q_b), shape3(kb16), vbt, 0, [], None)
    else:
        ka32 = _proj(xn, seg(w_a, 2 * w_a), "kv_rope", tables=tables)
        kb32 = _proj(xn, seg(4 * w_a, 5 * w_a), "kv")
        va32 = _proj(xn, seg(2 * w_a, 3 * w_a), "kv")
        vb32 = _proj(xn, seg(5 * w_a, 6 * w_a), "kv")
        k_a_rows = ka32.reshape(b, t, N_HEADS, 2, MAP_W)
        pka, pva, pkb, pvb = past
        t_p = -(-t // V7X_LANES) * V7X_LANES
        pad_rows = lambda a: jnp.pad(shape3(a), ((0, 0), (0, t_p - t), (0, 0)))
        by_head = lambda c: c.reshape(c.shape[0], c.shape[1] * N_HEADS, HEAD_W)
        pka_t = pka.transpose(0, 2, 3, 4, 1).reshape(pka.shape[0], N_HEADS, HEAD_W, pka.shape[1])
        o_a = _attention_cached("diff", pad_rows(q_a), pka_t, by_head(pva), pad_rows(ka32),
                                pad_rows(va32), t, diff_extra, lam_init)
        o_b = _attention_cached("sb", pad_rows(q_b), by_head(pkb), by_head(pvb), pad_rows(kb32),
                                pad_rows(vb32), t, [], None)
    o_a = o_a[:, :t].reshape(n, w_a)
    o_b = o_b[:, :t].reshape(n, w_a)

    h2 = _merge_out(o_a, o_b, g_a, g_b, h1, p["w_up_a"].astype(BF16), p["w_up_b"].astype(BF16),
                    p["w_o"].astype(BF16), vec(p["nmb"]))
    y = _ffn(h2, vec(p["n2a"]), vec(p["n2b"]), None,
             _pad_ffn(p["f2g"], 1), _pad_ffn(p["f2u"], 1), _pad_ffn(p["f2d"], 0))
    rows = (k_a_rows, va32.reshape(b, t, N_HEADS, HEAD_W),
            kb32.reshape(b, t, N_HEADS, HEAD_W), vb32.reshape(b, t, N_HEADS, HEAD_W))
    return y.reshape(b, t, d), rows


def _stack(xs):
    return xs[0][None] if len(xs) == 1 else jnp.stack(xs)


def kernel(x_prompt, x_sample, cache_diff_k, cache_diff_v, cache_sb_k, cache_sb_v, w_in, w_up_a, w_up_b, w_o, lam_q1, lam_k1, lam_q2, lam_k2, subln_g, norm_ffn1_pre, norm_ffn1_post, norm_mix_pre, norm_mix_post, norm_ffn2_pre, norm_ffn2_post, ffn1_w_gate, ffn1_w_up, ffn1_w_down, ffn2_w_gate, ffn2_w_up, ffn2_w_down):
    depth = w_in.shape[0]
    pos_p = jnp.arange(x_prompt.shape[1], dtype=jnp.int32)
    pos_s = cache_diff_k.shape[2] + jnp.arange(x_sample.shape[1], dtype=jnp.int32)
    hp, hs = x_prompt, x_sample
    rows_p, rows_s = [], []
    for l in range(depth):
        lam_init = 0.8 - 0.6 * math.exp(-0.3 * l)
        p = dict(w_in=w_in[l], w_up_a=w_up_a[l], w_up_b=w_up_b[l], w_o=w_o[l],
                 lq1=lam_q1[l], lk1=lam_k1[l], lq2=lam_q2[l], lk2=lam_k2[l], subln_g=subln_g[l],
                 n1a=norm_ffn1_pre[l], n1b=norm_ffn1_post[l], nma=norm_mix_pre[l], nmb=norm_mix_post[l],
                 n2a=norm_ffn2_pre[l], n2b=norm_ffn2_post[l],
                 f1g=ffn1_w_gate[l], f1u=ffn1_w_up[l], f1d=ffn1_w_down[l],
                 f2g=ffn2_w_gate[l], f2u=ffn2_w_up[l], f2d=ffn2_w_down[l])
        hp, rp = _layer(hp, pos_p, None, lam_init, p)
        past = (cache_diff_k[l], cache_diff_v[l], cache_sb_k[l], cache_sb_v[l])
        hs, rs = _layer(hs, pos_s, past, lam_init, p)
        rows_p.append(rp)
        rows_s.append(rs)
    outs_p = [_stack([r[i] for r in rows_p]) for i in range(4)]
    outs_s = [_stack([r[i] for r in rows_s]) for i in range(4)]
    return (hp, hs, *outs_p, *outs_s)
```

```python
import functools
import math

import jax
import jax.numpy as jnp
from jax import lax
from jax.experimental import pallas as pl
from jax.experimental.pallas import tpu as pltpu

F32 = jnp.float32
BF16 = jnp.bfloat16

EPS = 1e-6
CHUNK = 64
CHUNK_SHIFT = CHUNK.bit_length() - 1
assert CHUNK == 1 << CHUNK_SHIFT
N_HEADS = 8
HEAD_W = 128
MAP_W = 64
ROT_DIM = 16
ROPE_THETA = 500000.0
NEG = -1e30

V7X_LANES = 128
V7X_MXU_DIM = 256
V7X_VMEM_BYTES = 64 * 1024 * 1024
VMEM_LIMIT_BYTES = V7X_VMEM_BYTES - 8 * 1024 * 1024

FFN_TOKEN_TILE = 512
FFN_HIDDEN_TILE = 512
PROJ_TOKEN_TILE = 1024
GATE_TOKEN_TILE = 512
OUT_TOKEN_TILE = 256
ATTN_Q_TILE = 512
DIFF_K_TILE = 256
SB_K_TILE = V7X_MXU_DIM
DIFF_HEADS_PER_STEP = 4
SB_HEADS_PER_STEP = 8
LOG2E = math.log2(math.e)


def _tile(n, target):
    t = min(n, target)
    while n % t:
        t -= 1
    return t


def _params(*sem):
    return pltpu.CompilerParams(dimension_semantics=sem, vmem_limit_bytes=VMEM_LIMIT_BYTES)


def _rms(x, g):
    return x * lax.rsqrt(jnp.mean(x * x, axis=-1, keepdims=True) + EPS) * g


def _ffn_kernel(n_f, has_next, x_ref, npre_ref, npost_ref, *refs):
    if has_next:
        nnext_ref, wg_ref, wu_ref, wd_ref, h_ref, xn_ref, xs_ref, acc_ref = refs
    else:
        wg_ref, wu_ref, wd_ref, h_ref, xs_ref, acc_ref = refs
    f = pl.program_id(1)

    @pl.when(f == 0)
    def _():
        xs_ref[...] = _rms(x_ref[...], npre_ref[...]).astype(BF16)
        acc_ref[...] = jnp.zeros_like(acc_ref)

    xs = xs_ref[...]
    g = jnp.dot(xs, wg_ref[...], preferred_element_type=F32)
    u = jnp.dot(xs, wu_ref[...], preferred_element_type=F32)
    a = (g * jax.nn.sigmoid(g) * u).astype(BF16)
    acc_ref[...] += jnp.dot(a, wd_ref[...], preferred_element_type=F32)

    @pl.when(f == n_f - 1)
    def _():
        h = x_ref[...] + 0.5 * _rms(acc_ref[...], npost_ref[...])
        h_ref[...] = h
        if has_next:
            xn_ref[...] = _rms(h, nnext_ref[...]).astype(BF16)


def _ffn(x, n_pre, n_post, n_next, wg, wu, wd):
    n, d = x.shape
    fp = wg.shape[1]
    tm = _tile(n, FFN_TOKEN_TILE)
    tf = _tile(fp, FFN_HIDDEN_TILE)
    n_f = fp // tf
    has_next = n_next is not None
    row = pl.BlockSpec((tm, d), lambda i, f: (i, 0))
    vec = pl.BlockSpec((1, d), lambda i, f: (0, 0))
    in_specs = [row, vec, vec] + ([vec] if has_next else []) + [
        pl.BlockSpec((d, tf), lambda i, f: (0, f)),
        pl.BlockSpec((d, tf), lambda i, f: (0, f)),
        pl.BlockSpec((tf, d), lambda i, f: (f, 0)),
    ]
    out_shape = [jax.ShapeDtypeStruct((n, d), F32)]
    out_specs = [row]
    if has_next:
        out_shape.append(jax.ShapeDtypeStruct((n, d), BF16))
        out_specs.append(row)
    args = [x, n_pre, n_post] + ([n_next] if has_next else []) + [wg, wu, wd]
    res = pl.pallas_call(
        functools.partial(_ffn_kernel, n_f, has_next),
        grid=(n // tm, n_f),
        in_specs=in_specs,
        out_specs=out_specs,
        out_shape=out_shape,
        scratch_shapes=[pltpu.VMEM((tm, d), BF16), pltpu.VMEM((tm, d), F32)],
        compiler_params=_params("parallel", "arbitrary"),
        name="ffn_next" if has_next else "ffn",
    )(*args)
    return res if has_next else res[0]


def _rope(r, c, sa, sb):
    heads = []
    for h in range(r.shape[1] // HEAD_W):
        x = r[:, h * HEAD_W:(h + 1) * HEAD_W]
        up = pltpu.roll(x, HEAD_W - ROT_DIM // 2, 1)
        dn = pltpu.roll(x, ROT_DIM // 2, 1)
        heads.append(x * c + up * sa + dn * sb)
    return jnp.concatenate(heads, axis=1)


def _proj_kernel(mode, scale, tk, x_ref, w_ref, *refs):
    r = jnp.dot(x_ref[...], w_ref[...], preferred_element_type=F32)
    if mode in ("q_rope", "k_rope", "kv_rope"):
        c_ref, sa_ref, sb_ref = refs[:3]
        refs = refs[3:]
        r = _rope(r, c_ref[...], sa_ref[...], sb_ref[...])
    if mode in ("q", "q_rope"):
        (q_ref,) = refs
        q_ref[...] = (r * scale).astype(BF16)
    elif mode == "gate":
        (g_ref,) = refs
        g_ref[...] = jax.nn.sigmoid(r)
    elif mode == "k":
        k32_ref, k16_ref = refs
        k32_ref[...] = r
        k16_ref[...] = r.astype(BF16)
    elif mode == "k_rope":
        kt_ref, k16_ref = refs
        k16_ref[...] = r.astype(BF16)
        for h in range(N_HEADS):
            kt_ref[0, h] = r[:, h * HEAD_W:(h + 1) * HEAD_W].T
    elif mode in ("kv", "kv_rope"):
        (r32_ref,) = refs
        r32_ref[...] = r
    else:
        v32_ref, vt_ref = refs
        v32_ref[...] = r
        for h in range(N_HEADS):
            for jb in range(r.shape[0] // tk):
                blk = r[jb * tk:(jb + 1) * tk, h * HEAD_W:(h + 1) * HEAD_W]
                vt_ref[0, h, jb] = blk.T.astype(BF16)


def _proj(xn, w, mode, *, tables=None, scale=1.0, seq=None, tk=None, tile=PROJ_TOKEN_TILE):
    n, d = xn.shape
    width = w.shape[1]
    tm = _tile(n if seq is None else seq, tile)
    row = lambda wd: pl.BlockSpec((tm, wd), lambda i: (i, 0))
    in_specs = [row(d), pl.BlockSpec((d, width), lambda i: (0, 0))]
    args = [xn, w]
    if mode in ("q_rope", "k_rope", "kv_rope"):
        in_specs += [row(HEAD_W)] * 3
        args += list(tables)
    if mode in ("q", "q_rope"):
        out_shape = [jax.ShapeDtypeStruct((n, width), BF16)]
        out_specs = [row(width)]
    elif mode in ("gate", "kv", "kv_rope"):
        out_shape = [jax.ShapeDtypeStruct((n, width), F32)]
        out_specs = [row(width)]
    elif mode == "k":
        out_shape = [jax.ShapeDtypeStruct((n, width), F32), jax.ShapeDtypeStruct((n, width), BF16)]
        out_specs = [row(width), row(width)]
    elif mode == "k_rope":
        assert seq % tm == 0
        per_seq = seq // tm
        out_shape = [jax.ShapeDtypeStruct((n // seq, N_HEADS, HEAD_W, seq), F32),
                     jax.ShapeDtypeStruct((n, width), BF16)]
        out_specs = [pl.BlockSpec((1, N_HEADS, HEAD_W, tm), lambda i: (i // per_seq, 0, 0, i % per_seq)),
                     row(width)]
    else:
        assert mode == "v_t" and tm % tk == 0 and seq % tm == 0
        per_seq = seq // tm
        out_shape = [jax.ShapeDtypeStruct((n, width), F32),
                     jax.ShapeDtypeStruct((n // seq, N_HEADS, seq // tk, HEAD_W, tk), BF16)]
        out_specs = [row(width),
                     pl.BlockSpec((1, N_HEADS, tm // tk, HEAD_W, tk),
                                  lambda i: (i // per_seq, 0, i % per_seq, 0, 0))]
    res = pl.pallas_call(
        functools.partial(_proj_kernel, mode, scale, tk),
        grid=(n // tm,),
        in_specs=in_specs,
        out_specs=out_specs,
        out_shape=out_shape,
        compiler_params=_params("parallel"),
        name="proj_" + mode,
    )(*args)
    return res[0] if len(res) == 1 else res


def _kv_range(kind, i, tq, tk, n_k, q_off):
    qmin = q_off + i * tq
    qmax = qmin + tq - 1
    if kind == "diff":
        n_int = ((qmin // CHUNK + 1) * CHUNK) // tk
        last = ((qmax // CHUNK + 1) * CHUNK - 1) // tk
    else:
        n_int = qmin // tk
        last = (qmax - 1) // tk
    smaller = min if isinstance(i, int) else jnp.minimum
    return smaller(n_int, n_k), smaller(last + 1, n_k), qmin


def _split_maps(q_ref, qm_ref, hb):
    lane = lax.broadcasted_iota(jnp.int32, (1, HEAD_W), 1)
    for h in range(hb):
        q = q_ref[0, :, h * HEAD_W:(h + 1) * HEAD_W]
        zero = jnp.zeros_like(q)
        qm_ref[2 * h] = jnp.where(lane < MAP_W, q, zero)
        qm_ref[2 * h + 1] = jnp.where(lane >= MAP_W, q, zero)


def _diff_block(qm_ref, acc_ref, kbs, vtbs, valid, carry):
    carry, probs, alphas = _diff_softmax(_diff_scores(qm_ref, kbs), valid, carry)
    _diff_values(acc_ref, vtbs, probs, alphas)
    return carry


def _diff_scores(qm_ref, kbs):
    return [lax.dot_general(kbs[c // 2], qm_ref[c], (((1,), (1,)), ((), ())), preferred_element_type=F32)
            for c in range(2 * len(kbs))]


def _diff_softmax(scores, valid, carry):
    new, probs, alphas = [], [], []
    for c, s in enumerate(scores):
        m, l = carry[2 * c], carry[2 * c + 1]
        if valid is not None:
            s = jnp.where(valid, s, NEG)
        m_new = jnp.maximum(m, jnp.max(s, axis=0, keepdims=True))
        alpha = jnp.exp2(m - m_new)
        p = jnp.exp2(s - m_new)
        new += [m_new, alpha * l + jnp.sum(p, axis=0, keepdims=True)]
        probs.append(p.astype(BF16))
        alphas.append(alpha)
    return tuple(new), probs, alphas


def _diff_values(acc_ref, vtbs, probs, alphas):
    for c, (p, alpha) in enumerate(zip(probs, alphas)):
        acc_ref[c] = alpha * acc_ref[c] + jnp.dot(vtbs[c // 2], p, preferred_element_type=F32)


def _diff_init(acc_ref, tq, hb):
    acc_ref[...] = jnp.zeros_like(acc_ref)
    return (jnp.full((1, tq), NEG, F32), jnp.zeros((1, tq), F32)) * (2 * hb)


def _diff_finish(carry, acc_ref, lam_refs, g_ref, lam_init, o_ref, hb):
    lq1_ref, lk1_ref, lq2_ref, lk2_ref = lam_refs
    lam = (jnp.exp(jnp.sum(lq1_ref[...] * lk1_ref[...], keepdims=True))
           - jnp.exp(jnp.sum(lq2_ref[...] * lk2_ref[...], keepdims=True)) + lam_init)
    for h in range(hb):
        l0, l1 = carry[4 * h + 1], carry[4 * h + 3]
        ot = acc_ref[2 * h] * (1.0 / l0) - lam * (acc_ref[2 * h + 1] * (1.0 / l1))
        ms = jnp.mean(ot * ot, axis=0, keepdims=True)
        ot = ot * lax.rsqrt(ms + EPS) * g_ref[...] * (1.0 - lam_init)
        o_ref[0, :, h * HEAD_W:(h + 1) * HEAD_W] = ot.T.astype(BF16)


def _diff_attn_kernel(tq, tk, n_k, q_off, lam_init, hb,
                      q_ref, k_ref, vt_ref, lq1_ref, lk1_ref, lq2_ref, lk2_ref, g_ref,
                      o_ref, qm_ref, acc_ref, s_ref, s_next_ref, p_ref):
    n_int, _, qmin = _kv_range("diff", pl.program_id(2), tq, tk, n_k, q_off)
    _split_maps(q_ref, qm_ref, hb)
    q_chunk = (qmin + lax.broadcasted_iota(jnp.int32, (1, tq), 1)) >> CHUNK_SHIFT
    chains = range(2 * hb)

    def keys(b):
        k0 = pl.multiple_of(jnp.minimum(b, n_k - 1) * tk, tk)
        return [k_ref[0, pl.ds(k0, tk), h * HEAD_W:(h + 1) * HEAD_W] for h in range(hb)]

    def values(b):
        j = jnp.clip(b, 0, n_k - 1)
        return [vt_ref[0, h, j] for h in range(hb)]

    def mask(b):
        return ((b * tk + lax.broadcasted_iota(jnp.int32, (tk, 1), 0)) >> CHUNK_SHIFT) <= q_chunk

    def pair(last, r, state):
        carry, alphas = state
        b0, b1 = 2 * r, 2 * r + 1
        _diff_values(acc_ref, values(b0 - 1), [p_ref[c] for c in chains], alphas)
        s1 = _diff_scores(qm_ref, keys(b1))
        if not last:
            for c, s in enumerate(_diff_scores(qm_ref, keys(b1 + 1))):
                s_next_ref[c] = s
        carry, p0, a0 = _diff_softmax([s_ref[c] for c in chains], mask(b0) if last else None, carry)
        _diff_values(acc_ref, values(b0), p0, a0)
        carry, p1, a1 = _diff_softmax(s1, mask(b1) if last else None, carry)
        if last:
            _diff_values(acc_ref, values(b1), p1, a1)
        else:
            for c in chains:
                s_ref[c] = s_next_ref[c]
                p_ref[c] = p1[c]
        return carry, tuple(a1)

    carry = _diff_init(acc_ref, tq, hb)
    for c, s in enumerate(_diff_scores(qm_ref, keys(0))):
        s_ref[c] = s
    p_ref[...] = jnp.zeros_like(p_ref)
    state = (carry, (jnp.ones((1, tq), F32),) * (2 * hb))
    state = lax.fori_loop(0, n_int // 2, functools.partial(pair, False), state)
    carry, _ = pair(True, n_int // 2, state)
    _diff_finish(carry, acc_ref, (lq1_ref, lk1_ref, lq2_ref, lk2_ref), g_ref, lam_init, o_ref, hb)


def _suffix_matrix(tk):
    s = (lax.broadcasted_iota(jnp.int32, (tk, tk), 1)
         >= lax.broadcasted_iota(jnp.int32, (tk, tk), 0)).astype(BF16)
    return jnp.concatenate([s, s], axis=1)


def _sb_block(q_ref, acc_ref, suffix2, kbs, vtbs, valid, later):
    heads = range(len(kbs))
    ws = [lax.dot_general(kbs[h], q_ref[0, :, h * HEAD_W:(h + 1) * HEAD_W], (((1,), (1,)), ((), ())),
                          preferred_element_type=F32) for h in heads]
    tails = []
    for h in heads:
        sp = jnp.maximum(ws[h], 0.0) + jnp.log(1.0 + jnp.exp2(-jnp.abs(ws[h]))) * LOG2E
        if valid is not None:
            sp = jnp.where(valid, sp, 0.0)
        hi = sp.astype(BF16)
        lo = (sp - hi.astype(F32)).astype(BF16)
        tails.append(jnp.dot(suffix2, jnp.concatenate([hi, lo], axis=0),
                             preferred_element_type=F32))
    weights = []
    for h in heads:
        logit = ws[h] - tails[h] - later[h]
        if valid is not None:
            logit = jnp.where(valid, logit, NEG)
        weights.append(jnp.exp2(logit).astype(BF16))
    for h in heads:
        acc_ref[h] += jnp.dot(vtbs[h], weights[h], preferred_element_type=F32)
    return tuple(later[h] + tails[h][0:1, :] for h in heads)


def _sb_finish(acc_ref, o_ref, hb):
    for h in range(hb):
        o_ref[0, :, h * HEAD_W:(h + 1) * HEAD_W] = acc_ref[h].T.astype(BF16)


def _sb_attn_kernel(tq, tk, n_k, q_off, hb, q_ref, k_ref, vt_ref, o_ref, acc_ref):
    n_int, n_tot, qmin = _kv_range("sb", pl.program_id(2), tq, tk, n_k, q_off)
    q_pos = qmin + lax.broadcasted_iota(jnp.int32, (1, tq), 1)
    suffix2 = _suffix_matrix(tk)
    acc_ref[...] = jnp.zeros_like(acc_ref)

    def step(masked, t, later):
        j = n_tot - 1 - t
        k0 = pl.multiple_of(j * tk, tk)
        valid = (k0 + lax.broadcasted_iota(jnp.int32, (tk, 1), 0)) < q_pos if masked else None
        kbs = [k_ref[0, pl.ds(k0, tk), h * HEAD_W:(h + 1) * HEAD_W] for h in range(hb)]
        vtbs = [vt_ref[0, h, j] for h in range(hb)]
        return _sb_block(q_ref, acc_ref, suffix2, kbs, vtbs, valid, later)

    later = (jnp.zeros((1, tq), F32),) * hb
    later = lax.fori_loop(0, n_tot - n_int, functools.partial(step, True), later)
    lax.fori_loop(n_tot - n_int, n_tot, functools.partial(step, False), later)
    _sb_finish(acc_ref, o_ref, hb)


def _attention(kind, q, k, vt, q_off, extra, cfg):
    b, t_q, width = q.shape
    n_k, tk = vt.shape[2], vt.shape[4]
    t_k = k.shape[1]
    assert t_k == n_k * tk
    tq = _tile(t_q, ATTN_Q_TILE)
    hb = DIFF_HEADS_PER_STEP if kind == "diff" else SB_HEADS_PER_STEP
    kv_mode = dict(pipeline_mode=pl.Buffered(1))
    q_spec = pl.BlockSpec((1, tq, hb * HEAD_W), lambda bi, h, i: (bi, i, h))
    in_specs = [q_spec,
                pl.BlockSpec((1, t_k, hb * HEAD_W), lambda bi, h, i: (bi, 0, h), **kv_mode),
                pl.BlockSpec((1, hb, n_k, HEAD_W, tk), lambda bi, h, i: (bi, h, 0, 0, 0), **kv_mode)]
    for e in extra:
        in_specs.append(pl.BlockSpec(e.shape, lambda bi, h, i: (0, 0)))
    if kind == "diff":
        for i in range(t_q // tq):
            n_int, n_tot, _ = _kv_range("diff", i, tq, tk, n_k, q_off)
            assert -(-int(n_tot) // 2) - int(n_int) // 2 == 1
        body = functools.partial(_diff_attn_kernel, tq, tk, n_k, q_off, cfg, hb)
        scratch = [pltpu.VMEM((2 * hb, tq, HEAD_W), BF16), pltpu.VMEM((2 * hb, HEAD_W, tq), F32),
                   pltpu.VMEM((2 * hb, tk, tq), F32), pltpu.VMEM((2 * hb, tk, tq), F32),
                   pltpu.VMEM((2 * hb, tk, tq), BF16)]
    else:
        body = functools.partial(_sb_attn_kernel, tq, tk, n_k, q_off, hb)
        scratch = [pltpu.VMEM((hb, HEAD_W, tq), F32)]
    return pl.pallas_call(
        body,
        grid=(b, N_HEADS // hb, t_q // tq),
        in_specs=in_specs,
        out_specs=q_spec,
        out_shape=jax.ShapeDtypeStruct((b, t_q, width), BF16),
        scratch_shapes=scratch,
        compiler_params=_params("parallel", "parallel", "arbitrary"),
        name=kind + "_attn",
    )(q, k, vt, *extra)


def _past_operands(kp_ref, vp_ref, k0, tk):
    heads = range(N_HEADS)
    rows = lambda h: pl.ds(k0 * N_HEADS + h, tk, stride=N_HEADS)
    kbs = [kp_ref[0, rows(h), :].astype(BF16) for h in heads]
    vtbs = [vp_ref[0, rows(h), :].T.astype(BF16) for h in heads]
    return kbs, vtbs


def _new_operands(kn_ref, vn_ref):
    heads = range(N_HEADS)
    kbs = [kn_ref[0, :, h * HEAD_W:(h + 1) * HEAD_W].astype(BF16) for h in heads]
    vtbs = [vn_ref[0, :, h * HEAD_W:(h + 1) * HEAD_W].T.astype(BF16) for h in heads]
    return kbs, vtbs


def _new_positions(q_off, t_new, tq):
    k_idx = lax.broadcasted_iota(jnp.int32, (t_new, 1), 0)
    q_idx = lax.broadcasted_iota(jnp.int32, (1, tq), 1)
    return k_idx, q_off + k_idx, q_off + q_idx


def _diff_cached_kernel(tq, tk, n_past, q_off, t_real, lam_init,
                        q_ref, kp_ref, vp_ref, kn_ref, vn_ref, lq1_ref, lk1_ref, lq2_ref, lk2_ref, g_ref,
                        o_ref, qm_ref, acc_ref):
    _split_maps(q_ref, qm_ref, N_HEADS)
    carry = _diff_init(acc_ref, tq, N_HEADS)
    heads = range(N_HEADS)
    for j in range(n_past):
        kbs = [kp_ref[0, h, :, j * tk:(j + 1) * tk].T.astype(BF16) for h in heads]
        vtbs = [vp_ref[0, pl.ds(j * tk * N_HEADS + h, tk, stride=N_HEADS), :].T.astype(BF16) for h in heads]
        carry = _diff_block(qm_ref, acc_ref, kbs, vtbs, None, carry)
    k_idx, k_pos, q_pos = _new_positions(q_off, kn_ref.shape[1], tq)
    valid = ((k_pos >> CHUNK_SHIFT) <= (q_pos >> CHUNK_SHIFT)) & (k_idx < t_real)
    kbs, vtbs = _new_operands(kn_ref, vn_ref)
    carry = _diff_block(qm_ref, acc_ref, kbs, vtbs, valid, carry)
    _diff_finish(carry, acc_ref, (lq1_ref, lk1_ref, lq2_ref, lk2_ref), g_ref, lam_init, o_ref, N_HEADS)


def _sb_cached_kernel(tq, tk, n_past, q_off, t_real, q_ref, kp_ref, vp_ref, kn_ref, vn_ref, o_ref, acc_ref):
    acc_ref[...] = jnp.zeros_like(acc_ref)
    t_new = kn_ref.shape[1]
    k_idx, k_pos, q_pos = _new_positions(q_off, t_new, tq)
    valid = (k_pos < q_pos) & (k_idx < t_real)
    kbs, vtbs = _new_operands(kn_ref, vn_ref)
    later = (jnp.zeros((1, tq), F32),) * N_HEADS
    later = _sb_block(q_ref, acc_ref, _suffix_matrix(t_new), kbs, vtbs, valid, later)
    suffix2 = _suffix_matrix(tk)

    def past(t, later):
        kbs, vtbs = _past_operands(kp_ref, vp_ref, pl.multiple_of((n_past - 1 - t) * tk, tk), tk)
        return _sb_block(q_ref, acc_ref, suffix2, kbs, vtbs, None, later)

    lax.fori_loop(0, n_past, past, later)
    _sb_finish(acc_ref, o_ref, N_HEADS)


def _attention_cached(kind, q, k_past, v_past, k_new, v_new, t_real, extra, lam_init):
    b, tq, width = q.shape
    p_len, t_new = v_past.shape[1] // N_HEADS, k_new.shape[1]
    tk = _tile(p_len, SB_K_TILE)
    n_past = p_len // tk
    row3 = lambda t: pl.BlockSpec((1, t, width), lambda bi: (bi, 0, 0))
    past_spec = pl.BlockSpec((1, p_len * N_HEADS, HEAD_W), lambda bi: (bi, 0, 0))
    k_spec = pl.BlockSpec((1, N_HEADS, HEAD_W, p_len), lambda bi: (bi, 0, 0, 0)) if kind == "diff" else past_spec
    in_specs = [row3(tq), k_spec, past_spec, row3(t_new), row3(t_new)]
    for e in extra:
        in_specs.append(pl.BlockSpec(e.shape, lambda bi: (0, 0)))
    if kind == "diff":
        body = functools.partial(_diff_cached_kernel, tq, tk, n_past, p_len, t_real, lam_init)
        scratch = [pltpu.VMEM((2 * N_HEADS, tq, HEAD_W), BF16), pltpu.VMEM((2 * N_HEADS, HEAD_W, tq), F32)]
    else:
        body = functools.partial(_sb_cached_kernel, tq, tk, n_past, p_len, t_real)
        scratch = [pltpu.VMEM((N_HEADS, HEAD_W, tq), F32)]
    return pl.pallas_call(
        body,
        grid=(b,),
        in_specs=in_specs,
        out_specs=row3(tq),
        out_shape=jax.ShapeDtypeStruct((b, tq, width), BF16),
        scratch_shapes=scratch,
        compiler_params=_params("parallel"),
        name=kind + "_attn_cached",
    )(q, k_past, v_past, k_new, v_new, *extra)


def _out_kernel(oa_ref, ob_ref, ga_ref, gb_ref, h_ref, wua_ref, wub_ref, wo_ref, n_ref, o_ref):
    pa = jnp.dot(oa_ref[...], wua_ref[...], preferred_element_type=F32)
    pb = jnp.dot(ob_ref[...], wub_ref[...], preferred_element_type=F32)
    merged = (ga_ref[...] * pa + gb_ref[...] * pb).astype(BF16)
    r = jnp.dot(merged, wo_ref[...], preferred_element_type=F32)
    o_ref[...] = h_ref[...] + _rms(r, n_ref[...])


def _merge_out(oa, ob, ga, gb, h, wua, wub, wo, n_post):
    n, d = h.shape
    w_attn = oa.shape[1]
    tm = _tile(n, OUT_TOKEN_TILE)
    row = lambda wd: pl.BlockSpec((tm, wd), lambda i: (i, 0))
    const = lambda shape: pl.BlockSpec(shape, lambda i: (0, 0), pipeline_mode=pl.Buffered(1))
    return pl.pallas_call(
        _out_kernel,
        grid=(n // tm,),
        in_specs=[row(w_attn), row(w_attn), row(d), row(d), row(d),
                  const(wua.shape), const(wub.shape), const(wo.shape), const(n_post.shape)],
        out_specs=row(d),
        out_shape=jax.ShapeDtypeStruct((n, d), F32),
        compiler_params=_params("parallel"),
        name="merge_out",
    )(oa, ob, ga, gb, h, wua, wub, wo, n_post)


def _rope_tables(pos):
    half = ROT_DIM // 2
    inv_freq = jnp.power(ROPE_THETA, -jnp.arange(0, ROT_DIM, 2, dtype=F32) / ROT_DIM)
    ang = pos.astype(F32)[:, None] * inv_freq[None, :]
    cos, sin = jnp.cos(ang), jnp.sin(ang)
    t = pos.shape[0]
    ones = jnp.ones((t, MAP_W - ROT_DIM), F32)
    zeros = jnp.zeros((t, MAP_W - ROT_DIM), F32)
    zh = jnp.zeros((t, half), F32)
    c = jnp.concatenate([cos, cos, ones], axis=1)
    sa = jnp.concatenate([-sin, zh, zeros], axis=1)
    sb = jnp.concatenate([zh, sin, zeros], axis=1)
    return tuple(jnp.concatenate([m, m], axis=1) for m in (c, sa, sb))


def _pad_ffn(w, axis):
    f = w.shape[axis]
    fp = -(-f // FFN_HIDDEN_TILE) * FFN_HIDDEN_TILE
    shape = list(w.shape)
    shape[axis] = fp - f
    return jnp.concatenate([w.astype(BF16), jnp.zeros(shape, BF16)], axis=axis)


def _layer(x, pos, past, lam_init, p):
    b, t, d = x.shape
    n = b * t
    w_a = N_HEADS * HEAD_W
    vec = lambda v: v.reshape(1, -1)
    w_in = p["w_in"].astype(BF16)
    seg = lambda lo, hi: w_in[:, lo:hi]

    h1, xn = _ffn(x.reshape(n, d), vec(p["n1a"]), vec(p["n1b"]), vec(p["nma"]),
                  _pad_ffn(p["f1g"], 1), _pad_ffn(p["f1u"], 1), _pad_ffn(p["f1d"], 0))

    tables = tuple(jnp.tile(m, (b, 1)) for m in _rope_tables(pos))
    q_a = _proj(xn, seg(0, w_a), "q_rope", tables=tables, scale=MAP_W ** -0.5 * LOG2E)
    q_b = _proj(xn, seg(3 * w_a, 4 * w_a), "q", scale=HEAD_W ** -0.5 * LOG2E)
    g_a = _proj(xn, seg(6 * w_a, 6 * w_a + d), "gate", tile=GATE_TOKEN_TILE)
    g_b = _proj(xn, seg(6 * w_a + d, 6 * w_a + 2 * d), "gate", tile=GATE_TOKEN_TILE)

    shape3 = lambda a: a.reshape(b, t, w_a)
    diff_extra = [vec(p["lq1"]), vec(p["lk1"]), vec(p["lq2"]), vec(p["lk2"]), p["subln_g"].reshape(-1, 1)]
    if past is None:
        kat, ka16 = _proj(xn, seg(w_a, 2 * w_a), "k_rope", tables=tables, seq=t)
        k_a_rows = kat.reshape(b, N_HEADS, 2, MAP_W, t).transpose(0, 4, 1, 2, 3)
        kb32, kb16 = _proj(xn, seg(4 * w_a, 5 * w_a), "k")
        va32, vat = _proj(xn, seg(2 * w_a, 3 * w_a), "v_t", seq=t, tk=_tile(t, DIFF_K_TILE))
        vb32, vbt = _proj(xn, seg(5 * w_a, 6 * w_a), "v_t", seq=t, tk=_tile(t, SB_K_TILE))
        o_a = _attention("diff", shape3(q_a), shape3(ka16), vat, 0, diff_extra, lam_init)
        o_b = _attention("sb", shape3(q_b), shape3(kb16), vbt, 0, [], None)
    else:
        ka32 = _proj(xn, seg(w_a, 2 * w_a), "kv_rope", tables=tables)
        kb32 = _proj(xn, seg(4 * w_a, 5 * w_a), "kv")
        va32 = _proj(xn, seg(2 * w_a, 3 * w_a), "kv")
        vb32 = _proj(xn, seg(5 * w_a, 6 * w_a), "kv")
        k_a_rows = ka32.reshape(b, t, N_HEADS, 2, MAP_W)
        pka, pva, pkb, pvb = past
        t_p = -(-t // V7X_LANES) * V7X_LANES
        pad_rows = lambda a: jnp.pad(shape3(a), ((0, 0), (0, t_p - t), (0, 0)))
        by_head = lambda c: c.reshape(c.shape[0], c.shape[1] * N_HEADS, HEAD_W)
        pka_t = pka.transpose(0, 2, 3, 4, 1).reshape(pka.shape[0], N_HEADS, HEAD_W, pka.shape[1])
        o_a = _attention_cached("diff", pad_rows(q_a), pka_t, by_head(pva), pad_rows(ka32),
                                pad_rows(va32), t, diff_extra, lam_init)
        o_b = _attention_cached("sb", pad_rows(q_b), by_head(pkb), by_head(pvb), pad_rows(kb32),
                                pad_rows(vb32), t, [], None)
    o_a = o_a[:, :t].reshape(n, w_a)
    o_b = o_b[:, :t].reshape(n, w_a)

    h2 = _merge_out(o_a, o_b, g_a, g_b, h1, p["w_up_a"].astype(BF16), p["w_up_b"].astype(BF16),
                    p["w_o"].astype(BF16), vec(p["nmb"]))
    y = _ffn(h2, vec(p["n2a"]), vec(p["n2b"]), None,
             _pad_ffn(p["f2g"], 1), _pad_ffn(p["f2u"], 1), _pad_ffn(p["f2d"], 0))
    rows = (k_a_rows, va32.reshape(b, t, N_HEADS, HEAD_W),
            kb32.reshape(b, t, N_HEADS, HEAD_W), vb32.reshape(b, t, N_HEADS, HEAD_W))
    return y.reshape(b, t, d), rows


def _stack(xs):
    return xs[0][None] if len(xs) == 1 else jnp.stack(xs)


def kernel(x_prompt, x_sample, cache_diff_k, cache_diff_v, cache_sb_k, cache_sb_v, w_in, w_up_a, w_up_b, w_o, lam_q1, lam_k1, lam_q2, lam_k2, subln_g, norm_ffn1_pre, norm_ffn1_post, norm_mix_pre, norm_mix_post, norm_ffn2_pre, norm_ffn2_post, ffn1_w_gate, ffn1_w_up, ffn1_w_down, ffn2_w_gate, ffn2_w_up, ffn2_w_down):
    depth = w_in.shape[0]
    pos_p = jnp.arange(x_prompt.shape[1], dtype=jnp.int32)
    pos_s = cache_diff_k.shape[2] + jnp.arange(x_sample.shape[1], dtype=jnp.int32)
    hp, hs = x_prompt, x_sample
    rows_p, rows_s = [], []
    for l in range(depth):
        lam_init = 0.8 - 0.6 * math.exp(-0.3 * l)
        p = dict(w_in=w_in[l], w_up_a=w_up_a[l], w_up_b=w_up_b[l], w_o=w_o[l],
                 lq1=lam_q1[l], lk1=lam_k1[l], lq2=lam_q2[l], lk2=lam_k2[l], subln_g=subln_g[l],
                 n1a=norm_ffn1_pre[l], n1b=norm_ffn1_post[l], nma=norm_mix_pre[l], nmb=norm_mix_post[l],
                 n2a=norm_ffn2_pre[l], n2b=norm_ffn2_post[l],
                 f1g=ffn1_w_gate[l], f1u=ffn1_w_up[l], f1d=ffn1_w_down[l],
                 f2g=ffn2_w_gate[l], f2u=ffn2_w_up[l], f2d=ffn2_w_down[l])
        hp, rp = _layer(hp, pos_p, None, lam_init, p)
        past = (cache_diff_k[l], cache_diff_v[l], cache_sb_k[l], cache_sb_v[l])
        hs, rs = _layer(hs, pos_s, past, lam_init, p)
        rows_p.append(rp)
        rows_s.append(rs)
    outs_p = [_stack([r[i] for r in rows_p]) for i in range(4)]
    outs_s = [_stack([r[i] for r in rows_s]) for i in range(4)]
    return (hp, hs, *outs_p, *outs_s)
```

```python
import functools
import math

import jax
import jax.numpy as jnp
from jax import lax
from jax.experimental import pallas as pl
from jax.experimental.pallas import tpu as pltpu

F32 = jnp.float32
BF16 = jnp.bfloat16

EPS = 1e-6
CHUNK = 64
CHUNK_SHIFT = CHUNK.bit_length() - 1
assert CHUNK == 1 << CHUNK_SHIFT
N_HEADS = 8
HEAD_W = 128
MAP_W = 64
ROT_DIM = 16
ROPE_THETA = 500000.0
NEG = -1e30

V7X_LANES = 128
V7X_MXU_DIM = 256
V7X_VMEM_BYTES = 64 * 1024 * 1024
VMEM_LIMIT_BYTES = V7X_VMEM_BYTES - 8 * 1024 * 1024

FFN_TOKEN_TILE = 512
FFN_HIDDEN_TILE = 512
PROJ_TOKEN_TILE = 1024
PAIR_TOKEN_TILE = 512
GATE_TOKEN_TILE = 512
OUT_TOKEN_TILE = 256
ATTN_Q_TILE = 512
DIFF_K_TILE = 512
SB_K_TILE = V7X_MXU_DIM
DIFF_HEADS_PER_STEP = 4
SB_HEADS_PER_STEP = 8
LOG2E = math.log2(math.e)


def _tile(n, target):
    t = min(n, target)
    while n % t:
        t -= 1
    return t


def _params(*sem):
    return pltpu.CompilerParams(dimension_semantics=sem, vmem_limit_bytes=VMEM_LIMIT_BYTES)


def _rms(x, g):
    return x * lax.rsqrt(jnp.mean(x * x, axis=-1, keepdims=True) + EPS) * g


def _ffn_kernel(n_f, has_next, x_ref, npre_ref, npost_ref, *refs):
    if has_next:
        nnext_ref, wg_ref, wu_ref, wd_ref, h_ref, xn_ref, xs_ref, acc_ref = refs
    else:
        wg_ref, wu_ref, wd_ref, h_ref, xs_ref, acc_ref = refs
    f = pl.program_id(1)

    @pl.when(f == 0)
    def _():
        xs_ref[...] = _rms(x_ref[...], npre_ref[...]).astype(BF16)
        acc_ref[...] = jnp.zeros_like(acc_ref)

    xs = xs_ref[...]
    g = jnp.dot(xs, wg_ref[...], preferred_element_type=F32)
    u = jnp.dot(xs, wu_ref[...], preferred_element_type=F32)
    a = (g * jax.nn.sigmoid(g) * u).astype(BF16)
    acc_ref[...] += jnp.dot(a, wd_ref[...], preferred_element_type=F32)

    @pl.when(f == n_f - 1)
    def _():
        h = x_ref[...] + 0.5 * _rms(acc_ref[...], npost_ref[...])
        h_ref[...] = h
        if has_next:
            xn_ref[...] = _rms(h, nnext_ref[...]).astype(BF16)


def _ffn(x, n_pre, n_post, n_next, wg, wu, wd):
    n, d = x.shape
    fp = wg.shape[1]
    tm = _tile(n, FFN_TOKEN_TILE)
    tf = _tile(fp, FFN_HIDDEN_TILE)
    n_f = fp // tf
    has_next = n_next is not None
    row = pl.BlockSpec((tm, d), lambda i, f: (i, 0))
    vec = pl.BlockSpec((1, d), lambda i, f: (0, 0))
    in_specs = [row, vec, vec] + ([vec] if has_next else []) + [
        pl.BlockSpec((d, tf), lambda i, f: (0, f)),
        pl.BlockSpec((d, tf), lambda i, f: (0, f)),
        pl.BlockSpec((tf, d), lambda i, f: (f, 0)),
    ]
    out_shape = [jax.ShapeDtypeStruct((n, d), F32)]
    out_specs = [row]
    if has_next:
        out_shape.append(jax.ShapeDtypeStruct((n, d), BF16))
        out_specs.append(row)
    args = [x, n_pre, n_post] + ([n_next] if has_next else []) + [wg, wu, wd]
    res = pl.pallas_call(
        functools.partial(_ffn_kernel, n_f, has_next),
        grid=(n // tm, n_f),
        in_specs=in_specs,
        out_specs=out_specs,
        out_shape=out_shape,
        scratch_shapes=[pltpu.VMEM((tm, d), BF16), pltpu.VMEM((tm, d), F32)],
        compiler_params=_params("parallel", "arbitrary"),
        name="ffn_next" if has_next else "ffn",
    )(*args)
    return res if has_next else res[0]


def _rope(r, c, sa, sb):
    heads = []
    for h in range(r.shape[1] // HEAD_W):
        x = r[:, h * HEAD_W:(h + 1) * HEAD_W]
        up = pltpu.roll(x, HEAD_W - ROT_DIM // 2, 1)
        dn = pltpu.roll(x, ROT_DIM // 2, 1)
        heads.append(x * c + up * sa + dn * sb)
    return jnp.concatenate(heads, axis=1)


PROJ_OUTPUTS = {"q": 1, "q_rope": 1, "gate": 1, "k": 2, "k_rope": 2, "kv": 1, "kv_rope": 1, "v_t": 2}
PROJ_ROTARY = ("q_rope", "k_rope", "kv_rope")


def _proj_epilogue(mode, scale, tk, r, tables, refs):
    if mode in PROJ_ROTARY:
        r = _rope(r, *tables)
    if mode in ("q", "q_rope"):
        (q_ref,) = refs
        q_ref[...] = (r * scale).astype(BF16)
    elif mode == "gate":
        (g_ref,) = refs
        g_ref[...] = jax.nn.sigmoid(r)
    elif mode == "k":
        k32_ref, k16_ref = refs
        k32_ref[...] = r
        k16_ref[...] = r.astype(BF16)
    elif mode == "k_rope":
        kt_ref, k16_ref = refs
        k16_ref[...] = r.astype(BF16)
        for h in range(N_HEADS):
            kt_ref[0, h] = r[:, h * HEAD_W:(h + 1) * HEAD_W].T
    elif mode in ("kv", "kv_rope"):
        (r32_ref,) = refs
        r32_ref[...] = r
    else:
        v32_ref, vt_ref = refs
        v32_ref[...] = r
        for h in range(N_HEADS):
            for jb in range(r.shape[0] // tk):
                blk = r[jb * tk:(jb + 1) * tk, h * HEAD_W:(h + 1) * HEAD_W]
                vt_ref[0, h, jb] = blk.T.astype(BF16)


def _proj_kernel(parts, has_tables, x_ref, *refs):
    w_refs, refs = refs[:len(parts)], refs[len(parts):]
    tables = None
    if has_tables:
        tables, refs = tuple(t[...] for t in refs[:3]), refs[3:]
    x = x_ref[...]
    for (mode, scale, tk), w_ref in zip(parts, w_refs):
        outs, refs = refs[:PROJ_OUTPUTS[mode]], refs[PROJ_OUTPUTS[mode]:]
        _proj_epilogue(mode, scale, tk, jnp.dot(x, w_ref[...], preferred_element_type=F32), tables, outs)


def _proj(xn, parts, *, tables=None, seq=None, tile=PROJ_TOKEN_TILE):
    n, d = xn.shape
    tm = _tile(n if seq is None else seq, tile)
    row = lambda wd: pl.BlockSpec((tm, wd), lambda i: (i, 0))
    has_tables = any(mode in PROJ_ROTARY for _, mode, _, _ in parts)
    in_specs = [row(d)] + [pl.BlockSpec(w.shape, lambda i: (0, 0)) for w, _, _, _ in parts]
    args = [xn] + [w for w, _, _, _ in parts]
    if has_tables:
        in_specs += [row(HEAD_W)] * 3
        args += list(tables)
    out_shape, out_specs = [], []
    for w, mode, _, tk in parts:
        width = w.shape[1]
        if mode in ("q", "q_rope"):
            out_shape += [jax.ShapeDtypeStruct((n, width), BF16)]
            out_specs += [row(width)]
        elif mode in ("gate", "kv", "kv_rope"):
            out_shape += [jax.ShapeDtypeStruct((n, width), F32)]
            out_specs += [row(width)]
        elif mode == "k":
            out_shape += [jax.ShapeDtypeStruct((n, width), F32), jax.ShapeDtypeStruct((n, width), BF16)]
            out_specs += [row(width), row(width)]
        elif mode == "k_rope":
            assert seq % tm == 0
            per_seq = seq // tm
            out_shape += [jax.ShapeDtypeStruct((n // seq, N_HEADS, HEAD_W, seq), F32),
                          jax.ShapeDtypeStruct((n, width), BF16)]
            out_specs += [pl.BlockSpec((1, N_HEADS, HEAD_W, tm),
                                       lambda i, per_seq=per_seq: (i // per_seq, 0, 0, i % per_seq)),
                          row(width)]
        else:
            assert mode == "v_t" and tm % tk == 0 and seq % tm == 0
            per_seq = seq // tm
            out_shape += [jax.ShapeDtypeStruct((n, width), F32),
                          jax.ShapeDtypeStruct((n // seq, N_HEADS, seq // tk, HEAD_W, tk), BF16)]
            out_specs += [row(width),
                          pl.BlockSpec((1, N_HEADS, tm // tk, HEAD_W, tk),
                                       lambda i, per_seq=per_seq: (i // per_seq, 0, i % per_seq, 0, 0))]
    res = pl.pallas_call(
        functools.partial(_proj_kernel, tuple((mode, scale, tk) for _, mode, scale, tk in parts), has_tables),
        grid=(n // tm,),
        in_specs=in_specs,
        out_specs=out_specs,
        out_shape=out_shape,
        compiler_params=_params("parallel"),
        name="proj_" + "_".join(mode for _, mode, _, _ in parts),
    )(*args)
    out, k = [], 0
    for _, mode, _, _ in parts:
        out.append(tuple(res[k:k + PROJ_OUTPUTS[mode]]))
        k += PROJ_OUTPUTS[mode]
    return out


def _kv_range(kind, i, tq, tk, n_k, q_off):
    qmin = q_off + i * tq
    qmax = qmin + tq - 1
    if kind == "diff":
        n_int = ((qmin // CHUNK + 1) * CHUNK) // tk
        last = ((qmax // CHUNK + 1) * CHUNK - 1) // tk
    else:
        n_int = qmin // tk
        last = (qmax - 1) // tk
    smaller = min if isinstance(i, int) else jnp.minimum
    return smaller(n_int, n_k), smaller(last + 1, n_k), qmin


def _split_maps(q_ref, qm_ref, hb):
    lane = lax.broadcasted_iota(jnp.int32, (1, HEAD_W), 1)
    for h in range(hb):
        q = q_ref[0, :, h * HEAD_W:(h + 1) * HEAD_W]
        zero = jnp.zeros_like(q)
        qm_ref[2 * h] = jnp.where(lane < MAP_W, q, zero)
        qm_ref[2 * h + 1] = jnp.where(lane >= MAP_W, q, zero)


def _diff_block(qm_ref, acc_ref, kbs, vtbs, valid, carry):
    carry, probs, alphas = _diff_softmax(_diff_scores(qm_ref, kbs), valid, carry)
    _diff_values(acc_ref, vtbs, probs, alphas)
    return carry


def _diff_scores(qm_ref, kbs):
    return [lax.dot_general(kbs[c // 2], qm_ref[c], (((1,), (1,)), ((), ())), preferred_element_type=F32)
            for c in range(2 * len(kbs))]


def _diff_softmax(scores, valid, carry):
    new, probs, alphas = [], [], []
    for c, s in enumerate(scores):
        m, l = carry[2 * c], carry[2 * c + 1]
        if valid is not None:
            s = jnp.where(valid, s, NEG)
        m_new = jnp.maximum(m, jnp.max(s, axis=0, keepdims=True))
        alpha = jnp.exp2(m - m_new)
        p = jnp.exp2(s - m_new)
        new += [m_new, alpha * l + jnp.sum(p, axis=0, keepdims=True)]
        probs.append(p.astype(BF16))
        alphas.append(alpha)
    return tuple(new), probs, alphas


def _diff_values(acc_ref, vtbs, probs, alphas):
    for c, (p, alpha) in enumerate(zip(probs, alphas)):
        acc_ref[c] = alpha * acc_ref[c] + jnp.dot(vtbs[c // 2], p, preferred_element_type=F32)


def _diff_init(acc_ref, tq, hb):
    acc_ref[...] = jnp.zeros_like(acc_ref)
    return (jnp.full((1, tq), NEG, F32), jnp.zeros((1, tq), F32)) * (2 * hb)


def _diff_finish(carry, acc_ref, lam_refs, g_ref, lam_init, o_ref, hb):
    lq1_ref, lk1_ref, lq2_ref, lk2_ref = lam_refs
    lam = (jnp.exp(jnp.sum(lq1_ref[...] * lk1_ref[...], keepdims=True))
           - jnp.exp(jnp.sum(lq2_ref[...] * lk2_ref[...], keepdims=True)) + lam_init)
    for h in range(hb):
        l0, l1 = carry[4 * h + 1], carry[4 * h + 3]
        ot = acc_ref[2 * h] * (1.0 / l0) - lam * (acc_ref[2 * h + 1] * (1.0 / l1))
        ms = jnp.mean(ot * ot, axis=0, keepdims=True)
        ot = ot * lax.rsqrt(ms + EPS) * g_ref[...] * (1.0 - lam_init)
        o_ref[0, :, h * HEAD_W:(h + 1) * HEAD_W] = ot.T.astype(BF16)


def _diff_attn_kernel(tq, tk, n_k, q_off, lam_init, hb,
                      q_ref, k_ref, vt_ref, lq1_ref, lk1_ref, lq2_ref, lk2_ref, g_ref,
                      o_ref, qm_ref, acc_ref):
    n_int, n_tot, qmin = _kv_range("diff", pl.program_id(2), tq, tk, n_k, q_off)
    _split_maps(q_ref, qm_ref, hb)
    q_chunk = (qmin + lax.broadcasted_iota(jnp.int32, (1, tq), 1)) >> CHUNK_SHIFT

    def step(masked, j, carry):
        k0 = pl.multiple_of(j * tk, tk)
        valid = None
        if masked:
            k_chunk = (k0 + lax.broadcasted_iota(jnp.int32, (tk, 1), 0)) >> CHUNK_SHIFT
            valid = k_chunk <= q_chunk
        kbs = [k_ref[0, pl.ds(k0, tk), h * HEAD_W:(h + 1) * HEAD_W] for h in range(hb)]
        vtbs = [vt_ref[0, h, j] for h in range(hb)]
        return _diff_block(qm_ref, acc_ref, kbs, vtbs, valid, carry)

    carry = _diff_init(acc_ref, tq, hb)
    carry = lax.fori_loop(0, n_int, functools.partial(step, False), carry)
    carry = lax.fori_loop(n_int, n_tot, functools.partial(step, True), carry)
    _diff_finish(carry, acc_ref, (lq1_ref, lk1_ref, lq2_ref, lk2_ref), g_ref, lam_init, o_ref, hb)


def _suffix_matrix(tk):
    s = (lax.broadcasted_iota(jnp.int32, (tk, tk), 1)
         >= lax.broadcasted_iota(jnp.int32, (tk, tk), 0)).astype(BF16)
    return jnp.concatenate([s, s], axis=1)


def _sb_block(q_ref, acc_ref, suffix2, kbs, vtbs, valid, later):
    heads = range(len(kbs))
    ws = [lax.dot_general(kbs[h], q_ref[0, :, h * HEAD_W:(h + 1) * HEAD_W], (((1,), (1,)), ((), ())),
                          preferred_element_type=F32) for h in heads]
    tails = []
    for h in heads:
        sp = jnp.maximum(ws[h], 0.0) + jnp.log(1.0 + jnp.exp2(-jnp.abs(ws[h]))) * LOG2E
        if valid is not None:
            sp = jnp.where(valid, sp, 0.0)
        hi = sp.astype(BF16)
        lo = (sp - hi.astype(F32)).astype(BF16)
        tails.append(jnp.dot(suffix2, jnp.concatenate([hi, lo], axis=0),
                             preferred_element_type=F32))
    weights = []
    for h in heads:
        logit = ws[h] - tails[h] - later[h]
        if valid is not None:
            logit = jnp.where(valid, logit, NEG)
        weights.append(jnp.exp2(logit).astype(BF16))
    for h in heads:
        acc_ref[h] += jnp.dot(vtbs[h], weights[h], preferred_element_type=F32)
    return tuple(later[h] + tails[h][0:1, :] for h in heads)


def _sb_finish(acc_ref, o_ref, hb):
    for h in range(hb):
        o_ref[0, :, h * HEAD_W:(h + 1) * HEAD_W] = acc_ref[h].T.astype(BF16)


def _sb_attn_kernel(tq, tk, n_k, q_off, hb, q_ref, k_ref, vt_ref, o_ref, acc_ref):
    n_int, n_tot, qmin = _kv_range("sb", pl.program_id(2), tq, tk, n_k, q_off)
    q_pos = qmin + lax.broadcasted_iota(jnp.int32, (1, tq), 1)
    suffix2 = _suffix_matrix(tk)
    acc_ref[...] = jnp.zeros_like(acc_ref)

    def step(masked, t, later):
        j = n_tot - 1 - t
        k0 = pl.multiple_of(j * tk, tk)
        valid = (k0 + lax.broadcasted_iota(jnp.int32, (tk, 1), 0)) < q_pos if masked else None
        kbs = [k_ref[0, pl.ds(k0, tk), h * HEAD_W:(h + 1) * HEAD_W] for h in range(hb)]
        vtbs = [vt_ref[0, h, j] for h in range(hb)]
        return _sb_block(q_ref, acc_ref, suffix2, kbs, vtbs, valid, later)

    later = (jnp.zeros((1, tq), F32),) * hb
    later = lax.fori_loop(0, n_tot - n_int, functools.partial(step, True), later)
    lax.fori_loop(n_tot - n_int, n_tot, functools.partial(step, False), later)
    _sb_finish(acc_ref, o_ref, hb)


def _attention(kind, q, k, vt, q_off, extra, cfg):
    b, t_q, width = q.shape
    n_k, tk = vt.shape[2], vt.shape[4]
    t_k = k.shape[1]
    assert t_k == n_k * tk
    tq = _tile(t_q, ATTN_Q_TILE)
    hb = DIFF_HEADS_PER_STEP if kind == "diff" else SB_HEADS_PER_STEP
    kv_mode = dict(pipeline_mode=pl.Buffered(1)) if hb == N_HEADS else {}
    q_spec = pl.BlockSpec((1, tq, hb * HEAD_W), lambda bi, h, i: (bi, i, h))
    in_specs = [q_spec,
                pl.BlockSpec((1, t_k, hb * HEAD_W), lambda bi, h, i: (bi, 0, h), **kv_mode),
                pl.BlockSpec((1, hb, n_k, HEAD_W, tk), lambda bi, h, i: (bi, h, 0, 0, 0), **kv_mode)]
    for e in extra:
        in_specs.append(pl.BlockSpec(e.shape, lambda bi, h, i: (0, 0)))
    if kind == "diff":
        body = functools.partial(_diff_attn_kernel, tq, tk, n_k, q_off, cfg, hb)
        scratch = [pltpu.VMEM((2 * hb, tq, HEAD_W), BF16), pltpu.VMEM((2 * hb, HEAD_W, tq), F32)]
    else:
        body = functools.partial(_sb_attn_kernel, tq, tk, n_k, q_off, hb)
        scratch = [pltpu.VMEM((hb, HEAD_W, tq), F32)]
    return pl.pallas_call(
        body,
        grid=(b, N_HEADS // hb, t_q // tq),
        in_specs=in_specs,
        out_specs=q_spec,
        out_shape=jax.ShapeDtypeStruct((b, t_q, width), BF16),
        scratch_shapes=scratch,
        compiler_params=_params("parallel", "parallel", "arbitrary"),
        name=kind + "_attn",
    )(q, k, vt, *extra)


def _past_operands(kp_ref, vp_ref, k0, tk):
    heads = range(N_HEADS)
    rows = lambda h: pl.ds(k0 * N_HEADS + h, tk, stride=N_HEADS)
    kbs = [kp_ref[0, rows(h), :].astype(BF16) for h in heads]
    vtbs = [vp_ref[0, rows(h), :].T.astype(BF16) for h in heads]
    return kbs, vtbs


def _new_operands(kn_ref, vn_ref):
    heads = range(N_HEADS)
    kbs = [kn_ref[0, :, h * HEAD_W:(h + 1) * HEAD_W].astype(BF16) for h in heads]
    vtbs = [vn_ref[0, :, h * HEAD_W:(h + 1) * HEAD_W].T.astype(BF16) for h in heads]
    return kbs, vtbs


def _new_positions(q_off, t_new, tq):
    k_idx = lax.broadcasted_iota(jnp.int32, (t_new, 1), 0)
    q_idx = lax.broadcasted_iota(jnp.int32, (1, tq), 1)
    return k_idx, q_off + k_idx, q_off + q_idx


def _diff_cached_kernel(tq, tk, n_past, q_off, t_real, lam_init,
                        q_ref, kp_ref, vp_ref, kn_ref, vn_ref, lq1_ref, lk1_ref, lq2_ref, lk2_ref, g_ref,
                        o_ref, qm_ref, acc_ref):
    _split_maps(q_ref, qm_ref, N_HEADS)
    carry = _diff_init(acc_ref, tq, N_HEADS)
    heads = range(N_HEADS)
    for j in range(n_past):
        kbs = [kp_ref[0, h, :, j * tk:(j + 1) * tk].T.astype(BF16) for h in heads]
        vtbs = [vp_ref[0, pl.ds(j * tk * N_HEADS + h, tk, stride=N_HEADS), :].T.astype(BF16) for h in heads]
        carry = _diff_block(qm_ref, acc_ref, kbs, vtbs, None, carry)
    k_idx, k_pos, q_pos = _new_positions(q_off, kn_ref.shape[1], tq)
    valid = ((k_pos >> CHUNK_SHIFT) <= (q_pos >> CHUNK_SHIFT)) & (k_idx < t_real)
    kbs, vtbs = _new_operands(kn_ref, vn_ref)
    carry = _diff_block(qm_ref, acc_ref, kbs, vtbs, valid, carry)
    _diff_finish(carry, acc_ref, (lq1_ref, lk1_ref, lq2_ref, lk2_ref), g_ref, lam_init, o_ref, N_HEADS)


def _sb_cached_kernel(tq, tk, n_past, q_off, t_real, q_ref, kp_ref, vp_ref, kn_ref, vn_ref, o_ref, acc_ref):
    acc_ref[...] = jnp.zeros_like(acc_ref)
    t_new = kn_ref.shape[1]
    k_idx, k_pos, q_pos = _new_positions(q_off, t_new, tq)
    valid = (k_pos < q_pos) & (k_idx < t_real)
    kbs, vtbs = _new_operands(kn_ref, vn_ref)
    later = (jnp.zeros((1, tq), F32),) * N_HEADS
    later = _sb_block(q_ref, acc_ref, _suffix_matrix(t_new), kbs, vtbs, valid, later)
    suffix2 = _suffix_matrix(tk)

    def past(t, later):
        kbs, vtbs = _past_operands(kp_ref, vp_ref, pl.multiple_of((n_past - 1 - t) * tk, tk), tk)
        return _sb_block(q_ref, acc_ref, suffix2, kbs, vtbs, None, later)

    lax.fori_loop(0, n_past, past, later)
    _sb_finish(acc_ref, o_ref, N_HEADS)


def _attention_cached(kind, q, k_past, v_past, k_new, v_new, t_real, extra, lam_init):
    b, tq, width = q.shape
    p_len, t_new = v_past.shape[1] // N_HEADS, k_new.shape[1]
    tk = _tile(p_len, SB_K_TILE)
    n_past = p_len // tk
    row3 = lambda t: pl.BlockSpec((1, t, width), lambda bi: (bi, 0, 0))
    past_spec = pl.BlockSpec((1, p_len * N_HEADS, HEAD_W), lambda bi: (bi, 0, 0))
    k_spec = pl.BlockSpec((1, N_HEADS, HEAD_W, p_len), lambda bi: (bi, 0, 0, 0)) if kind == "diff" else past_spec
    in_specs = [row3(tq), k_spec, past_spec, row3(t_new), row3(t_new)]
    for e in extra:
        in_specs.append(pl.BlockSpec(e.shape, lambda bi: (0, 0)))
    if kind == "diff":
        body = functools.partial(_diff_cached_kernel, tq, tk, n_past, p_len, t_real, lam_init)
        scratch = [pltpu.VMEM((2 * N_HEADS, tq, HEAD_W), BF16), pltpu.VMEM((2 * N_HEADS, HEAD_W, tq), F32)]
    else:
        body = functools.partial(_sb_cached_kernel, tq, tk, n_past, p_len, t_real)
        scratch = [pltpu.VMEM((N_HEADS, HEAD_W, tq), F32)]
    return pl.pallas_call(
        body,
        grid=(b,),
        in_specs=in_specs,
        out_specs=row3(tq),
        out_shape=jax.ShapeDtypeStruct((b, tq, width), BF16),
        scratch_shapes=scratch,
        compiler_params=_params("parallel"),
        name=kind + "_attn_cached",
    )(q, k_past, v_past, k_new, v_new, *extra)


def _out_kernel(oa_ref, ob_ref, ga_ref, gb_ref, h_ref, wua_ref, wub_ref, wo_ref, n_ref, o_ref):
    pa = jnp.dot(oa_ref[...], wua_ref[...], preferred_element_type=F32)
    pb = jnp.dot(ob_ref[...], wub_ref[...], preferred_element_type=F32)
    merged = (ga_ref[...] * pa + gb_ref[...] * pb).astype(BF16)
    r = jnp.dot(merged, wo_ref[...], preferred_element_type=F32)
    o_ref[...] = h_ref[...] + _rms(r, n_ref[...])


def _merge_out(oa, ob, ga, gb, h, wua, wub, wo, n_post):
    n, d = h.shape
    w_attn = oa.shape[1]
    tm = _tile(n, OUT_TOKEN_TILE)
    row = lambda wd: pl.BlockSpec((tm, wd), lambda i: (i, 0))
    const = lambda shape: pl.BlockSpec(shape, lambda i: (0, 0), pipeline_mode=pl.Buffered(1))
    return pl.pallas_call(
        _out_kernel,
        grid=(n // tm,),
        in_specs=[row(w_attn), row(w_attn), row(d), row(d), row(d),
                  const(wua.shape), const(wub.shape), const(wo.shape), const(n_post.shape)],
        out_specs=row(d),
        out_shape=jax.ShapeDtypeStruct((n, d), F32),
        compiler_params=_params("parallel"),
        name="merge_out",
    )(oa, ob, ga, gb, h, wua, wub, wo, n_post)


def _rope_tables(pos):
    half = ROT_DIM // 2
    inv_freq = jnp.power(ROPE_THETA, -jnp.arange(0, ROT_DIM, 2, dtype=F32) / ROT_DIM)
    ang = pos.astype(F32)[:, None] * inv_freq[None, :]
    cos, sin = jnp.cos(ang), jnp.sin(ang)
    t = pos.shape[0]
    ones = jnp.ones((t, MAP_W - ROT_DIM), F32)
    zeros = jnp.zeros((t, MAP_W - ROT_DIM), F32)
    zh = jnp.zeros((t, half), F32)
    c = jnp.concatenate([cos, cos, ones], axis=1)
    sa = jnp.concatenate([-sin, zh, zeros], axis=1)
    sb = jnp.concatenate([zh, sin, zeros], axis=1)
    return tuple(jnp.concatenate([m, m], axis=1) for m in (c, sa, sb))


def _pad_ffn(w, axis):
    f = w.shape[axis]
    fp = -(-f // FFN_HIDDEN_TILE) * FFN_HIDDEN_TILE
    shape = list(w.shape)
    shape[axis] = fp - f
    return jnp.concatenate([w.astype(BF16), jnp.zeros(shape, BF16)], axis=axis)


def _layer(x, pos, past, lam_init, p):
    b, t, d = x.shape
    n = b * t
    w_a = N_HEADS * HEAD_W
    vec = lambda v: v.reshape(1, -1)
    w_in = p["w_in"].astype(BF16)
    seg = lambda lo, hi: w_in[:, lo:hi]

    h1, xn = _ffn(x.reshape(n, d), vec(p["n1a"]), vec(p["n1b"]), vec(p["nma"]),
                  _pad_ffn(p["f1g"], 1), _pad_ffn(p["f1u"], 1), _pad_ffn(p["f1d"], 0))

    tables = tuple(jnp.tile(m, (b, 1)) for m in _rope_tables(pos))
    w_qa, w_ka, w_va, w_qb, w_kb, w_vb = (seg(k * w_a, (k + 1) * w_a) for k in range(6))
    qa_part = (w_qa, "q_rope", MAP_W ** -0.5 * LOG2E, None)
    qb_part = (w_qb, "q", HEAD_W ** -0.5 * LOG2E, None)
    (g_a,), = _proj(xn, [(seg(6 * w_a, 6 * w_a + d), "gate", None, None)], tile=GATE_TOKEN_TILE)
    (g_b,), = _proj(xn, [(seg(6 * w_a + d, 6 * w_a + 2 * d), "gate", None, None)], tile=GATE_TOKEN_TILE)

    shape3 = lambda a: a.reshape(b, t, w_a)
    diff_extra = [vec(p["lq1"]), vec(p["lk1"]), vec(p["lq2"]), vec(p["lk2"]), p["subln_g"].reshape(-1, 1)]
    if past is None:
        (q_a,), (kat, ka16) = _proj(xn, [qa_part, (w_ka, "k_rope", None, None)], tables=tables, seq=t,
                                    tile=PAIR_TOKEN_TILE)
        (q_b,), (kb32, kb16) = _proj(xn, [qb_part, (w_kb, "k", None, None)], tile=PAIR_TOKEN_TILE)
        (va32, vat), (vb32, vbt) = _proj(xn, [(w_va, "v_t", None, _tile(t, DIFF_K_TILE)),
                                              (w_vb, "v_t", None, _tile(t, SB_K_TILE))],
                                         seq=t, tile=PAIR_TOKEN_TILE)
        k_a_rows = kat.reshape(b, N_HEADS, 2, MAP_W, t).transpose(0, 4, 1, 2, 3)
        o_a = _attention("diff", shape3(q_a), shape3(ka16), vat, 0, diff_extra, lam_init)
        o_b = _attention("sb", shape3(q_b), shape3(kb16), vbt, 0, [], None)
    else:
        (q_a,), (ka32,) = _proj(xn, [qa_part, (w_ka, "kv_rope", None, None)], tables=tables,
                                tile=PAIR_TOKEN_TILE)
        (q_b,), (kb32,) = _proj(xn, [qb_part, (w_kb, "kv", None, None)], tile=PAIR_TOKEN_TILE)
        (va32,), (vb32,) = _proj(xn, [(w_va, "kv", None, None), (w_vb, "kv", None, None)],
                                 tile=PAIR_TOKEN_TILE)
        k_a_rows = ka32.reshape(b, t, N_HEADS, 2, MAP_W)
        pka, pva, pkb, pvb = past
        t_p = -(-t // V7X_LANES) * V7X_LANES
        pad_rows = lambda a: jnp.pad(shape3(a), ((0, 0), (0, t_p - t), (0, 0)))
        by_head = lambda c: c.reshape(c.shape[0], c.shape[1] * N_HEADS, HEAD_W)
        pka_t = pka.transpose(0, 2, 3, 4, 1).reshape(pka.shape[0], N_HEADS, HEAD_W, pka.shape[1])
        o_a = _attention_cached("diff", pad_rows(q_a), pka_t, by_head(pva), pad_rows(ka32),
                                pad_rows(va32), t, diff_extra, lam_init)
        o_b = _attention_cached("sb", pad_rows(q_b), by_head(pkb), by_head(pvb), pad_rows(kb32),
                                pad_rows(vb32), t, [], None)
    o_a = o_a[:, :t].reshape(n, w_a)
    o_b = o_b[:, :t].reshape(n, w_a)

    h2 = _merge_out(o_a, o_b, g_a, g_b, h1, p["w_up_a"].astype(BF16), p["w_up_b"].astype(BF16),
                    p["w_o"].astype(BF16), vec(p["nmb"]))
    y = _ffn(h2, vec(p["n2a"]), vec(p["n2b"]), None,
             _pad_ffn(p["f2g"], 1), _pad_ffn(p["f2u"], 1), _pad_ffn(p["f2d"], 0))
    rows = (k_a_rows, va32.reshape(b, t, N_HEADS, HEAD_W),
            kb32.reshape(b, t, N_HEADS, HEAD_W), vb32.reshape(b, t, N_HEADS, HEAD_W))
    return y.reshape(b, t, d), rows


def _stack(xs):
    return xs[0][None] if len(xs) == 1 else jnp.stack(xs)


def kernel(x_prompt, x_sample, cache_diff_k, cache_diff_v, cache_sb_k, cache_sb_v, w_in, w_up_a, w_up_b, w_o, lam_q1, lam_k1, lam_q2, lam_k2, subln_g, norm_ffn1_pre, norm_ffn1_post, norm_mix_pre, norm_mix_post, norm_ffn2_pre, norm_ffn2_post, ffn1_w_gate, ffn1_w_up, ffn1_w_down, ffn2_w_gate, ffn2_w_up, ffn2_w_down):
    depth = w_in.shape[0]
    pos_p = jnp.arange(x_prompt.shape[1], dtype=jnp.int32)
    pos_s = cache_diff_k.shape[2] + jnp.arange(x_sample.shape[1], dtype=jnp.int32)
    hp, hs = x_prompt, x_sample
    rows_p, rows_s = [], []
    for l in range(depth):
        lam_init = 0.8 - 0.6 * math.exp(-0.3 * l)
        p = dict(w_in=w_in[l], w_up_a=w_up_a[l], w_up_b=w_up_b[l], w_o=w_o[l],
                 lq1=lam_q1[l], lk1=lam_k1[l], lq2=lam_q2[l], lk2=lam_k2[l], subln_g=subln_g[l],
                 n1a=norm_ffn1_pre[l], n1b=norm_ffn1_post[l], nma=norm_mix_pre[l], nmb=norm_mix_post[l],
                 n2a=norm_ffn2_pre[l], n2b=norm_ffn2_post[l],
                 f1g=ffn1_w_gate[l], f1u=ffn1_w_up[l], f1d=ffn1_w_down[l],
                 f2g=ffn2_w_gate[l], f2u=ffn2_w_up[l], f2d=ffn2_w_down[l])
        hp, rp = _layer(hp, pos_p, None, lam_init, p)
        past = (cache_diff_k[l], cache_diff_v[l], cache_sb_k[l], cache_sb_v[l])
        hs, rs = _layer(hs, pos_s, past, lam_init, p)
        rows_p.append(rp)
        rows_s.append(rs)
    outs_p = [_stack([r[i] for r in rows_p]) for i in range(4)]
    outs_s = [_stack([r[i] for r in rows_s]) for i in range(4)]
    return (hp, hs, *outs_p, *outs_s)
```

```python
import functools
import math

import jax
import jax.numpy as jnp
from jax import lax
from jax.experimental import pallas as pl
from jax.experimental.pallas import tpu as pltpu

F32 = jnp.float32
BF16 = jnp.bfloat16

EPS = 1e-6
CHUNK = 64
CHUNK_SHIFT = CHUNK.bit_length() - 1
assert CHUNK == 1 << CHUNK_SHIFT
N_HEADS = 8
HEAD_W = 128
MAP_W = 64
ROT_DIM = 16
ROPE_THETA = 500000.0
NEG = -1e30

V7X_LANES = 128
V7X_MXU_DIM = 256
V7X_VMEM_BYTES = 64 * 1024 * 1024
VMEM_LIMIT_BYTES = V7X_VMEM_BYTES - 8 * 1024 * 1024

FFN_TOKEN_TILE = 512
FFN_HIDDEN_TILE = 512
PROJ_TOKEN_TILE = 1024
PAIR_TOKEN_TILE = 512
GATE_TOKEN_TILE = 512
OUT_TOKEN_TILE = 256
ATTN_Q_TILE = 512
DIFF_K_TILE = 512
SB_K_TILE = V7X_MXU_DIM
DIFF_HEADS_PER_STEP = 4
SB_HEADS_PER_STEP = 8
LOG2E = math.log2(math.e)
F32_MIN_DENORMAL_LOG2 = -149
SB_DEAD_LOG2 = float(-F32_MIN_DENORMAL_LOG2 + 11)


def _tile(n, target):
    t = min(n, target)
    while n % t:
        t -= 1
    return t


def _params(*sem):
    return pltpu.CompilerParams(dimension_semantics=sem, vmem_limit_bytes=VMEM_LIMIT_BYTES)


def _rms(x, g):
    return x * lax.rsqrt(jnp.mean(x * x, axis=-1, keepdims=True) + EPS) * g


def _ffn_kernel(n_f, has_next, x_ref, npre_ref, npost_ref, *refs):
    if has_next:
        nnext_ref, wg_ref, wu_ref, wd_ref, h_ref, xn_ref, xs_ref, acc_ref = refs
    else:
        wg_ref, wu_ref, wd_ref, h_ref, xs_ref, acc_ref = refs
    f = pl.program_id(1)

    @pl.when(f == 0)
    def _():
        xs_ref[...] = _rms(x_ref[...], npre_ref[...]).astype(BF16)
        acc_ref[...] = jnp.zeros_like(acc_ref)

    xs = xs_ref[...]
    g = jnp.dot(xs, wg_ref[...], preferred_element_type=F32)
    u = jnp.dot(xs, wu_ref[...], preferred_element_type=F32)
    a = (g * jax.nn.sigmoid(g) * u).astype(BF16)
    acc_ref[...] += jnp.dot(a, wd_ref[...], preferred_element_type=F32)

    @pl.when(f == n_f - 1)
    def _():
        h = x_ref[...] + 0.5 * _rms(acc_ref[...], npost_ref[...])
        h_ref[...] = h
        if has_next:
            xn_ref[...] = _rms(h, nnext_ref[...]).astype(BF16)


def _ffn(x, n_pre, n_post, n_next, wg, wu, wd):
    n, d = x.shape
    fp = wg.shape[1]
    tm = _tile(n, FFN_TOKEN_TILE)
    tf = _tile(fp, FFN_HIDDEN_TILE)
    n_f = fp // tf
    has_next = n_next is not None
    row = pl.BlockSpec((tm, d), lambda i, f: (i, 0))
    vec = pl.BlockSpec((1, d), lambda i, f: (0, 0))
    in_specs = [row, vec, vec] + ([vec] if has_next else []) + [
        pl.BlockSpec((d, tf), lambda i, f: (0, f)),
        pl.BlockSpec((d, tf), lambda i, f: (0, f)),
        pl.BlockSpec((tf, d), lambda i, f: (f, 0)),
    ]
    out_shape = [jax.ShapeDtypeStruct((n, d), F32)]
    out_specs = [row]
    if has_next:
        out_shape.append(jax.ShapeDtypeStruct((n, d), BF16))
        out_specs.append(row)
    args = [x, n_pre, n_post] + ([n_next] if has_next else []) + [wg, wu, wd]
    res = pl.pallas_call(
        functools.partial(_ffn_kernel, n_f, has_next),
        grid=(n // tm, n_f),
        in_specs=in_specs,
        out_specs=out_specs,
        out_shape=out_shape,
        scratch_shapes=[pltpu.VMEM((tm, d), BF16), pltpu.VMEM((tm, d), F32)],
        compiler_params=_params("parallel", "arbitrary"),
        name="ffn_next" if has_next else "ffn",
    )(*args)
    return res if has_next else res[0]


def _rope(r, c, sa, sb):
    heads = []
    for h in range(r.shape[1] // HEAD_W):
        x = r[:, h * HEAD_W:(h + 1) * HEAD_W]
        up = pltpu.roll(x, HEAD_W - ROT_DIM // 2, 1)
        dn = pltpu.roll(x, ROT_DIM // 2, 1)
        heads.append(x * c + up * sa + dn * sb)
    return jnp.concatenate(heads, axis=1)


PROJ_OUTPUTS = {"q": 1, "q_rope": 1, "gate": 1, "k": 2, "k_rope": 2, "kv": 1, "kv_rope": 1, "v_t": 2}
PROJ_ROTARY = ("q_rope", "k_rope", "kv_rope")


def _proj_epilogue(mode, scale, tk, r, tables, refs):
    if mode in PROJ_ROTARY:
        r = _rope(r, *tables)
    if mode in ("q", "q_rope"):
        (q_ref,) = refs
        q_ref[...] = (r * scale).astype(BF16)
    elif mode == "gate":
        (g_ref,) = refs
        g_ref[...] = jax.nn.sigmoid(r)
    elif mode == "k":
        k32_ref, k16_ref = refs
        k32_ref[...] = r
        k16_ref[...] = r.astype(BF16)
    elif mode == "k_rope":
        kt_ref, k16_ref = refs
        k16_ref[...] = r.astype(BF16)
        for h in range(N_HEADS):
            kt_ref[0, h] = r[:, h * HEAD_W:(h + 1) * HEAD_W].T
    elif mode in ("kv", "kv_rope"):
        (r32_ref,) = refs
        r32_ref[...] = r
    else:
        v32_ref, vt_ref = refs
        v32_ref[...] = r
        for h in range(N_HEADS):
            for jb in range(r.shape[0] // tk):
                blk = r[jb * tk:(jb + 1) * tk, h * HEAD_W:(h + 1) * HEAD_W]
                vt_ref[0, h, jb] = blk.T.astype(BF16)


def _proj_kernel(parts, has_tables, x_ref, *refs):
    w_refs, refs = refs[:len(parts)], refs[len(parts):]
    tables = None
    if has_tables:
        tables, refs = tuple(t[...] for t in refs[:3]), refs[3:]
    x = x_ref[...]
    for (mode, scale, tk), w_ref in zip(parts, w_refs):
        outs, refs = refs[:PROJ_OUTPUTS[mode]], refs[PROJ_OUTPUTS[mode]:]
        _proj_epilogue(mode, scale, tk, jnp.dot(x, w_ref[...], preferred_element_type=F32), tables, outs)


def _proj(xn, parts, *, tables=None, seq=None, tile=PROJ_TOKEN_TILE):
    n, d = xn.shape
    tm = _tile(n if seq is None else seq, tile)
    row = lambda wd: pl.BlockSpec((tm, wd), lambda i: (i, 0))
    has_tables = any(mode in PROJ_ROTARY for _, mode, _, _ in parts)
    in_specs = [row(d)] + [pl.BlockSpec(w.shape, lambda i: (0, 0)) for w, _, _, _ in parts]
    args = [xn] + [w for w, _, _, _ in parts]
    if has_tables:
        in_specs += [row(HEAD_W)] * 3
        args += list(tables)
    out_shape, out_specs = [], []
    for w, mode, _, tk in parts:
        width = w.shape[1]
        if mode in ("q", "q_rope"):
            out_shape += [jax.ShapeDtypeStruct((n, width), BF16)]
            out_specs += [row(width)]
        elif mode in ("gate", "kv", "kv_rope"):
            out_shape += [jax.ShapeDtypeStruct((n, width), F32)]
            out_specs += [row(width)]
        elif mode == "k":
            out_shape += [jax.ShapeDtypeStruct((n, width), F32), jax.ShapeDtypeStruct((n, width), BF16)]
            out_specs += [row(width), row(width)]
        elif mode == "k_rope":
            assert seq % tm == 0
            per_seq = seq // tm
            out_shape += [jax.ShapeDtypeStruct((n // seq, N_HEADS, HEAD_W, seq), F32),
                          jax.ShapeDtypeStruct((n, width), BF16)]
            out_specs += [pl.BlockSpec((1, N_HEADS, HEAD_W, tm),
                                       lambda i, per_seq=per_seq: (i // per_seq, 0, 0, i % per_seq)),
                          row(width)]
        else:
            assert mode == "v_t" and tm % tk == 0 and seq % tm == 0
            per_seq = seq // tm
            out_shape += [jax.ShapeDtypeStruct((n, width), F32),
                          jax.ShapeDtypeStruct((n // seq, N_HEADS, seq // tk, HEAD_W, tk), BF16)]
            out_specs += [row(width),
                          pl.BlockSpec((1, N_HEADS, tm // tk, HEAD_W, tk),
                                       lambda i, per_seq=per_seq: (i // per_seq, 0, i % per_seq, 0, 0))]
    res = pl.pallas_call(
        functools.partial(_proj_kernel, tuple((mode, scale, tk) for _, mode, scale, tk in parts), has_tables),
        grid=(n // tm,),
        in_specs=in_specs,
        out_specs=out_specs,
        out_shape=out_shape,
        compiler_params=_params("parallel"),
        name="proj_" + "_".join(mode for _, mode, _, _ in parts),
    )(*args)
    out, k = [], 0
    for _, mode, _, _ in parts:
        out.append(tuple(res[k:k + PROJ_OUTPUTS[mode]]))
        k += PROJ_OUTPUTS[mode]
    return out


def _kv_range(kind, i, tq, tk, n_k, q_off):
    qmin = q_off + i * tq
    qmax = qmin + tq - 1
    if kind == "diff":
        n_int = ((qmin // CHUNK + 1) * CHUNK) // tk
        last = ((qmax // CHUNK + 1) * CHUNK - 1) // tk
    else:
        n_int = qmin // tk
        last = (qmax - 1) // tk
    smaller = min if isinstance(i, int) else jnp.minimum
    return smaller(n_int, n_k), smaller(last + 1, n_k), qmin


def _split_maps(q_ref, qm_ref, hb):
    lane = lax.broadcasted_iota(jnp.int32, (1, HEAD_W), 1)
    for h in range(hb):
        q = q_ref[0, :, h * HEAD_W:(h + 1) * HEAD_W]
        zero = jnp.zeros_like(q)
        qm_ref[2 * h] = jnp.where(lane < MAP_W, q, zero)
        qm_ref[2 * h + 1] = jnp.where(lane >= MAP_W, q, zero)


def _diff_block(qm_ref, acc_ref, kbs, vtbs, valid, carry):
    carry, probs, alphas = _diff_softmax(_diff_scores(qm_ref, kbs), valid, carry)
    _diff_values(acc_ref, vtbs, probs, alphas)
    return carry


def _diff_scores(qm_ref, kbs):
    return [lax.dot_general(kbs[c // 2], qm_ref[c], (((1,), (1,)), ((), ())), preferred_element_type=F32)
            for c in range(2 * len(kbs))]


def _diff_softmax(scores, valid, carry):
    new, probs, alphas = [], [], []
    for c, s in enumerate(scores):
        m, l = carry[2 * c], carry[2 * c + 1]
        if valid is not None:
            s = jnp.where(valid, s, NEG)
        m_new = jnp.maximum(m, jnp.max(s, axis=0, keepdims=True))
        alpha = jnp.exp2(m - m_new)
        p = jnp.exp2(s - m_new)
        new += [m_new, alpha * l + jnp.sum(p, axis=0, keepdims=True)]
        probs.append(p.astype(BF16))
        alphas.append(alpha)
    return tuple(new), probs, alphas


def _diff_values(acc_ref, vtbs, probs, alphas):
    for c, (p, alpha) in enumerate(zip(probs, alphas)):
        acc_ref[c] = alpha * acc_ref[c] + jnp.dot(vtbs[c // 2], p, preferred_element_type=F32)


def _diff_init(acc_ref, tq, hb):
    acc_ref[...] = jnp.zeros_like(acc_ref)
    return (jnp.full((1, tq), NEG, F32), jnp.zeros((1, tq), F32)) * (2 * hb)


def _diff_finish(carry, acc_ref, lam_refs, g_ref, lam_init, o_ref, hb):
    lq1_ref, lk1_ref, lq2_ref, lk2_ref = lam_refs
    lam = (jnp.exp(jnp.sum(lq1_ref[...] * lk1_ref[...], keepdims=True))
           - jnp.exp(jnp.sum(lq2_ref[...] * lk2_ref[...], keepdims=True)) + lam_init)
    for h in range(hb):
        l0, l1 = carry[4 * h + 1], carry[4 * h + 3]
        ot = acc_ref[2 * h] * (1.0 / l0) - lam * (acc_ref[2 * h + 1] * (1.0 / l1))
        ms = jnp.mean(ot * ot, axis=0, keepdims=True)
        ot = ot * lax.rsqrt(ms + EPS) * g_ref[...] * (1.0 - lam_init)
        o_ref[0, :, h * HEAD_W:(h + 1) * HEAD_W] = ot.T.astype(BF16)


def _diff_attn_kernel(tq, tk, n_k, q_off, lam_init, hb,
                      q_ref, k_ref, vt_ref, lq1_ref, lk1_ref, lq2_ref, lk2_ref, g_ref,
                      o_ref, qm_ref, acc_ref):
    n_int, n_tot, qmin = _kv_range("diff", pl.program_id(2), tq, tk, n_k, q_off)
    _split_maps(q_ref, qm_ref, hb)
    q_chunk = (qmin + lax.broadcasted_iota(jnp.int32, (1, tq), 1)) >> CHUNK_SHIFT

    def step(masked, j, carry):
        k0 = pl.multiple_of(j * tk, tk)
        valid = None
        if masked:
            k_chunk = (k0 + lax.broadcasted_iota(jnp.int32, (tk, 1), 0)) >> CHUNK_SHIFT
            valid = k_chunk <= q_chunk
        kbs = [k_ref[0, pl.ds(k0, tk), h * HEAD_W:(h + 1) * HEAD_W] for h in range(hb)]
        vtbs = [vt_ref[0, h, j] for h in range(hb)]
        return _diff_block(qm_ref, acc_ref, kbs, vtbs, valid, carry)

    carry = _diff_init(acc_ref, tq, hb)
    carry = lax.fori_loop(0, n_int, functools.partial(step, False), carry)
    carry = lax.fori_loop(n_int, n_tot, functools.partial(step, True), carry)
    _diff_finish(carry, acc_ref, (lq1_ref, lk1_ref, lq2_ref, lk2_ref), g_ref, lam_init, o_ref, hb)


def _suffix_matrix(tk):
    s = (lax.broadcasted_iota(jnp.int32, (tk, tk), 1)
         >= lax.broadcasted_iota(jnp.int32, (tk, tk), 0)).astype(BF16)
    return jnp.concatenate([s, s], axis=1)


def _sb_block(q_ref, acc_ref, suffix2, kbs, vtbs, valid, later):
    heads = range(len(kbs))
    ws = [lax.dot_general(kbs[h], q_ref[0, :, h * HEAD_W:(h + 1) * HEAD_W], (((1,), (1,)), ((), ())),
                          preferred_element_type=F32) for h in heads]
    tails = []
    for h in heads:
        sp = jnp.maximum(ws[h], 0.0) + jnp.log(1.0 + jnp.exp2(-jnp.abs(ws[h]))) * LOG2E
        if valid is not None:
            sp = jnp.where(valid, sp, 0.0)
        hi = sp.astype(BF16)
        lo = (sp - hi.astype(F32)).astype(BF16)
        tails.append(jnp.dot(suffix2, jnp.concatenate([hi, lo], axis=0),
                             preferred_element_type=F32))
    weights = []
    for h in heads:
        logit = ws[h] - tails[h] - later[h]
        if valid is not None:
            logit = jnp.where(valid, logit, NEG)
        weights.append(jnp.exp2(logit).astype(BF16))
    for h in heads:
        acc_ref[h] += jnp.dot(vtbs[h], weights[h], preferred_element_type=F32)
    return tuple(later[h] + tails[h][0:1, :] for h in heads)


def _sb_walk(step, first, last, later):
    def alive(later):
        return (jnp.min(functools.reduce(jnp.minimum, later)) < SB_DEAD_LOG2).astype(jnp.int32)

    def cond(state):
        return jnp.logical_and(state[0] < last, state[1] > 0)

    def body(state):
        later = step(state[0], state[2])
        return state[0] + 1, alive(later), later

    return lax.while_loop(cond, body, (jnp.asarray(first, jnp.int32), alive(later), later))[2]


def _sb_finish(acc_ref, o_ref, hb):
    for h in range(hb):
        o_ref[0, :, h * HEAD_W:(h + 1) * HEAD_W] = acc_ref[h].T.astype(BF16)


def _sb_attn_kernel(tq, tk, n_k, q_off, hb, q_ref, k_ref, vt_ref, o_ref, acc_ref):
    n_int, n_tot, qmin = _kv_range("sb", pl.program_id(2), tq, tk, n_k, q_off)
    q_pos = qmin + lax.broadcasted_iota(jnp.int32, (1, tq), 1)
    suffix2 = _suffix_matrix(tk)
    acc_ref[...] = jnp.zeros_like(acc_ref)

    def step(masked, t, later):
        j = n_tot - 1 - t
        k0 = pl.multiple_of(j * tk, tk)
        valid = (k0 + lax.broadcasted_iota(jnp.int32, (tk, 1), 0)) < q_pos if masked else None
        kbs = [k_ref[0, pl.ds(k0, tk), h * HEAD_W:(h + 1) * HEAD_W] for h in range(hb)]
        vtbs = [vt_ref[0, h, j] for h in range(hb)]
        return _sb_block(q_ref, acc_ref, suffix2, kbs, vtbs, valid, later)

    later = (jnp.zeros((1, tq), F32),) * hb
    later = lax.fori_loop(0, n_tot - n_int, functools.partial(step, True), later)
    _sb_walk(functools.partial(step, False), n_tot - n_int, n_tot, later)
    _sb_finish(acc_ref, o_ref, hb)


def _attention(kind, q, k, vt, q_off, extra, cfg):
    b, t_q, width = q.shape
    n_k, tk = vt.shape[2], vt.shape[4]
    t_k = k.shape[1]
    assert t_k == n_k * tk
    tq = _tile(t_q, ATTN_Q_TILE)
    hb = DIFF_HEADS_PER_STEP if kind == "diff" else SB_HEADS_PER_STEP
    kv_mode = dict(pipeline_mode=pl.Buffered(1)) if hb == N_HEADS else {}
    q_spec = pl.BlockSpec((1, tq, hb * HEAD_W), lambda bi, h, i: (bi, i, h))
    in_specs = [q_spec,
                pl.BlockSpec((1, t_k, hb * HEAD_W), lambda bi, h, i: (bi, 0, h), **kv_mode),
                pl.BlockSpec((1, hb, n_k, HEAD_W, tk), lambda bi, h, i: (bi, h, 0, 0, 0), **kv_mode)]
    for e in extra:
        in_specs.append(pl.BlockSpec(e.shape, lambda bi, h, i: (0, 0)))
    if kind == "diff":
        body = functools.partial(_diff_attn_kernel, tq, tk, n_k, q_off, cfg, hb)
        scratch = [pltpu.VMEM((2 * hb, tq, HEAD_W), BF16), pltpu.VMEM((2 * hb, HEAD_W, tq), F32)]
    else:
        body = functools.partial(_sb_attn_kernel, tq, tk, n_k, q_off, hb)
        scratch = [pltpu.VMEM((hb, HEAD_W, tq), F32)]
    return pl.pallas_call(
        body,
        grid=(b, N_HEADS // hb, t_q // tq),
        in_specs=in_specs,
        out_specs=q_spec,
        out_shape=jax.ShapeDtypeStruct((b, t_q, width), BF16),
        scratch_shapes=scratch,
        compiler_params=_params("parallel", "parallel", "arbitrary"),
        name=kind + "_attn",
    )(q, k, vt, *extra)


def _past_operands(kp_ref, vp_ref, k0, tk):
    heads = range(N_HEADS)
    rows = lambda h: pl.ds(k0 * N_HEADS + h, tk, stride=N_HEADS)
    kbs = [kp_ref[0, rows(h), :].astype(BF16) for h in heads]
    vtbs = [vp_ref[0, rows(h), :].T.astype(BF16) for h in heads]
    return kbs, vtbs


def _new_operands(kn_ref, vn_ref):
    heads = range(N_HEADS)
    kbs = [kn_ref[0, :, h * HEAD_W:(h + 1) * HEAD_W].astype(BF16) for h in heads]
    vtbs = [vn_ref[0, :, h * HEAD_W:(h + 1) * HEAD_W].T.astype(BF16) for h in heads]
    return kbs, vtbs


def _new_positions(q_off, t_new, tq):
    k_idx = lax.broadcasted_iota(jnp.int32, (t_new, 1), 0)
    q_idx = lax.broadcasted_iota(jnp.int32, (1, tq), 1)
    return k_idx, q_off + k_idx, q_off + q_idx


def _diff_cached_kernel(tq, tk, n_past, q_off, t_real, lam_init,
                        q_ref, kp_ref, vp_ref, kn_ref, vn_ref, lq1_ref, lk1_ref, lq2_ref, lk2_ref, g_ref,
                        o_ref, qm_ref, acc_ref):
    _split_maps(q_ref, qm_ref, N_HEADS)
    carry = _diff_init(acc_ref, tq, N_HEADS)
    heads = range(N_HEADS)
    for j in range(n_past):
        kbs = [kp_ref[0, h, :, j * tk:(j + 1) * tk].T.astype(BF16) for h in heads]
        vtbs = [vp_ref[0, pl.ds(j * tk * N_HEADS + h, tk, stride=N_HEADS), :].T.astype(BF16) for h in heads]
        carry = _diff_block(qm_ref, acc_ref, kbs, vtbs, None, carry)
    k_idx, k_pos, q_pos = _new_positions(q_off, kn_ref.shape[1], tq)
    valid = ((k_pos >> CHUNK_SHIFT) <= (q_pos >> CHUNK_SHIFT)) & (k_idx < t_real)
    kbs, vtbs = _new_operands(kn_ref, vn_ref)
    carry = _diff_block(qm_ref, acc_ref, kbs, vtbs, valid, carry)
    _diff_finish(carry, acc_ref, (lq1_ref, lk1_ref, lq2_ref, lk2_ref), g_ref, lam_init, o_ref, N_HEADS)


def _sb_cached_kernel(tq, tk, n_past, q_off, t_real, q_ref, kp_ref, vp_ref, kn_ref, vn_ref, o_ref, acc_ref):
    acc_ref[...] = jnp.zeros_like(acc_ref)
    t_new = kn_ref.shape[1]
    k_idx, k_pos, q_pos = _new_positions(q_off, t_new, tq)
    valid = (k_pos < q_pos) & (k_idx < t_real)
    kbs, vtbs = _new_operands(kn_ref, vn_ref)
    later = (jnp.zeros((1, tq), F32),) * N_HEADS
    later = _sb_block(q_ref, acc_ref, _suffix_matrix(t_new), kbs, vtbs, valid, later)
    suffix2 = _suffix_matrix(tk)

    def past(t, later):
        kbs, vtbs = _past_operands(kp_ref, vp_ref, pl.multiple_of((n_past - 1 - t) * tk, tk), tk)
        return _sb_block(q_ref, acc_ref, suffix2, kbs, vtbs, None, later)

    _sb_walk(past, 0, n_past, later)
    _sb_finish(acc_ref, o_ref, N_HEADS)


def _attention_cached(kind, q, k_past, v_past, k_new, v_new, t_real, extra, lam_init):
    b, tq, width = q.shape
    p_len, t_new = v_past.shape[1] // N_HEADS, k_new.shape[1]
    tk = _tile(p_len, SB_K_TILE)
    n_past = p_len // tk
    row3 = lambda t: pl.BlockSpec((1, t, width), lambda bi: (bi, 0, 0))
    past_spec = pl.BlockSpec((1, p_len * N_HEADS, HEAD_W), lambda bi: (bi, 0, 0))
    k_spec = pl.BlockSpec((1, N_HEADS, HEAD_W, p_len), lambda bi: (bi, 0, 0, 0)) if kind == "diff" else past_spec
    in_specs = [row3(tq), k_spec, past_spec, row3(t_new), row3(t_new)]
    for e in extra:
        in_specs.append(pl.BlockSpec(e.shape, lambda bi: (0, 0)))
    if kind == "diff":
        body = functools.partial(_diff_cached_kernel, tq, tk, n_past, p_len, t_real, lam_init)
        scratch = [pltpu.VMEM((2 * N_HEADS, tq, HEAD_W), BF16), pltpu.VMEM((2 * N_HEADS, HEAD_W, tq), F32)]
    else:
        body = functools.partial(_sb_cached_kernel, tq, tk, n_past, p_len, t_real)
        scratch = [pltpu.VMEM((N_HEADS, HEAD_W, tq), F32)]
    return pl.pallas_call(
        body,
        grid=(b,),
        in_specs=in_specs,
        out_specs=row3(tq),
        out_shape=jax.ShapeDtypeStruct((b, tq, width), BF16),
        scratch_shapes=scratch,
        compiler_params=_params("parallel"),
        name=kind + "_attn_cached",
    )(q, k_past, v_past, k_new, v_new, *extra)


def _out_kernel(oa_ref, ob_ref, ga_ref, gb_ref, h_ref, wua_ref, wub_ref, wo_ref, n_ref, o_ref):
    pa = jnp.dot(oa_ref[...], wua_ref[...], preferred_element_type=F32)
    pb = jnp.dot(ob_ref[...], wub_ref[...], preferred_element_type=F32)
    merged = (ga_ref[...] * pa + gb_ref[...] * pb).astype(BF16)
    r = jnp.dot(merged, wo_ref[...], preferred_element_type=F32)
    o_ref[...] = h_ref[...] + _rms(r, n_ref[...])


def _merge_out(oa, ob, ga, gb, h, wua, wub, wo, n_post):
    n, d = h.shape
    w_attn = oa.shape[1]
    tm = _tile(n, OUT_TOKEN_TILE)
    row = lambda wd: pl.BlockSpec((tm, wd), lambda i: (i, 0))
    const = lambda shape: pl.BlockSpec(shape, lambda i: (0, 0), pipeline_mode=pl.Buffered(1))
    return pl.pallas_call(
        _out_kernel,
        grid=(n // tm,),
        in_specs=[row(w_attn), row(w_attn), row(d), row(d), row(d),
                  const(wua.shape), const(wub.shape), const(wo.shape), const(n_post.shape)],
        out_specs=row(d),
        out_shape=jax.ShapeDtypeStruct((n, d), F32),
        compiler_params=_params("parallel"),
        name="merge_out",
    )(oa, ob, ga, gb, h, wua, wub, wo, n_post)


def _rope_tables(pos):
    half = ROT_DIM // 2
    inv_freq = jnp.power(ROPE_THETA, -jnp.arange(0, ROT_DIM, 2, dtype=F32) / ROT_DIM)
    ang = pos.astype(F32)[:, None] * inv_freq[None, :]
    cos, sin = jnp.cos(ang), jnp.sin(ang)
    t = pos.shape[0]
    ones = jnp.ones((t, MAP_W - ROT_DIM), F32)
    zeros = jnp.zeros((t, MAP_W - ROT_DIM), F32)
    zh = jnp.zeros((t, half), F32)
    c = jnp.concatenate([cos, cos, ones], axis=1)
    sa = jnp.concatenate([-sin, zh, zeros], axis=1)
    sb = jnp.concatenate([zh, sin, zeros], axis=1)
    return tuple(jnp.concatenate([m, m], axis=1) for m in (c, sa, sb))


def _pad_ffn(w, axis):
    f = w.shape[axis]
    fp = -(-f // FFN_HIDDEN_TILE) * FFN_HIDDEN_TILE
    shape = list(w.shape)
    shape[axis] = fp - f
    return jnp.concatenate([w.astype(BF16), jnp.zeros(shape, BF16)], axis=axis)


def _layer(x, pos, past, lam_init, p):
    b, t, d = x.shape
    n = b * t
    w_a = N_HEADS * HEAD_W
    vec = lambda v: v.reshape(1, -1)
    w_in = p["w_in"].astype(BF16)
    seg = lambda lo, hi: w_in[:, lo:hi]

    h1, xn = _ffn(x.reshape(n, d), vec(p["n1a"]), vec(p["n1b"]), vec(p["nma"]),
                  _pad_ffn(p["f1g"], 1), _pad_ffn(p["f1u"], 1), _pad_ffn(p["f1d"], 0))

    tables = tuple(jnp.tile(m, (b, 1)) for m in _rope_tables(pos))
    w_qa, w_ka, w_va, w_qb, w_kb, w_vb = (seg(k * w_a, (k + 1) * w_a) for k in range(6))
    qa_part = (w_qa, "q_rope", MAP_W ** -0.5 * LOG2E, None)
    qb_part = (w_qb, "q", HEAD_W ** -0.5 * LOG2E, None)
    (g_a,), = _proj(xn, [(seg(6 * w_a, 6 * w_a + d), "gate", None, None)], tile=GATE_TOKEN_TILE)
    (g_b,), = _proj(xn, [(seg(6 * w_a + d, 6 * w_a + 2 * d), "gate", None, None)], tile=GATE_TOKEN_TILE)

    shape3 = lambda a: a.reshape(b, t, w_a)
    diff_extra = [vec(p["lq1"]), vec(p["lk1"]), vec(p["lq2"]), vec(p["lk2"]), p["subln_g"].reshape(-1, 1)]
    if past is None:
        (q_a,), (kat, ka16) = _proj(xn, [qa_part, (w_ka, "k_rope", None, None)], tables=tables, seq=t,
                                    tile=PAIR_TOKEN_TILE)
        (q_b,), (kb32, kb16) = _proj(xn, [qb_part, (w_kb, "k", None, None)], tile=PAIR_TOKEN_TILE)
        (va32, vat), (vb32, vbt) = _proj(xn, [(w_va, "v_t", None, _tile(t, DIFF_K_TILE)),
                                              (w_vb, "v_t", None, _tile(t, SB_K_TILE))],
                                         seq=t, tile=PAIR_TOKEN_TILE)
        k_a_rows = kat.reshape(b, N_HEADS, 2, MAP_W, t).transpose(0, 4, 1, 2, 3)
        o_a = _attention("diff", shape3(q_a), shape3(ka16), vat, 0, diff_extra, lam_init)
        o_b = _attention("sb", shape3(q_b), shape3(kb16), vbt, 0, [], None)
    else:
        (q_a,), (ka32,) = _proj(xn, [qa_part, (w_ka, "kv_rope", None, None)], tables=tables,
                                tile=PAIR_TOKEN_TILE)
        (q_b,), (kb32,) = _proj(xn, [qb_part, (w_kb, "kv", None, None)], tile=PAIR_TOKEN_TILE)
        (va32,), (vb32,) = _proj(xn, [(w_va, "kv", None, None), (w_vb, "kv", None, None)],
                                 tile=PAIR_TOKEN_TILE)
        k_a_rows = ka32.reshape(b, t, N_HEADS, 2, MAP_W)
        pka, pva, pkb, pvb = past
        t_p = -(-t // V7X_LANES) * V7X_LANES
        pad_rows = lambda a: jnp.pad(shape3(a), ((0, 0), (0, t_p - t), (0, 0)))
        by_head = lambda c: c.reshape(c.shape[0], c.shape[1] * N_HEADS, HEAD_W)
        pka_t = pka.transpose(0, 2, 3, 4, 1).reshape(pka.shape[0], N_HEADS, HEAD_W, pka.shape[1])
        o_a = _attention_cached("diff", pad_rows(q_a), pka_t, by_head(pva), pad_rows(ka32),
                                pad_rows(va32), t, diff_extra, lam_init)
        o_b = _attention_cached("sb", pad_rows(q_b), by_head(pkb), by_head(pvb), pad_rows(kb32),
                                pad_rows(vb32), t, [], None)
    o_a = o_a[:, :t].reshape(n, w_a)
    o_b = o_b[:, :t].reshape(n, w_a)

    h2 = _merge_out(o_a, o_b, g_a, g_b, h1, p["w_up_a"].astype(BF16), p["w_up_b"].astype(BF16),
                    p["w_o"].astype(BF16), vec(p["nmb"]))
    y = _ffn(h2, vec(p["n2a"]), vec(p["n2b"]), None,
             _pad_ffn(p["f2g"], 1), _pad_ffn(p["f2u"], 1), _pad_ffn(p["f2d"], 0))
    rows = (k_a_rows, va32.reshape(b, t, N_HEADS, HEAD_W),
            kb32.reshape(b, t, N_HEADS, HEAD_W), vb32.reshape(b, t, N_HEADS, HEAD_W))
    return y.reshape(b, t, d), rows


def _stack(xs):
    return xs[0][None] if len(xs) == 1 else jnp.stack(xs)


def kernel(x_prompt, x_sample, cache_diff_k, cache_diff_v, cache_sb_k, cache_sb_v, w_in, w_up_a, w_up_b, w_o, lam_q1, lam_k1, lam_q2, lam_k2, subln_g, norm_ffn1_pre, norm_ffn1_post, norm_mix_pre, norm_mix_post, norm_ffn2_pre, norm_ffn2_post, ffn1_w_gate, ffn1_w_up, ffn1_w_down, ffn2_w_gate, ffn2_w_up, ffn2_w_down):
    depth = w_in.shape[0]
    pos_p = jnp.arange(x_prompt.shape[1], dtype=jnp.int32)
    pos_s = cache_diff_k.shape[2] + jnp.arange(x_sample.shape[1], dtype=jnp.int32)
    hp, hs = x_prompt, x_sample
    rows_p, rows_s = [], []
    for l in range(depth):
        lam_init = 0.8 - 0.6 * math.exp(-0.3 * l)
        p = dict(w_in=w_in[l], w_up_a=w_up_a[l], w_up_b=w_up_b[l], w_o=w_o[l],
                 lq1=lam_q1[l], lk1=lam_k1[l], lq2=lam_q2[l], lk2=lam_k2[l], subln_g=subln_g[l],
                 n1a=norm_ffn1_pre[l], n1b=norm_ffn1_post[l], nma=norm_mix_pre[l], nmb=norm_mix_post[l],
                 n2a=norm_ffn2_pre[l], n2b=norm_ffn2_post[l],
                 f1g=ffn1_w_gate[l], f1u=ffn1_w_up[l], f1d=ffn1_w_down[l],
                 f2g=ffn2_w_gate[l], f2u=ffn2_w_up[l], f2d=ffn2_w_down[l])
        hp, rp = _layer(hp, pos_p, None, lam_init, p)
        past = (cache_diff_k[l], cache_diff_v[l], cache_sb_k[l], cache_sb_v[l])
        hs, rs = _layer(hs, pos_s, past, lam_init, p)
        rows_p.append(rp)
        rows_s.append(rs)
    outs_p = [_stack([r[i] for r in rows_p]) for i in range(4)]
    outs_s = [_stack([r[i] for r in rows_s]) for i in range(4)]
    return (hp, hs, *outs_p, *outs_s)
```

```python
import functools
import math

import jax
import jax.numpy as jnp
from jax import lax
from jax.experimental import pallas as pl
from jax.experimental.pallas import tpu as pltpu

F32 = jnp.float32
BF16 = jnp.bfloat16

EPS = 1e-6
CHUNK = 64
CHUNK_SHIFT = CHUNK.bit_length() - 1
assert CHUNK == 1 << CHUNK_SHIFT
N_HEADS = 8
HEAD_W = 128
MAP_W = 64
ROT_DIM = 16
ROPE_THETA = 500000.0
NEG = -1e30

V7X_LANES = 128
V7X_MXU_DIM = 256
V7X_VMEM_BYTES = 64 * 1024 * 1024
VMEM_LIMIT_BYTES = V7X_VMEM_BYTES - 8 * 1024 * 1024

FFN_TOKEN_TILE = 512
FFN_HIDDEN_TILE = 512
PROJ_TOKEN_TILE = 1024
PAIR_TOKEN_TILE = 512
GATE_TOKEN_TILE = 512
OUT_TOKEN_TILE = 256
DIFF_Q_TILE = 512
SB_Q_TILE = 256
DIFF_K_TILE = 512
SB_K_TILE = V7X_MXU_DIM
DIFF_HEADS_PER_STEP = 4
SB_HEADS_PER_STEP = 8
LOG2E = math.log2(math.e)
F32_MIN_DENORMAL_LOG2 = -149
SB_DEAD_LOG2 = float(-F32_MIN_DENORMAL_LOG2 + 11)


def _tile(n, target):
    t = min(n, target)
    while n % t:
        t -= 1
    return t


def _params(*sem):
    return pltpu.CompilerParams(dimension_semantics=sem, vmem_limit_bytes=VMEM_LIMIT_BYTES)


def _rms(x, g):
    return x * lax.rsqrt(jnp.mean(x * x, axis=-1, keepdims=True) + EPS) * g


def _ffn_kernel(n_f, has_next, x_ref, npre_ref, npost_ref, *refs):
    if has_next:
        nnext_ref, wg_ref, wu_ref, wd_ref, h_ref, xn_ref, xs_ref, acc_ref = refs
    else:
        wg_ref, wu_ref, wd_ref, h_ref, xs_ref, acc_ref = refs
    f = pl.program_id(1)

    @pl.when(f == 0)
    def _():
        xs_ref[...] = _rms(x_ref[...], npre_ref[...]).astype(BF16)
        acc_ref[...] = jnp.zeros_like(acc_ref)

    xs = xs_ref[...]
    g = jnp.dot(xs, wg_ref[...], preferred_element_type=F32)
    u = jnp.dot(xs, wu_ref[...], preferred_element_type=F32)
    a = (g * jax.nn.sigmoid(g) * u).astype(BF16)
    acc_ref[...] += jnp.dot(a, wd_ref[...], preferred_element_type=F32)

    @pl.when(f == n_f - 1)
    def _():
        h = x_ref[...] + 0.5 * _rms(acc_ref[...], npost_ref[...])
        h_ref[...] = h
        if has_next:
            xn_ref[...] = _rms(h, nnext_ref[...]).astype(BF16)


def _ffn(x, n_pre, n_post, n_next, wg, wu, wd):
    n, d = x.shape
    fp = wg.shape[1]
    tm = _tile(n, FFN_TOKEN_TILE)
    tf = _tile(fp, FFN_HIDDEN_TILE)
    n_f = fp // tf
    has_next = n_next is not None
    row = pl.BlockSpec((tm, d), lambda i, f: (i, 0))
    vec = pl.BlockSpec((1, d), lambda i, f: (0, 0))
    in_specs = [row, vec, vec] + ([vec] if has_next else []) + [
        pl.BlockSpec((d, tf), lambda i, f: (0, f)),
        pl.BlockSpec((d, tf), lambda i, f: (0, f)),
        pl.BlockSpec((tf, d), lambda i, f: (f, 0)),
    ]
    out_shape = [jax.ShapeDtypeStruct((n, d), F32)]
    out_specs = [row]
    if has_next:
        out_shape.append(jax.ShapeDtypeStruct((n, d), BF16))
        out_specs.append(row)
    args = [x, n_pre, n_post] + ([n_next] if has_next else []) + [wg, wu, wd]
    res = pl.pallas_call(
        functools.partial(_ffn_kernel, n_f, has_next),
        grid=(n // tm, n_f),
        in_specs=in_specs,
        out_specs=out_specs,
        out_shape=out_shape,
        scratch_shapes=[pltpu.VMEM((tm, d), BF16), pltpu.VMEM((tm, d), F32)],
        compiler_params=_params("parallel", "arbitrary"),
        name="ffn_next" if has_next else "ffn",
    )(*args)
    return res if has_next else res[0]


def _rope(r, c, sa, sb):
    heads = []
    for h in range(r.shape[1] // HEAD_W):
        x = r[:, h * HEAD_W:(h + 1) * HEAD_W]
        up = pltpu.roll(x, HEAD_W - ROT_DIM // 2, 1)
        dn = pltpu.roll(x, ROT_DIM // 2, 1)
        heads.append(x * c + up * sa + dn * sb)
    return jnp.concatenate(heads, axis=1)


PROJ_OUTPUTS = {"q": 1, "q_rope": 1, "gate": 1, "k": 2, "k_rope": 2, "kv": 1, "kv_rope": 1, "v_t": 2}
PROJ_ROTARY = ("q_rope", "k_rope", "kv_rope")


def _proj_epilogue(mode, scale, tk, r, tables, refs):
    if mode in PROJ_ROTARY:
        r = _rope(r, *tables)
    if mode in ("q", "q_rope"):
        (q_ref,) = refs
        q_ref[...] = (r * scale).astype(BF16)
    elif mode == "gate":
        (g_ref,) = refs
        g_ref[...] = jax.nn.sigmoid(r)
    elif mode == "k":
        k32_ref, k16_ref = refs
        k32_ref[...] = r
        k16_ref[...] = r.astype(BF16)
    elif mode == "k_rope":
        kt_ref, k16_ref = refs
        k16_ref[...] = r.astype(BF16)
        for h in range(N_HEADS):
            kt_ref[0, h] = r[:, h * HEAD_W:(h + 1) * HEAD_W].T
    elif mode in ("kv", "kv_rope"):
        (r32_ref,) = refs
        r32_ref[...] = r
    else:
        v32_ref, vt_ref = refs
        v32_ref[...] = r
        for h in range(N_HEADS):
            for jb in range(r.shape[0] // tk):
                blk = r[jb * tk:(jb + 1) * tk, h * HEAD_W:(h + 1) * HEAD_W]
                vt_ref[0, h, jb] = blk.T.astype(BF16)


def _proj_kernel(parts, has_tables, x_ref, *refs):
    w_refs, refs = refs[:len(parts)], refs[len(parts):]
    tables = None
    if has_tables:
        tables, refs = tuple(t[...] for t in refs[:3]), refs[3:]
    x = x_ref[...]
    for (mode, scale, tk), w_ref in zip(parts, w_refs):
        outs, refs = refs[:PROJ_OUTPUTS[mode]], refs[PROJ_OUTPUTS[mode]:]
        _proj_epilogue(mode, scale, tk, jnp.dot(x, w_ref[...], preferred_element_type=F32), tables, outs)


def _proj(xn, parts, *, tables=None, seq=None, tile=PROJ_TOKEN_TILE):
    n, d = xn.shape
    tm = _tile(n if seq is None else seq, tile)
    row = lambda wd: pl.BlockSpec((tm, wd), lambda i: (i, 0))
    has_tables = any(mode in PROJ_ROTARY for _, mode, _, _ in parts)
    in_specs = [row(d)] + [pl.BlockSpec(w.shape, lambda i: (0, 0)) for w, _, _, _ in parts]
    args = [xn] + [w for w, _, _, _ in parts]
    if has_tables:
        in_specs += [row(HEAD_W)] * 3
        args += list(tables)
    out_shape, out_specs = [], []
    for w, mode, _, tk in parts:
        width = w.shape[1]
        if mode in ("q", "q_rope"):
            out_shape += [jax.ShapeDtypeStruct((n, width), BF16)]
            out_specs += [row(width)]
        elif mode in ("gate", "kv", "kv_rope"):
            out_shape += [jax.ShapeDtypeStruct((n, width), F32)]
            out_specs += [row(width)]
        elif mode == "k":
            out_shape += [jax.ShapeDtypeStruct((n, width), F32), jax.ShapeDtypeStruct((n, width), BF16)]
            out_specs += [row(width), row(width)]
        elif mode == "k_rope":
            assert seq % tm == 0
            per_seq = seq // tm
            out_shape += [jax.ShapeDtypeStruct((n // seq, N_HEADS, HEAD_W, seq), F32),
                          jax.ShapeDtypeStruct((n, width), BF16)]
            out_specs += [pl.BlockSpec((1, N_HEADS, HEAD_W, tm),
                                       lambda i, per_seq=per_seq: (i // per_seq, 0, 0, i % per_seq)),
                          row(width)]
        else:
            assert mode == "v_t" and tm % tk == 0 and seq % tm == 0
            per_seq = seq // tm
            out_shape += [jax.ShapeDtypeStruct((n, width), F32),
                          jax.ShapeDtypeStruct((n // seq, N_HEADS, seq // tk, HEAD_W, tk), BF16)]
            out_specs += [row(width),
                          pl.BlockSpec((1, N_HEADS, tm // tk, HEAD_W, tk),
                                       lambda i, per_seq=per_seq: (i // per_seq, 0, i % per_seq, 0, 0))]
    res = pl.pallas_call(
        functools.partial(_proj_kernel, tuple((mode, scale, tk) for _, mode, scale, tk in parts), has_tables),
        grid=(n // tm,),
        in_specs=in_specs,
        out_specs=out_specs,
        out_shape=out_shape,
        compiler_params=_params("parallel"),
        name="proj_" + "_".join(mode for _, mode, _, _ in parts),
    )(*args)
    out, k = [], 0
    for _, mode, _, _ in parts:
        out.append(tuple(res[k:k + PROJ_OUTPUTS[mode]]))
        k += PROJ_OUTPUTS[mode]
    return out


def _kv_range(kind, i, tq, tk, n_k, q_off):
    qmin = q_off + i * tq
    qmax = qmin + tq - 1
    if kind == "diff":
        n_int = ((qmin // CHUNK + 1) * CHUNK) // tk
        last = ((qmax // CHUNK + 1) * CHUNK - 1) // tk
    else:
        n_int = qmin // tk
        last = (qmax - 1) // tk
    smaller = min if isinstance(i, int) else jnp.minimum
    return smaller(n_int, n_k), smaller(last + 1, n_k), qmin


def _split_maps(q_ref, qm_ref, hb):
    lane = lax.broadcasted_iota(jnp.int32, (1, HEAD_W), 1)
    for h in range(hb):
        q = q_ref[0, :, h * HEAD_W:(h + 1) * HEAD_W]
        zero = jnp.zeros_like(q)
        qm_ref[2 * h] = jnp.where(lane < MAP_W, q, zero)
        qm_ref[2 * h + 1] = jnp.where(lane >= MAP_W, q, zero)


def _diff_block(qm_ref, acc_ref, kbs, vtbs, valid, carry):
    carry, probs, alphas = _diff_softmax(_diff_scores(qm_ref, kbs), valid, carry)
    _diff_values(acc_ref, vtbs, probs, alphas)
    return carry


def _diff_scores(qm_ref, kbs):
    return [lax.dot_general(kbs[c // 2], qm_ref[c], (((1,), (1,)), ((), ())), preferred_element_type=F32)
            for c in range(2 * len(kbs))]


def _diff_softmax(scores, valid, carry):
    new, probs, alphas = [], [], []
    for c, s in enumerate(scores):
        m, l = carry[2 * c], carry[2 * c + 1]
        if valid is not None:
            s = jnp.where(valid, s, NEG)
        m_new = jnp.maximum(m, jnp.max(s, axis=0, keepdims=True))
        alpha = jnp.exp2(m - m_new)
        p = jnp.exp2(s - m_new)
        new += [m_new, alpha * l + jnp.sum(p, axis=0, keepdims=True)]
        probs.append(p.astype(BF16))
        alphas.append(alpha)
    return tuple(new), probs, alphas


def _diff_values(acc_ref, vtbs, probs, alphas):
    for c, (p, alpha) in enumerate(zip(probs, alphas)):
        acc_ref[c] = alpha * acc_ref[c] + jnp.dot(vtbs[c // 2], p, preferred_element_type=F32)


def _diff_init(acc_ref, tq, hb):
    acc_ref[...] = jnp.zeros_like(acc_ref)
    return (jnp.full((1, tq), NEG, F32), jnp.zeros((1, tq), F32)) * (2 * hb)


def _diff_finish(carry, acc_ref, lam_refs, g_ref, lam_init, o_ref, hb):
    lq1_ref, lk1_ref, lq2_ref, lk2_ref = lam_refs
    lam = (jnp.exp(jnp.sum(lq1_ref[...] * lk1_ref[...], keepdims=True))
           - jnp.exp(jnp.sum(lq2_ref[...] * lk2_ref[...], keepdims=True)) + lam_init)
    for h in range(hb):
        l0, l1 = carry[4 * h + 1], carry[4 * h + 3]
        ot = acc_ref[2 * h] * (1.0 / l0) - lam * (acc_ref[2 * h + 1] * (1.0 / l1))
        ms = jnp.mean(ot * ot, axis=0, keepdims=True)
        ot = ot * lax.rsqrt(ms + EPS) * g_ref[...] * (1.0 - lam_init)
        o_ref[0, :, h * HEAD_W:(h + 1) * HEAD_W] = ot.T.astype(BF16)


def _diff_attn_kernel(tq, tk, n_k, q_off, lam_init, hb,
                      q_ref, k_ref, vt_ref, lq1_ref, lk1_ref, lq2_ref, lk2_ref, g_ref,
                      o_ref, qm_ref, acc_ref):
    n_int, n_tot, qmin = _kv_range("diff", pl.program_id(2), tq, tk, n_k, q_off)
    _split_maps(q_ref, qm_ref, hb)
    q_chunk = (qmin + lax.broadcasted_iota(jnp.int32, (1, tq), 1)) >> CHUNK_SHIFT

    def step(masked, j, carry):
        k0 = pl.multiple_of(j * tk, tk)
        valid = None
        if masked:
            k_chunk = (k0 + lax.broadcasted_iota(jnp.int32, (tk, 1), 0)) >> CHUNK_SHIFT
            valid = k_chunk <= q_chunk
        kbs = [k_ref[0, pl.ds(k0, tk), h * HEAD_W:(h + 1) * HEAD_W] for h in range(hb)]
        vtbs = [vt_ref[0, h, j] for h in range(hb)]
        return _diff_block(qm_ref, acc_ref, kbs, vtbs, valid, carry)

    carry = _diff_init(acc_ref, tq, hb)
    carry = lax.fori_loop(0, n_int, functools.partial(step, False), carry)
    carry = lax.fori_loop(n_int, n_tot, functools.partial(step, True), carry)
    _diff_finish(carry, acc_ref, (lq1_ref, lk1_ref, lq2_ref, lk2_ref), g_ref, lam_init, o_ref, hb)


def _suffix_matrix(tk):
    s = (lax.broadcasted_iota(jnp.int32, (tk, tk), 1)
         >= lax.broadcasted_iota(jnp.int32, (tk, tk), 0)).astype(BF16)
    return jnp.concatenate([s, s], axis=1)


def _sb_block(q_ref, acc_ref, suffix2, kbs, vtbs, valid, later):
    heads = range(len(kbs))
    ws = [lax.dot_general(kbs[h], q_ref[0, :, h * HEAD_W:(h + 1) * HEAD_W], (((1,), (1,)), ((), ())),
                          preferred_element_type=F32) for h in heads]
    tails = []
    for h in heads:
        sp = jnp.maximum(ws[h], 0.0) + jnp.log(1.0 + jnp.exp2(-jnp.abs(ws[h]))) * LOG2E
        if valid is not None:
            sp = jnp.where(valid, sp, 0.0)
        hi = sp.astype(BF16)
        lo = (sp - hi.astype(F32)).astype(BF16)
        tails.append(jnp.dot(suffix2, jnp.concatenate([hi, lo], axis=0),
                             preferred_element_type=F32))
    weights = []
    for h in heads:
        logit = ws[h] - tails[h] - later[h]
        if valid is not None:
            logit = jnp.where(valid, logit, NEG)
        weights.append(jnp.exp2(logit).astype(BF16))
    for h in heads:
        acc_ref[h] += jnp.dot(vtbs[h], weights[h], preferred_element_type=F32)
    return tuple(later[h] + tails[h][0:1, :] for h in heads)


def _sb_walk(step, first, last, later):
    def alive(later):
        return (jnp.min(functools.reduce(jnp.minimum, later)) < SB_DEAD_LOG2).astype(jnp.int32)

    def cond(state):
        return jnp.logical_and(state[0] < last, state[1] > 0)

    def body(state):
        later = step(state[0], state[2])
        return state[0] + 1, alive(later), later

    return lax.while_loop(cond, body, (jnp.asarray(first, jnp.int32), alive(later), later))[2]


def _sb_finish(acc_ref, o_ref, hb):
    for h in range(hb):
        o_ref[0, :, h * HEAD_W:(h + 1) * HEAD_W] = acc_ref[h].T.astype(BF16)


def _sb_attn_kernel(tq, tk, n_k, q_off, hb, q_ref, k_ref, vt_ref, o_ref, acc_ref):
    n_int, n_tot, qmin = _kv_range("sb", pl.program_id(2), tq, tk, n_k, q_off)
    q_pos = qmin + lax.broadcasted_iota(jnp.int32, (1, tq), 1)
    suffix2 = _suffix_matrix(tk)
    acc_ref[...] = jnp.zeros_like(acc_ref)

    def step(masked, t, later):
        j = n_tot - 1 - t
        k0 = pl.multiple_of(j * tk, tk)
        valid = (k0 + lax.broadcasted_iota(jnp.int32, (tk, 1), 0)) < q_pos if masked else None
        kbs = [k_ref[0, pl.ds(k0, tk), h * HEAD_W:(h + 1) * HEAD_W] for h in range(hb)]
        vtbs = [vt_ref[0, h, j] for h in range(hb)]
        return _sb_block(q_ref, acc_ref, suffix2, kbs, vtbs, valid, later)

    later = (jnp.zeros((1, tq), F32),) * hb
    later = lax.fori_loop(0, n_tot - n_int, functools.partial(step, True), later)
    _sb_walk(functools.partial(step, False), n_tot - n_int, n_tot, later)
    _sb_finish(acc_ref, o_ref, hb)


def _attention(kind, q, k, vt, q_off, extra, cfg):
    b, t_q, width = q.shape
    n_k, tk = vt.shape[2], vt.shape[4]
    t_k = k.shape[1]
    assert t_k == n_k * tk
    tq = _tile(t_q, DIFF_Q_TILE if kind == "diff" else SB_Q_TILE)
    hb = DIFF_HEADS_PER_STEP if kind == "diff" else SB_HEADS_PER_STEP
    kv_mode = dict(pipeline_mode=pl.Buffered(1)) if hb == N_HEADS else {}
    q_spec = pl.BlockSpec((1, tq, hb * HEAD_W), lambda bi, h, i: (bi, i, h))
    in_specs = [q_spec,
                pl.BlockSpec((1, t_k, hb * HEAD_W), lambda bi, h, i: (bi, 0, h), **kv_mode),
                pl.BlockSpec((1, hb, n_k, HEAD_W, tk), lambda bi, h, i: (bi, h, 0, 0, 0), **kv_mode)]
    for e in extra:
        in_specs.append(pl.BlockSpec(e.shape, lambda bi, h, i: (0, 0)))
    if kind == "diff":
        body = functools.partial(_diff_attn_kernel, tq, tk, n_k, q_off, cfg, hb)
        scratch = [pltpu.VMEM((2 * hb, tq, HEAD_W), BF16), pltpu.VMEM((2 * hb, HEAD_W, tq), F32)]
    else:
        body = functools.partial(_sb_attn_kernel, tq, tk, n_k, q_off, hb)
        scratch = [pltpu.VMEM((hb, HEAD_W, tq), F32)]
    return pl.pallas_call(
        body,
        grid=(b, N_HEADS // hb, t_q // tq),
        in_specs=in_specs,
        out_specs=q_spec,
        out_shape=jax.ShapeDtypeStruct((b, t_q, width), BF16),
        scratch_shapes=scratch,
        compiler_params=_params("parallel", "parallel", "arbitrary"),
        name=kind + "_attn",
    )(q, k, vt, *extra)


def _past_operands(kp_ref, vp_ref, k0, tk):
    heads = range(N_HEADS)
    rows = lambda h: pl.ds(k0 * N_HEADS + h, tk, stride=N_HEADS)
    kbs = [kp_ref[0, rows(h), :].astype(BF16) for h in heads]
    vtbs = [vp_ref[0, rows(h), :].T.astype(BF16) for h in heads]
    return kbs, vtbs


def _new_operands(kn_ref, vn_ref):
    heads = range(N_HEADS)
    kbs = [kn_ref[0, :, h * HEAD_W:(h + 1) * HEAD_W].astype(BF16) for h in heads]
    vtbs = [vn_ref[0, :, h * HEAD_W:(h + 1) * HEAD_W].T.astype(BF16) for h in heads]
    return kbs, vtbs


def _new_positions(q_off, t_new, tq):
    k_idx = lax.broadcasted_iota(jnp.int32, (t_new, 1), 0)
    q_idx = lax.broadcasted_iota(jnp.int32, (1, tq), 1)
    return k_idx, q_off + k_idx, q_off + q_idx


def _diff_cached_kernel(tq, tk, n_past, q_off, t_real, lam_init,
                        q_ref, kp_ref, vp_ref, kn_ref, vn_ref, lq1_ref, lk1_ref, lq2_ref, lk2_ref, g_ref,
                        o_ref, qm_ref, acc_ref):
    _split_maps(q_ref, qm_ref, N_HEADS)
    carry = _diff_init(acc_ref, tq, N_HEADS)
    heads = range(N_HEADS)
    for j in range(n_past):
        kbs = [kp_ref[0, h, :, j * tk:(j + 1) * tk].T.astype(BF16) for h in heads]
        vtbs = [vp_ref[0, pl.ds(j * tk * N_HEADS + h, tk, stride=N_HEADS), :].T.astype(BF16) for h in heads]
        carry = _diff_block(qm_ref, acc_ref, kbs, vtbs, None, carry)
    k_idx, k_pos, q_pos = _new_positions(q_off, kn_ref.shape[1], tq)
    valid = ((k_pos >> CHUNK_SHIFT) <= (q_pos >> CHUNK_SHIFT)) & (k_idx < t_real)
    kbs, vtbs = _new_operands(kn_ref, vn_ref)
    carry = _diff_block(qm_ref, acc_ref, kbs, vtbs, valid, carry)
    _diff_finish(carry, acc_ref, (lq1_ref, lk1_ref, lq2_ref, lk2_ref), g_ref, lam_init, o_ref, N_HEADS)


def _sb_cached_kernel(tq, tk, n_past, q_off, t_real, q_ref, kp_ref, vp_ref, kn_ref, vn_ref, o_ref, acc_ref):
    acc_ref[...] = jnp.zeros_like(acc_ref)
    t_new = kn_ref.shape[1]
    k_idx, k_pos, q_pos = _new_positions(q_off, t_new, tq)
    valid = (k_pos < q_pos) & (k_idx < t_real)
    kbs, vtbs = _new_operands(kn_ref, vn_ref)
    later = (jnp.zeros((1, tq), F32),) * N_HEADS
    later = _sb_block(q_ref, acc_ref, _suffix_matrix(t_new), kbs, vtbs, valid, later)
    suffix2 = _suffix_matrix(tk)

    def past(t, later):
        kbs, vtbs = _past_operands(kp_ref, vp_ref, pl.multiple_of((n_past - 1 - t) * tk, tk), tk)
        return _sb_block(q_ref, acc_ref, suffix2, kbs, vtbs, None, later)

    _sb_walk(past, 0, n_past, later)
    _sb_finish(acc_ref, o_ref, N_HEADS)


def _attention_cached(kind, q, k_past, v_past, k_new, v_new, t_real, extra, lam_init):
    b, tq, width = q.shape
    p_len, t_new = v_past.shape[1] // N_HEADS, k_new.shape[1]
    tk = _tile(p_len, SB_K_TILE)
    n_past = p_len // tk
    row3 = lambda t: pl.BlockSpec((1, t, width), lambda bi: (bi, 0, 0))
    past_spec = pl.BlockSpec((1, p_len * N_HEADS, HEAD_W), lambda bi: (bi, 0, 0))
    k_spec = pl.BlockSpec((1, N_HEADS, HEAD_W, p_len), lambda bi: (bi, 0, 0, 0)) if kind == "diff" else past_spec
    in_specs = [row3(tq), k_spec, past_spec, row3(t_new), row3(t_new)]
    for e in extra:
        in_specs.append(pl.BlockSpec(e.shape, lambda bi: (0, 0)))
    if kind == "diff":
        body = functools.partial(_diff_cached_kernel, tq, tk, n_past, p_len, t_real, lam_init)
        scratch = [pltpu.VMEM((2 * N_HEADS, tq, HEAD_W), BF16), pltpu.VMEM((2 * N_HEADS, HEAD_W, tq), F32)]
    else:
        body = functools.partial(_sb_cached_kernel, tq, tk, n_past, p_len, t_real)
        scratch = [pltpu.VMEM((N_HEADS, HEAD_W, tq), F32)]
    return pl.pallas_call(
        body,
        grid=(b,),
        in_specs=in_specs,
        out_specs=row3(tq),
        out_shape=jax.ShapeDtypeStruct((b, tq, width), BF16),
        scratch_shapes=scratch,
        compiler_params=_params("parallel"),
        name=kind + "_attn_cached",
    )(q, k_past, v_past, k_new, v_new, *extra)


def _out_kernel(oa_ref, ob_ref, ga_ref, gb_ref, h_ref, wua_ref, wub_ref, wo_ref, n_ref, o_ref):
    pa = jnp.dot(oa_ref[...], wua_ref[...], preferred_element_type=F32)
    pb = jnp.dot(ob_ref[...], wub_ref[...], preferred_element_type=F32)
    merged = (ga_ref[...] * pa + gb_ref[...] * pb).astype(BF16)
    r = jnp.dot(merged, wo_ref[...], preferred_element_type=F32)
    o_ref[...] = h_ref[...] + _rms(r, n_ref[...])


def _merge_out(oa, ob, ga, gb, h, wua, wub, wo, n_post):
    n, d = h.shape
    w_attn = oa.shape[1]
    tm = _tile(n, OUT_TOKEN_TILE)
    row = lambda wd: pl.BlockSpec((tm, wd), lambda i: (i, 0))
    const = lambda shape: pl.BlockSpec(shape, lambda i: (0, 0), pipeline_mode=pl.Buffered(1))
    return pl.pallas_call(
        _out_kernel,
        grid=(n // tm,),
        in_specs=[row(w_attn), row(w_attn), row(d), row(d), row(d),
                  const(wua.shape), const(wub.shape), const(wo.shape), const(n_post.shape)],
        out_specs=row(d),
        out_shape=jax.ShapeDtypeStruct((n, d), F32),
        compiler_params=_params("parallel"),
        name="merge_out",
    )(oa, ob, ga, gb, h, wua, wub, wo, n_post)


def _rope_tables(pos):
    half = ROT_DIM // 2
    inv_freq = jnp.power(ROPE_THETA, -jnp.arange(0, ROT_DIM, 2, dtype=F32) / ROT_DIM)
    ang = pos.astype(F32)[:, None] * inv_freq[None, :]
    cos, sin = jnp.cos(ang), jnp.sin(ang)
    t = pos.shape[0]
    ones = jnp.ones((t, MAP_W - ROT_DIM), F32)
    zeros = jnp.zeros((t, MAP_W - ROT_DIM), F32)
    zh = jnp.zeros((t, half), F32)
    c = jnp.concatenate([cos, cos, ones], axis=1)
    sa = jnp.concatenate([-sin, zh, zeros], axis=1)
    sb = jnp.concatenate([zh, sin, zeros], axis=1)
    return tuple(jnp.concatenate([m, m], axis=1) for m in (c, sa, sb))


def _pad_ffn(w, axis):
    f = w.shape[axis]
    fp = -(-f // FFN_HIDDEN_TILE) * FFN_HIDDEN_TILE
    shape = list(w.shape)
    shape[axis] = fp - f
    return jnp.concatenate([w.astype(BF16), jnp.zeros(shape, BF16)], axis=axis)


def _layer(x, pos, past, lam_init, p):
    b, t, d = x.shape
    n = b * t
    w_a = N_HEADS * HEAD_W
    vec = lambda v: v.reshape(1, -1)
    w_in = p["w_in"].astype(BF16)
    seg = lambda lo, hi: w_in[:, lo:hi]

    h1, xn = _ffn(x.reshape(n, d), vec(p["n1a"]), vec(p["n1b"]), vec(p["nma"]),
                  _pad_ffn(p["f1g"], 1), _pad_ffn(p["f1u"], 1), _pad_ffn(p["f1d"], 0))

    tables = tuple(jnp.tile(m, (b, 1)) for m in _rope_tables(pos))
    w_qa, w_ka, w_va, w_qb, w_kb, w_vb = (seg(k * w_a, (k + 1) * w_a) for k in range(6))
    qa_part = (w_qa, "q_rope", MAP_W ** -0.5 * LOG2E, None)
    qb_part = (w_qb, "q", HEAD_W ** -0.5 * LOG2E, None)
    (g_a,), = _proj(xn, [(seg(6 * w_a, 6 * w_a + d), "gate", None, None)], tile=GATE_TOKEN_TILE)
    (g_b,), = _proj(xn, [(seg(6 * w_a + d, 6 * w_a + 2 * d), "gate", None, None)], tile=GATE_TOKEN_TILE)

    shape3 = lambda a: a.reshape(b, t, w_a)
    diff_extra = [vec(p["lq1"]), vec(p["lk1"]), vec(p["lq2"]), vec(p["lk2"]), p["subln_g"].reshape(-1, 1)]
    if past is None:
        (q_a,), (kat, ka16) = _proj(xn, [qa_part, (w_ka, "k_rope", None, None)], tables=tables, seq=t,
                                    tile=PAIR_TOKEN_TILE)
        (q_b,), (kb32, kb16) = _proj(xn, [qb_part, (w_kb, "k", None, None)], tile=PAIR_TOKEN_TILE)
        (va32, vat), (vb32, vbt) = _proj(xn, [(w_va, "v_t", None, _tile(t, DIFF_K_TILE)),
                                              (w_vb, "v_t", None, _tile(t, SB_K_TILE))],
                                         seq=t, tile=PAIR_TOKEN_TILE)
        k_a_rows = kat.reshape(b, N_HEADS, 2, MAP_W, t).transpose(0, 4, 1, 2, 3)
        o_a = _attention("diff", shape3(q_a), shape3(ka16), vat, 0, diff_extra, lam_init)
        o_b = _attention("sb", shape3(q_b), shape3(kb16), vbt, 0, [], None)
    else:
        (q_a,), (ka32,) = _proj(xn, [qa_part, (w_ka, "kv_rope", None, None)], tables=tables,
                                tile=PAIR_TOKEN_TILE)
        (q_b,), (kb32,) = _proj(xn, [qb_part, (w_kb, "kv", None, None)], tile=PAIR_TOKEN_TILE)
        (va32,), (vb32,) = _proj(xn, [(w_va, "kv", None, None), (w_vb, "kv", None, None)],
                                 tile=PAIR_TOKEN_TILE)
        k_a_rows = ka32.reshape(b, t, N_HEADS, 2, MAP_W)
        pka, pva, pkb, pvb = past
        t_p = -(-t // V7X_LANES) * V7X_LANES
        pad_rows = lambda a: jnp.pad(shape3(a), ((0, 0), (0, t_p - t), (0, 0)))
        by_head = lambda c: c.reshape(c.shape[0], c.shape[1] * N_HEADS, HEAD_W)
        pka_t = pka.transpose(0, 2, 3, 4, 1).reshape(pka.shape[0], N_HEADS, HEAD_W, pka.shape[1])
        o_a = _attention_cached("diff", pad_rows(q_a), pka_t, by_head(pva), pad_rows(ka32),
                                pad_rows(va32), t, diff_extra, lam_init)
        o_b = _attention_cached("sb", pad_rows(q_b), by_head(pkb), by_head(pvb), pad_rows(kb32),
                                pad_rows(vb32), t, [], None)
    o_a = o_a[:, :t].reshape(n, w_a)
    o_b = o_b[:, :t].reshape(n, w_a)

    h2 = _merge_out(o_a, o_b, g_a, g_b, h1, p["w_up_a"].astype(BF16), p["w_up_b"].astype(BF16),
                    p["w_o"].astype(BF16), vec(p["nmb"]))
    y = _ffn(h2, vec(p["n2a"]), vec(p["n2b"]), None,
             _pad_ffn(p["f2g"], 1), _pad_ffn(p["f2u"], 1), _pad_ffn(p["f2d"], 0))
    rows = (k_a_rows, va32.reshape(b, t, N_HEADS, HEAD_W),
            kb32.reshape(b, t, N_HEADS, HEAD_W), vb32.reshape(b, t, N_HEADS, HEAD_W))
    return y.reshape(b, t, d), rows


def _stack(xs):
    return xs[0][None] if len(xs) == 1 else jnp.stack(xs)


def kernel(x_prompt, x_sample, cache_diff_k, cache_diff_v, cache_sb_k, cache_sb_v, w_in, w_up_a, w_up_b, w_o, lam_q1, lam_k1, lam_q2, lam_k2, subln_g, norm_ffn1_pre, norm_ffn1_post, norm_mix_pre, norm_mix_post, norm_ffn2_pre, norm_ffn2_post, ffn1_w_gate, ffn1_w_up, ffn1_w_down, ffn2_w_gate, ffn2_w_up, ffn2_w_down):
    depth = w_in.shape[0]
    pos_p = jnp.arange(x_prompt.shape[1], dtype=jnp.int32)
    pos_s = cache_diff_k.shape[2] + jnp.arange(x_sample.shape[1], dtype=jnp.int32)
    hp, hs = x_prompt, x_sample
    rows_p, rows_s = [], []
    for l in range(depth):
        lam_init = 0.8 - 0.6 * math.exp(-0.3 * l)
        p = dict(w_in=w_in[l], w_up_a=w_up_a[l], w_up_b=w_up_b[l], w_o=w_o[l],
                 lq1=lam_q1[l], lk1=lam_k1[l], lq2=lam_q2[l], lk2=lam_k2[l], subln_g=subln_g[l],
                 n1a=norm_ffn1_pre[l], n1b=norm_ffn1_post[l], nma=norm_mix_pre[l], nmb=norm_mix_post[l],
                 n2a=norm_ffn2_pre[l], n2b=norm_ffn2_post[l],
                 f1g=ffn1_w_gate[l], f1u=ffn1_w_up[l], f1d=ffn1_w_down[l],
                 f2g=ffn2_w_gate[l], f2u=ffn2_w_up[l], f2d=ffn2_w_down[l])
        hp, rp = _layer(hp, pos_p, None, lam_init, p)
        past = (cache_diff_k[l], cache_diff_v[l], cache_sb_k[l], cache_sb_v[l])
        hs, rs = _layer(hs, pos_s, past, lam_init, p)
        rows_p.append(rp)
        rows_s.append(rs)
    outs_p = [_stack([r[i] for r in rows_p]) for i in range(4)]
    outs_s = [_stack([r[i] for r in rows_s]) for i in range(4)]
    return (hp, hs, *outs_p, *outs_s)
```

```python
import functools
import math

import jax
import jax.numpy as jnp
from jax import lax
from jax.experimental import pallas as pl
from jax.experimental.pallas import tpu as pltpu

F32 = jnp.float32
BF16 = jnp.bfloat16

EPS = 1e-6
CHUNK = 64
CHUNK_SHIFT = CHUNK.bit_length() - 1
assert CHUNK == 1 << CHUNK_SHIFT
N_HEADS = 8
HEAD_W = 128
MAP_W = 64
ROT_DIM = 16
ROPE_THETA = 500000.0
NEG = -1e30

V7X_LANES = 128
V7X_MXU_DIM = 256
V7X_VMEM_BYTES = 64 * 1024 * 1024
VMEM_LIMIT_BYTES = V7X_VMEM_BYTES - 8 * 1024 * 1024

FFN_TOKEN_TILE = 512
FFN_HIDDEN_TILE = 512
PROJ_TOKEN_TILE = 1024
PAIR_TOKEN_TILE = 512
GATE_TOKEN_TILE = 512
OUT_TOKEN_TILE = 256
DIFF_Q_TILE = 512
SB_Q_TILE = 256
DIFF_K_TILE = 512
SB_K_TILE = V7X_MXU_DIM
DIFF_HEADS_PER_STEP = 4
SB_HEADS_PER_STEP = 8
LOG2E = math.log2(math.e)
F32_MIN_DENORMAL_LOG2 = -149
SB_DEAD_LOG2 = float(-F32_MIN_DENORMAL_LOG2 + 11)


def _tile(n, target):
    t = min(n, target)
    while n % t:
        t -= 1
    return t


def _params(*sem):
    return pltpu.CompilerParams(dimension_semantics=sem, vmem_limit_bytes=VMEM_LIMIT_BYTES)


def _rms(x, g):
    return x * lax.rsqrt(jnp.mean(x * x, axis=-1, keepdims=True) + EPS) * g


def _ffn_kernel(n_f, has_next, x_ref, npre_ref, npost_ref, *refs):
    if has_next:
        nnext_ref, wg_ref, wu_ref, wd_ref, h_ref, xn_ref, xs_ref, acc_ref = refs
    else:
        wg_ref, wu_ref, wd_ref, h_ref, xs_ref, acc_ref = refs
    f = pl.program_id(1)

    @pl.when(f == 0)
    def _():
        xs_ref[...] = _rms(x_ref[...], npre_ref[...]).astype(BF16)
        acc_ref[...] = jnp.zeros_like(acc_ref)

    xs = xs_ref[...]
    g = jnp.dot(xs, wg_ref[...], preferred_element_type=F32)
    u = jnp.dot(xs, wu_ref[...], preferred_element_type=F32)
    a = (g * jax.nn.sigmoid(g) * u).astype(BF16)
    acc_ref[...] += jnp.dot(a, wd_ref[...], preferred_element_type=F32)

    @pl.when(f == n_f - 1)
    def _():
        h = x_ref[...] + 0.5 * _rms(acc_ref[...], npost_ref[...])
        h_ref[...] = h
        if has_next:
            xn_ref[...] = _rms(h, nnext_ref[...]).astype(BF16)


def _ffn(x, n_pre, n_post, n_next, wg, wu, wd):
    n, d = x.shape
    fp = wg.shape[1]
    tm = _tile(n, FFN_TOKEN_TILE)
    tf = _tile(fp, FFN_HIDDEN_TILE)
    n_f = fp // tf
    has_next = n_next is not None
    row = pl.BlockSpec((tm, d), lambda i, f: (i, 0))
    vec = pl.BlockSpec((1, d), lambda i, f: (0, 0))
    in_specs = [row, vec, vec] + ([vec] if has_next else []) + [
        pl.BlockSpec((d, tf), lambda i, f: (0, f)),
        pl.BlockSpec((d, tf), lambda i, f: (0, f)),
        pl.BlockSpec((tf, d), lambda i, f: (f, 0)),
    ]
    out_shape = [jax.ShapeDtypeStruct((n, d), F32)]
    out_specs = [row]
    if has_next:
        out_shape.append(jax.ShapeDtypeStruct((n, d), BF16))
        out_specs.append(row)
    args = [x, n_pre, n_post] + ([n_next] if has_next else []) + [wg, wu, wd]
    res = pl.pallas_call(
        functools.partial(_ffn_kernel, n_f, has_next),
        grid=(n // tm, n_f),
        in_specs=in_specs,
        out_specs=out_specs,
        out_shape=out_shape,
        scratch_shapes=[pltpu.VMEM((tm, d), BF16), pltpu.VMEM((tm, d), F32)],
        compiler_params=_params("parallel", "arbitrary"),
        name="ffn_next" if has_next else "ffn",
    )(*args)
    return res if has_next else res[0]


def _rope(r, c, sa, sb):
    heads = []
    for h in range(r.shape[1] // HEAD_W):
        x = r[:, h * HEAD_W:(h + 1) * HEAD_W]
        up = pltpu.roll(x, HEAD_W - ROT_DIM // 2, 1)
        dn = pltpu.roll(x, ROT_DIM // 2, 1)
        heads.append(x * c + up * sa + dn * sb)
    return jnp.concatenate(heads, axis=1)


PROJ_OUTPUTS = {"q": 1, "q_rope": 1, "gate": 1, "k": 2, "k_rope": 2, "kv": 1, "kv_rope": 1, "v_t": 2}
PROJ_ROTARY = ("q_rope", "k_rope", "kv_rope")


def _proj_epilogue(mode, scale, tk, r, tables, refs):
    if mode in PROJ_ROTARY:
        r = _rope(r, *tables)
    if mode in ("q", "q_rope"):
        (q_ref,) = refs
        q_ref[...] = (r * scale).astype(BF16)
    elif mode == "gate":
        (g_ref,) = refs
        g_ref[...] = jax.nn.sigmoid(r)
    elif mode == "k":
        k32_ref, k16_ref = refs
        k32_ref[...] = r
        k16_ref[...] = r.astype(BF16)
    elif mode == "k_rope":
        kt_ref, k16_ref = refs
        k16_ref[...] = r.astype(BF16)
        for h in range(N_HEADS):
            kt_ref[0, h] = r[:, h * HEAD_W:(h + 1) * HEAD_W].T
    elif mode in ("kv", "kv_rope"):
        (r32_ref,) = refs
        r32_ref[...] = r
    else:
        v32_ref, vt_ref = refs
        v32_ref[...] = r
        for h in range(N_HEADS):
            for jb in range(r.shape[0] // tk):
                blk = r[jb * tk:(jb + 1) * tk, h * HEAD_W:(h + 1) * HEAD_W]
                vt_ref[0, h, jb] = blk.T.astype(BF16)


def _proj_kernel(parts, has_tables, x_ref, *refs):
    w_refs, refs = refs[:len(parts)], refs[len(parts):]
    tables = None
    if has_tables:
        tables, refs = tuple(t[...] for t in refs[:3]), refs[3:]
    x = x_ref[...]
    for (mode, scale, tk), w_ref in zip(parts, w_refs):
        outs, refs = refs[:PROJ_OUTPUTS[mode]], refs[PROJ_OUTPUTS[mode]:]
        _proj_epilogue(mode, scale, tk, jnp.dot(x, w_ref[...], preferred_element_type=F32), tables, outs)


def _proj(xn, parts, *, tables=None, seq=None, tile=PROJ_TOKEN_TILE):
    n, d = xn.shape
    tm = _tile(n if seq is None else seq, tile)
    row = lambda wd: pl.BlockSpec((tm, wd), lambda i: (i, 0))
    has_tables = any(mode in PROJ_ROTARY for _, mode, _, _ in parts)
    in_specs = [row(d)] + [pl.BlockSpec(w.shape, lambda i: (0, 0)) for w, _, _, _ in parts]
    args = [xn] + [w for w, _, _, _ in parts]
    if has_tables:
        in_specs += [row(HEAD_W)] * 3
        args += list(tables)
    out_shape, out_specs = [], []
    for w, mode, _, tk in parts:
        width = w.shape[1]
        if mode in ("q", "q_rope"):
            out_shape += [jax.ShapeDtypeStruct((n, width), BF16)]
            out_specs += [row(width)]
        elif mode in ("gate", "kv", "kv_rope"):
            out_shape += [jax.ShapeDtypeStruct((n, width), F32)]
            out_specs += [row(width)]
        elif mode == "k":
            out_shape += [jax.ShapeDtypeStruct((n, width), F32), jax.ShapeDtypeStruct((n, width), BF16)]
            out_specs += [row(width), row(width)]
        elif mode == "k_rope":
            assert seq % tm == 0
            per_seq = seq // tm
            out_shape += [jax.ShapeDtypeStruct((n // seq, N_HEADS, HEAD_W, seq), F32),
                          jax.ShapeDtypeStruct((n, width), BF16)]
            out_specs += [pl.BlockSpec((1, N_HEADS, HEAD_W, tm),
                                       lambda i, per_seq=per_seq: (i // per_seq, 0, 0, i % per_seq)),
                          row(width)]
        else:
            assert mode == "v_t" and tm % tk == 0 and seq % tm == 0
            per_seq = seq // tm
            out_shape += [jax.ShapeDtypeStruct((n, width), F32),
                          jax.ShapeDtypeStruct((n // seq, N_HEADS, seq // tk, HEAD_W, tk), BF16)]
            out_specs += [row(width),
                          pl.BlockSpec((1, N_HEADS, tm // tk, HEAD_W, tk),
                                       lambda i, per_seq=per_seq: (i // per_seq, 0, i % per_seq, 0, 0))]
    res = pl.pallas_call(
        functools.partial(_proj_kernel, tuple((mode, scale, tk) for _, mode, scale, tk in parts), has_tables),
        grid=(n // tm,),
        in_specs=in_specs,
        out_specs=out_specs,
        out_shape=out_shape,
        compiler_params=_params("parallel"),
        name="proj_" + "_".join(mode for _, mode, _, _ in parts),
    )(*args)
    out, k = [], 0
    for _, mode, _, _ in parts:
        out.append(tuple(res[k:k + PROJ_OUTPUTS[mode]]))
        k += PROJ_OUTPUTS[mode]
    return out


def _kv_range(kind, i, tq, tk, n_k, q_off):
    qmin = q_off + i * tq
    qmax = qmin + tq - 1
    if kind == "diff":
        n_int = ((qmin // CHUNK + 1) * CHUNK) // tk
        last = ((qmax // CHUNK + 1) * CHUNK - 1) // tk
    else:
        n_int = qmin // tk
        last = (qmax - 1) // tk
    smaller = min if isinstance(i, int) else jnp.minimum
    return smaller(n_int, n_k), smaller(last + 1, n_k), qmin


def _split_maps(q_ref, qm_ref, hb):
    lane = lax.broadcasted_iota(jnp.int32, (1, HEAD_W), 1)
    for h in range(hb):
        q = q_ref[0, :, h * HEAD_W:(h + 1) * HEAD_W]
        zero = jnp.zeros_like(q)
        qm_ref[2 * h] = jnp.where(lane < MAP_W, q, zero)
        qm_ref[2 * h + 1] = jnp.where(lane >= MAP_W, q, zero)


def _diff_block(qm_ref, acc_ref, kbs, vtbs, valid, carry):
    carry, probs, alphas = _diff_softmax(_diff_scores(qm_ref, kbs), valid, carry)
    _diff_values(acc_ref, vtbs, probs, alphas)
    return carry


def _diff_scores(qm_ref, kbs):
    return [lax.dot_general(kbs[c // 2], qm_ref[c], (((1,), (1,)), ((), ())), preferred_element_type=F32)
            for c in range(2 * len(kbs))]


def _diff_softmax(scores, valid, carry):
    new, probs, alphas = [], [], []
    for c, s in enumerate(scores):
        m, l = carry[2 * c], carry[2 * c + 1]
        if valid is not None:
            s = jnp.where(valid, s, NEG)
        m_new = jnp.maximum(m, jnp.max(s, axis=0, keepdims=True))
        alpha = jnp.exp2(m - m_new)
        p = jnp.exp2(s - m_new)
        new += [m_new, alpha * l + jnp.sum(p, axis=0, keepdims=True)]
        probs.append(p.astype(BF16))
        alphas.append(alpha)
    return tuple(new), probs, alphas


def _diff_values(acc_ref, vtbs, probs, alphas):
    for c, (p, alpha) in enumerate(zip(probs, alphas)):
        acc_ref[c] = alpha * acc_ref[c] + jnp.dot(vtbs[c // 2], p, preferred_element_type=F32)


def _diff_init(acc_ref, tq, hb):
    acc_ref[...] = jnp.zeros_like(acc_ref)
    return (jnp.full((1, tq), NEG, F32), jnp.zeros((1, tq), F32)) * (2 * hb)


def _diff_finish(carry, acc_ref, lam_refs, g_ref, lam_init, o_ref, hb):
    lq1_ref, lk1_ref, lq2_ref, lk2_ref = lam_refs
    lam = (jnp.exp(jnp.sum(lq1_ref[...] * lk1_ref[...], keepdims=True))
           - jnp.exp(jnp.sum(lq2_ref[...] * lk2_ref[...], keepdims=True)) + lam_init)
    for h in range(hb):
        l0, l1 = carry[4 * h + 1], carry[4 * h + 3]
        ot = acc_ref[2 * h] * (1.0 / l0) - lam * (acc_ref[2 * h + 1] * (1.0 / l1))
        ms = jnp.mean(ot * ot, axis=0, keepdims=True)
        ot = ot * lax.rsqrt(ms + EPS) * g_ref[...] * (1.0 - lam_init)
        o_ref[0, :, h * HEAD_W:(h + 1) * HEAD_W] = ot.T.astype(BF16)


def _diff_attn_kernel(tq, tk, n_k, q_off, lam_init, hb,
                      q_ref, k_ref, vt_ref, lq1_ref, lk1_ref, lq2_ref, lk2_ref, g_ref,
                      o_ref, qm_ref, acc_ref):
    n_int, n_tot, qmin = _kv_range("diff", pl.program_id(2), tq, tk, n_k, q_off)
    _split_maps(q_ref, qm_ref, hb)
    q_chunk = (qmin + lax.broadcasted_iota(jnp.int32, (1, tq), 1)) >> CHUNK_SHIFT

    def step(masked, j, carry):
        k0 = pl.multiple_of(j * tk, tk)
        valid = None
        if masked:
            k_chunk = (k0 + lax.broadcasted_iota(jnp.int32, (tk, 1), 0)) >> CHUNK_SHIFT
            valid = k_chunk <= q_chunk
        kbs = [k_ref[0, pl.ds(k0, tk), h * HEAD_W:(h + 1) * HEAD_W] for h in range(hb)]
        vtbs = [vt_ref[0, h, j] for h in range(hb)]
        return _diff_block(qm_ref, acc_ref, kbs, vtbs, valid, carry)

    carry = _diff_init(acc_ref, tq, hb)
    carry = lax.fori_loop(0, n_int, functools.partial(step, False), carry)
    carry = lax.fori_loop(n_int, n_tot, functools.partial(step, True), carry)
    _diff_finish(carry, acc_ref, (lq1_ref, lk1_ref, lq2_ref, lk2_ref), g_ref, lam_init, o_ref, hb)


def _suffix_matrix(tk):
    s = (lax.broadcasted_iota(jnp.int32, (tk, tk), 1)
         >= lax.broadcasted_iota(jnp.int32, (tk, tk), 0)).astype(BF16)
    return jnp.concatenate([s, s], axis=1)


def _sb_block(q_ref, acc_ref, suffix2, kbs, vtbs, valid, later):
    heads = range(len(kbs))
    ws = [lax.dot_general(kbs[h], q_ref[0, :, h * HEAD_W:(h + 1) * HEAD_W], (((1,), (1,)), ((), ())),
                          preferred_element_type=F32) for h in heads]
    tails = []
    for h in heads:
        sp = jnp.maximum(ws[h], 0.0) + jnp.log(1.0 + jnp.exp2(-jnp.abs(ws[h]))) * LOG2E
        if valid is not None:
            sp = jnp.where(valid, sp, 0.0)
        hi = sp.astype(BF16)
        lo = (sp - hi.astype(F32)).astype(BF16)
        tails.append(jnp.dot(suffix2, jnp.concatenate([hi, lo], axis=0),
                             preferred_element_type=F32))
    weights = []
    for h in heads:
        logit = ws[h] - tails[h] - later[h]
        if valid is not None:
            logit = jnp.where(valid, logit, NEG)
        weights.append(jnp.exp2(logit).astype(BF16))
    for h in heads:
        acc_ref[h] += jnp.dot(vtbs[h], weights[h], preferred_element_type=F32)
    return tuple(later[h] + tails[h][0:1, :] for h in heads)


def _sb_walk(step, first, last, later):
    def alive(later):
        return (jnp.min(functools.reduce(jnp.minimum, later)) < SB_DEAD_LOG2).astype(jnp.int32)

    def cond(state):
        return jnp.logical_and(state[0] < last, state[1] > 0)

    def body(state):
        later = step(state[0], state[2])
        return state[0] + 1, alive(later), later

    return lax.while_loop(cond, body, (jnp.asarray(first, jnp.int32), alive(later), later))[2]


def _sb_finish(acc_ref, o_ref, hb):
    for h in range(hb):
        o_ref[0, :, h * HEAD_W:(h + 1) * HEAD_W] = acc_ref[h].T.astype(BF16)


def _sb_attn_kernel(tq, tk, n_k, q_off, hb, q_ref, k_ref, vt_ref, o_ref, acc_ref):
    n_int, n_tot, qmin = _kv_range("sb", pl.program_id(2), tq, tk, n_k, q_off)
    q_pos = qmin + lax.broadcasted_iota(jnp.int32, (1, tq), 1)
    suffix2 = _suffix_matrix(tk)
    acc_ref[...] = jnp.zeros_like(acc_ref)

    def step(masked, t, later):
        j = n_tot - 1 - t
        k0 = pl.multiple_of(j * tk, tk)
        valid = (k0 + lax.broadcasted_iota(jnp.int32, (tk, 1), 0)) < q_pos if masked else None
        kbs = [k_ref[0, pl.ds(k0, tk), h * HEAD_W:(h + 1) * HEAD_W] for h in range(hb)]
        vtbs = [vt_ref[0, h, j] for h in range(hb)]
        return _sb_block(q_ref, acc_ref, suffix2, kbs, vtbs, valid, later)

    later = (jnp.zeros((1, tq), F32),) * hb
    later = lax.fori_loop(0, n_tot - n_int, functools.partial(step, True), later)
    _sb_walk(functools.partial(step, False), n_tot - n_int, n_tot, later)
    _sb_finish(acc_ref, o_ref, hb)


def _attention(kind, q, k, vt, q_off, extra, cfg):
    b, t_q, width = q.shape
    n_k, tk = vt.shape[2], vt.shape[4]
    t_k = k.shape[1]
    assert t_k == n_k * tk
    tq = _tile(t_q, DIFF_Q_TILE if kind == "diff" else SB_Q_TILE)
    hb = DIFF_HEADS_PER_STEP if kind == "diff" else SB_HEADS_PER_STEP
    kv_mode = dict(pipeline_mode=pl.Buffered(1)) if hb == N_HEADS else {}
    q_spec = pl.BlockSpec((1, tq, hb * HEAD_W), lambda bi, h, i: (bi, i, h))
    in_specs = [q_spec,
                pl.BlockSpec((1, t_k, hb * HEAD_W), lambda bi, h, i: (bi, 0, h), **kv_mode),
                pl.BlockSpec((1, hb, n_k, HEAD_W, tk), lambda bi, h, i: (bi, h, 0, 0, 0), **kv_mode)]
    for e in extra:
        in_specs.append(pl.BlockSpec(e.shape, lambda bi, h, i: (0, 0)))
    if kind == "diff":
        body = functools.partial(_diff_attn_kernel, tq, tk, n_k, q_off, cfg, hb)
        scratch = [pltpu.VMEM((2 * hb, tq, HEAD_W), BF16), pltpu.VMEM((2 * hb, HEAD_W, tq), F32)]
    else:
        body = functools.partial(_sb_attn_kernel, tq, tk, n_k, q_off, hb)
        scratch = [pltpu.VMEM((hb, HEAD_W, tq), F32)]
    return pl.pallas_call(
        body,
        grid=(b, N_HEADS // hb, t_q // tq),
        in_specs=in_specs,
        out_specs=q_spec,
        out_shape=jax.ShapeDtypeStruct((b, t_q, width), BF16),
        scratch_shapes=scratch,
        compiler_params=_params("parallel", "parallel", "arbitrary"),
        name=kind + "_attn",
    )(q, k, vt, *extra)


def _past_operands(kp_ref, vp_ref, k0, tk):
    heads = range(N_HEADS)
    rows = lambda h: pl.ds(k0 * N_HEADS + h, tk, stride=N_HEADS)
    kbs = [kp_ref[0, rows(h), :].astype(BF16) for h in heads]
    vtbs = [vp_ref[0, rows(h), :].T.astype(BF16) for h in heads]
    return kbs, vtbs


def _new_operands(kn_ref, vn_ref):
    heads = range(N_HEADS)
    kbs = [kn_ref[0, :, h * HEAD_W:(h + 1) * HEAD_W].astype(BF16) for h in heads]
    vtbs = [vn_ref[0, :, h * HEAD_W:(h + 1) * HEAD_W].T.astype(BF16) for h in heads]
    return kbs, vtbs


def _new_positions(q_off, t_new, tq):
    k_idx = lax.broadcasted_iota(jnp.int32, (t_new, 1), 0)
    q_idx = lax.broadcasted_iota(jnp.int32, (1, tq), 1)
    return k_idx, q_off + k_idx, q_off + q_idx


def _diff_cached_kernel(tq, tk, n_past, q_off, t_real, lam_init,
                        q_ref, kp_ref, vp_ref, kn_ref, vn_ref, lq1_ref, lk1_ref, lq2_ref, lk2_ref, g_ref,
                        o_ref, qm_ref, acc_ref):
    _split_maps(q_ref, qm_ref, N_HEADS)
    carry = _diff_init(acc_ref, tq, N_HEADS)
    heads = range(N_HEADS)
    for j in range(n_past):
        kbs = [kp_ref[0, h, :, j * tk:(j + 1) * tk].T.astype(BF16) for h in heads]
        vtbs = [vp_ref[0, pl.ds(j * tk * N_HEADS + h, tk, stride=N_HEADS), :].T.astype(BF16) for h in heads]
        carry = _diff_block(qm_ref, acc_ref, kbs, vtbs, None, carry)
    k_idx, k_pos, q_pos = _new_positions(q_off, kn_ref.shape[1], tq)
    valid = ((k_pos >> CHUNK_SHIFT) <= (q_pos >> CHUNK_SHIFT)) & (k_idx < t_real)
    kbs, vtbs = _new_operands(kn_ref, vn_ref)
    carry = _diff_block(qm_ref, acc_ref, kbs, vtbs, valid, carry)
    _diff_finish(carry, acc_ref, (lq1_ref, lk1_ref, lq2_ref, lk2_ref), g_ref, lam_init, o_ref, N_HEADS)


def _sb_cached_kernel(tq, tk, n_past, q_off, t_real, q_ref, kp_ref, vp_ref, kn_ref, vn_ref, o_ref, acc_ref):
    acc_ref[...] = jnp.zeros_like(acc_ref)
    t_new = kn_ref.shape[1]
    k_idx, k_pos, q_pos = _new_positions(q_off, t_new, tq)
    valid = (k_pos < q_pos) & (k_idx < t_real)
    kbs, vtbs = _new_operands(kn_ref, vn_ref)
    later = (jnp.zeros((1, tq), F32),) * N_HEADS
    later = _sb_block(q_ref, acc_ref, _suffix_matrix(t_new), kbs, vtbs, valid, later)
    suffix2 = _suffix_matrix(tk)

    def past(t, later):
        kbs, vtbs = _past_operands(kp_ref, vp_ref, pl.multiple_of((n_past - 1 - t) * tk, tk), tk)
        return _sb_block(q_ref, acc_ref, suffix2, kbs, vtbs, None, later)

    _sb_walk(past, 0, n_past, later)
    _sb_finish(acc_ref, o_ref, N_HEADS)


def _attention_cached(kind, q, k_past, v_past, k_new, v_new, t_real, extra, lam_init):
    b, tq, width = q.shape
    p_len, t_new = v_past.shape[1] // N_HEADS, k_new.shape[1]
    tk = _tile(p_len, SB_K_TILE)
    n_past = p_len // tk
    row3 = lambda t: pl.BlockSpec((1, t, width), lambda bi: (bi, 0, 0))
    past_spec = pl.BlockSpec((1, p_len * N_HEADS, HEAD_W), lambda bi: (bi, 0, 0))
    k_spec = pl.BlockSpec((1, N_HEADS, HEAD_W, p_len), lambda bi: (bi, 0, 0, 0)) if kind == "diff" else past_spec
    in_specs = [row3(tq), k_spec, past_spec, row3(t_new), row3(t_new)]
    for e in extra:
        in_specs.append(pl.BlockSpec(e.shape, lambda bi: (0, 0)))
    if kind == "diff":
        body = functools.partial(_diff_cached_kernel, tq, tk, n_past, p_len, t_real, lam_init)
        scratch = [pltpu.VMEM((2 * N_HEADS, tq, HEAD_W), BF16), pltpu.VMEM((2 * N_HEADS, HEAD_W, tq), F32)]
    else:
        body = functools.partial(_sb_cached_kernel, tq, tk, n_past, p_len, t_real)
        scratch = [pltpu.VMEM((N_HEADS, HEAD_W, tq), F32)]
    return pl.pallas_call(
        body,
        grid=(b,),
        in_specs=in_specs,
        out_specs=row3(tq),
        out_shape=jax.ShapeDtypeStruct((b, tq, width), BF16),
        scratch_shapes=scratch,
        compiler_params=_params("parallel"),
        name=kind + "_attn_cached",
    )(q, k_past, v_past, k_new, v_new, *extra)


def _out_kernel(oa_ref, ob_ref, ga_ref, gb_ref, h_ref, wua_ref, wub_ref, wo_ref, n_ref, o_ref):
    pa = jnp.dot(oa_ref[...], wua_ref[...], preferred_element_type=F32)
    pb = jnp.dot(ob_ref[...], wub_ref[...], preferred_element_type=F32)
    merged = (ga_ref[...] * pa + gb_ref[...] * pb).astype(BF16)
    r = jnp.dot(merged, wo_ref[...], preferred_element_type=F32)
    o_ref[...] = h_ref[...] + _rms(r, n_ref[...])


def _merge_out(oa, ob, ga, gb, h, wua, wub, wo, n_post):
    n, d = h.shape
    w_attn = oa.shape[1]
    tm = _tile(n, OUT_TOKEN_TILE)
    row = lambda wd: pl.BlockSpec((tm, wd), lambda i: (i, 0))
    const = lambda shape: pl.BlockSpec(shape, lambda i: (0, 0), pipeline_mode=pl.Buffered(1))
    return pl.pallas_call(
        _out_kernel,
        grid=(n // tm,),
        in_specs=[row(w_attn), row(w_attn), row(d), row(d), row(d),
                  const(wua.shape), const(wub.shape), const(wo.shape), const(n_post.shape)],
        out_specs=row(d),
        out_shape=jax.ShapeDtypeStruct((n, d), F32),
        compiler_params=_params("parallel"),
        name="merge_out",
    )(oa, ob, ga, gb, h, wua, wub, wo, n_post)


def _rope_tables(pos):
    half = ROT_DIM // 2
    inv_freq = jnp.power(ROPE_THETA, -jnp.arange(0, ROT_DIM, 2, dtype=F32) / ROT_DIM)
    lane = jnp.arange(HEAD_W) % MAP_W
    ang = pos.astype(F32)[:, None] * inv_freq[lane % half][None, :]
    cos, sin = jnp.cos(ang), jnp.sin(ang)
    c = jnp.where(lane < ROT_DIM, cos, 1.0)
    sa = jnp.where(lane < half, -sin, 0.0)
    sb = jnp.where((lane >= half) & (lane < ROT_DIM), sin, 0.0)
    return c, sa, sb


def _pad_ffn(w, axis):
    f = w.shape[axis]
    fp = -(-f // FFN_HIDDEN_TILE) * FFN_HIDDEN_TILE
    pad = [(0, 0), (0, 0)]
    pad[axis] = (0, fp - f)
    return jnp.pad(w, pad).astype(BF16)


def _layer(x, pos, past, lam_init, p):
    b, t, d = x.shape
    n = b * t
    w_a = N_HEADS * HEAD_W
    vec = lambda v: v.reshape(1, -1)
    w_in = p["w_in"].astype(BF16)
    seg = lambda lo, hi: w_in[:, lo:hi]

    h1, xn = _ffn(x.reshape(n, d), vec(p["n1a"]), vec(p["n1b"]), vec(p["nma"]),
                  _pad_ffn(p["f1g"], 1), _pad_ffn(p["f1u"], 1), _pad_ffn(p["f1d"], 0))

    tables = tuple(jnp.tile(m, (b, 1)) for m in _rope_tables(pos))
    w_qa, w_ka, w_va, w_qb, w_kb, w_vb = (seg(k * w_a, (k + 1) * w_a) for k in range(6))
    qa_part = (w_qa, "q_rope", MAP_W ** -0.5 * LOG2E, None)
    qb_part = (w_qb, "q", HEAD_W ** -0.5 * LOG2E, None)
    (g_a,), = _proj(xn, [(seg(6 * w_a, 6 * w_a + d), "gate", None, None)], tile=GATE_TOKEN_TILE)
    (g_b,), = _proj(xn, [(seg(6 * w_a + d, 6 * w_a + 2 * d), "gate", None, None)], tile=GATE_TOKEN_TILE)

    shape3 = lambda a: a.reshape(b, t, w_a)
    diff_extra = [vec(p["lq1"]), vec(p["lk1"]), vec(p["lq2"]), vec(p["lk2"]), p["subln_g"].reshape(-1, 1)]
    if past is None:
        (q_a,), (kat, ka16) = _proj(xn, [qa_part, (w_ka, "k_rope", None, None)], tables=tables, seq=t,
                                    tile=PAIR_TOKEN_TILE)
        (q_b,), (kb32, kb16) = _proj(xn, [qb_part, (w_kb, "k", None, None)], tile=PAIR_TOKEN_TILE)
        (va32, vat), (vb32, vbt) = _proj(xn, [(w_va, "v_t", None, _tile(t, DIFF_K_TILE)),
                                              (w_vb, "v_t", None, _tile(t, SB_K_TILE))],
                                         seq=t, tile=PAIR_TOKEN_TILE)
        k_a_rows = kat.reshape(b, N_HEADS, 2, MAP_W, t).transpose(0, 4, 1, 2, 3)
        o_a = _attention("diff", shape3(q_a), shape3(ka16), vat, 0, diff_extra, lam_init)
        o_b = _attention("sb", shape3(q_b), shape3(kb16), vbt, 0, [], None)
    else:
        (q_a,), (ka32,) = _proj(xn, [qa_part, (w_ka, "kv_rope", None, None)], tables=tables,
                                tile=PAIR_TOKEN_TILE)
        (q_b,), (kb32,) = _proj(xn, [qb_part, (w_kb, "kv", None, None)], tile=PAIR_TOKEN_TILE)
        (va32,), (vb32,) = _proj(xn, [(w_va, "kv", None, None), (w_vb, "kv", None, None)],
                                 tile=PAIR_TOKEN_TILE)
        k_a_rows = ka32.reshape(b, t, N_HEADS, 2, MAP_W)
        pka, pva, pkb, pvb = past
        t_p = -(-t // V7X_LANES) * V7X_LANES
        pad_rows = lambda a: jnp.pad(shape3(a), ((0, 0), (0, t_p - t), (0, 0)))
        by_head = lambda c: c.reshape(c.shape[0], c.shape[1] * N_HEADS, HEAD_W)
        pka_t = pka.transpose(0, 2, 3, 4, 1).reshape(pka.shape[0], N_HEADS, HEAD_W, pka.shape[1])
        o_a = _attention_cached("diff", pad_rows(q_a), pka_t, by_head(pva), pad_rows(ka32),
                                pad_rows(va32), t, diff_extra, lam_init)
        o_b = _attention_cached("sb", pad_rows(q_b), by_head(pkb), by_head(pvb), pad_rows(kb32),
                                pad_rows(vb32), t, [], None)
    o_a = o_a[:, :t].reshape(n, w_a)
    o_b = o_b[:, :t].reshape(n, w_a)

    h2 = _merge_out(o_a, o_b, g_a, g_b, h1, p["w_up_a"].astype(BF16), p["w_up_b"].astype(BF16),
                    p["w_o"].astype(BF16), vec(p["nmb"]))
    y = _ffn(h2, vec(p["n2a"]), vec(p["n2b"]), None,
             _pad_ffn(p["f2g"], 1), _pad_ffn(p["f2u"], 1), _pad_ffn(p["f2d"], 0))
    rows = (k_a_rows, va32.reshape(b, t, N_HEADS, HEAD_W),
            kb32.reshape(b, t, N_HEADS, HEAD_W), vb32.reshape(b, t, N_HEADS, HEAD_W))
    return y.reshape(b, t, d), rows


def _stack(xs):
    return xs[0][None] if len(xs) == 1 else jnp.stack(xs)


def kernel(x_prompt, x_sample, cache_diff_k, cache_diff_v, cache_sb_k, cache_sb_v, w_in, w_up_a, w_up_b, w_o, lam_q1, lam_k1, lam_q2, lam_k2, subln_g, norm_ffn1_pre, norm_ffn1_post, norm_mix_pre, norm_mix_post, norm_ffn2_pre, norm_ffn2_post, ffn1_w_gate, ffn1_w_up, ffn1_w_down, ffn2_w_gate, ffn2_w_up, ffn2_w_down):
    depth = w_in.shape[0]
    pos_p = jnp.arange(x_prompt.shape[1], dtype=jnp.int32)
    pos_s = cache_diff_k.shape[2] + jnp.arange(x_sample.shape[1], dtype=jnp.int32)
    hp, hs = x_prompt, x_sample
    rows_p, rows_s = [], []
    for l in range(depth):
        lam_init = 0.8 - 0.6 * math.exp(-0.3 * l)
        p = dict(w_in=w_in[l], w_up_a=w_up_a[l], w_up_b=w_up_b[l], w_o=w_o[l],
                 lq1=lam_q1[l], lk1=lam_k1[l], lq2=lam_q2[l], lk2=lam_k2[l], subln_g=subln_g[l],
                 n1a=norm_ffn1_pre[l], n1b=norm_ffn1_post[l], nma=norm_mix_pre[l], nmb=norm_mix_post[l],
                 n2a=norm_ffn2_pre[l], n2b=norm_ffn2_post[l],
                 f1g=ffn1_w_gate[l], f1u=ffn1_w_up[l], f1d=ffn1_w_down[l],
                 f2g=ffn2_w_gate[l], f2u=ffn2_w_up[l], f2d=ffn2_w_down[l])
        hp, rp = _layer(hp, pos_p, None, lam_init, p)
        past = (cache_diff_k[l], cache_diff_v[l], cache_sb_k[l], cache_sb_v[l])
        hs, rs = _layer(hs, pos_s, past, lam_init, p)
        rows_p.append(rp)
        rows_s.append(rs)
    outs_p = [_stack([r[i] for r in rows_p]) for i in range(4)]
    outs_s = [_stack([r[i] for r in rows_s]) for i in range(4)]
    return (hp, hs, *outs_p, *outs_s)
```

```python
import functools
import math

import jax
import jax.numpy as jnp
from jax import lax
from jax.experimental import pallas as pl
from jax.experimental.pallas import tpu as pltpu

F32 = jnp.float32
BF16 = jnp.bfloat16

EPS = 1e-6
CHUNK = 64
CHUNK_SHIFT = CHUNK.bit_length() - 1
assert CHUNK == 1 << CHUNK_SHIFT
N_HEADS = 8
HEAD_W = 128
MAP_W = 64
ROT_DIM = 16
ROPE_THETA = 500000.0
NEG = -1e30

V7X_LANES = 128
V7X_MXU_DIM = 256
V7X_VMEM_BYTES = 64 * 1024 * 1024
VMEM_LIMIT_BYTES = V7X_VMEM_BYTES - 8 * 1024 * 1024

FFN_TOKEN_TILE = 512
FFN_HIDDEN_TILE = 512
PROJ_TOKEN_TILE = 1024
PAIR_TOKEN_TILE = 512
GATE_TOKEN_TILE = 1024
CAST_TILE = 256
OUT_TOKEN_TILE = 256
DIFF_Q_TILE = 512
SB_Q_TILE = 256
DIFF_K_TILE = 512
SB_K_TILE = V7X_MXU_DIM
DIFF_HEADS_PER_STEP = 4
SB_HEADS_PER_STEP = 8
LOG2E = math.log2(math.e)
F32_MIN_DENORMAL_LOG2 = -149
SB_DEAD_LOG2 = float(-F32_MIN_DENORMAL_LOG2 + 11)


def _tile(n, target):
    t = min(n, target)
    while n % t:
        t -= 1
    return t


def _params(*sem):
    return pltpu.CompilerParams(dimension_semantics=sem, vmem_limit_bytes=VMEM_LIMIT_BYTES)


def _rms(x, g):
    return x * lax.rsqrt(jnp.mean(x * x, axis=-1, keepdims=True) + EPS) * g


def _ffn_kernel(n_f, has_next, x_ref, npre_ref, npost_ref, *refs):
    if has_next:
        nnext_ref, wg_ref, wu_ref, wd_ref, h_ref, xn_ref, xs_ref, acc_ref = refs
    else:
        wg_ref, wu_ref, wd_ref, h_ref, xs_ref, acc_ref = refs
    f = pl.program_id(1)

    @pl.when(f == 0)
    def _():
        xs_ref[...] = _rms(x_ref[...], npre_ref[...]).astype(BF16)
        acc_ref[...] = jnp.zeros_like(acc_ref)

    xs = xs_ref[...]
    g = jnp.dot(xs, wg_ref[...], preferred_element_type=F32)
    u = jnp.dot(xs, wu_ref[...], preferred_element_type=F32)
    a = (g * jax.nn.sigmoid(g) * u).astype(BF16)
    acc_ref[...] += jnp.dot(a, wd_ref[...], preferred_element_type=F32)

    @pl.when(f == n_f - 1)
    def _():
        h = x_ref[...] + 0.5 * _rms(acc_ref[...], npost_ref[...])
        h_ref[...] = h
        if has_next:
            xn_ref[...] = _rms(h, nnext_ref[...]).astype(BF16)


def _ffn(x, n_pre, n_post, n_next, wg, wu, wd):
    n, d = x.shape
    fp = wg.shape[1]
    tm = _tile(n, FFN_TOKEN_TILE)
    tf = _tile(fp, FFN_HIDDEN_TILE)
    n_f = fp // tf
    has_next = n_next is not None
    row = pl.BlockSpec((tm, d), lambda i, f: (i, 0))
    vec = pl.BlockSpec((1, d), lambda i, f: (0, 0))
    in_specs = [row, vec, vec] + ([vec] if has_next else []) + [
        pl.BlockSpec((d, tf), lambda i, f: (0, f)),
        pl.BlockSpec((d, tf), lambda i, f: (0, f)),
        pl.BlockSpec((tf, d), lambda i, f: (f, 0)),
    ]
    out_shape = [jax.ShapeDtypeStruct((n, d), F32)]
    out_specs = [row]
    if has_next:
        out_shape.append(jax.ShapeDtypeStruct((n, d), BF16))
        out_specs.append(row)
    args = [x, n_pre, n_post] + ([n_next] if has_next else []) + [wg, wu, wd]
    res = pl.pallas_call(
        functools.partial(_ffn_kernel, n_f, has_next),
        grid=(n // tm, n_f),
        in_specs=in_specs,
        out_specs=out_specs,
        out_shape=out_shape,
        scratch_shapes=[pltpu.VMEM((tm, d), BF16), pltpu.VMEM((tm, d), F32)],
        compiler_params=_params("parallel", "arbitrary"),
        name="ffn_next" if has_next else "ffn",
    )(*args)
    return res if has_next else res[0]


def _rope(r, c, sa, sb):
    heads = []
    for h in range(r.shape[1] // HEAD_W):
        x = r[:, h * HEAD_W:(h + 1) * HEAD_W]
        up = pltpu.roll(x, HEAD_W - ROT_DIM // 2, 1)
        dn = pltpu.roll(x, ROT_DIM // 2, 1)
        heads.append(x * c + up * sa + dn * sb)
    return jnp.concatenate(heads, axis=1)


PROJ_OUTPUTS = {"q": 1, "q_rope": 1, "gate": 1, "k": 2, "k_rope": 2, "kv": 1, "kv_rope": 1, "v_t": 2}
PROJ_ROTARY = ("q_rope", "k_rope", "kv_rope")


def _proj_epilogue(mode, scale, tk, r, tables, refs):
    if mode in PROJ_ROTARY:
        r = _rope(r, *tables)
    if mode in ("q", "q_rope"):
        (q_ref,) = refs
        q_ref[...] = (r * scale).astype(BF16)
    elif mode == "gate":
        (g_ref,) = refs
        g_ref[...] = jax.nn.sigmoid(r)
    elif mode == "k":
        k32_ref, k16_ref = refs
        k32_ref[...] = r
        k16_ref[...] = r.astype(BF16)
    elif mode == "k_rope":
        kt_ref, k16_ref = refs
        k16_ref[...] = r.astype(BF16)
        for h in range(N_HEADS):
            kt_ref[0, h] = r[:, h * HEAD_W:(h + 1) * HEAD_W].T
    elif mode in ("kv", "kv_rope"):
        (r32_ref,) = refs
        r32_ref[...] = r
    else:
        v32_ref, vt_ref = refs
        v32_ref[...] = r
        for h in range(N_HEADS):
            for jb in range(r.shape[0] // tk):
                blk = r[jb * tk:(jb + 1) * tk, h * HEAD_W:(h + 1) * HEAD_W]
                vt_ref[0, h, jb] = blk.T.astype(BF16)


def _proj_kernel(parts, has_tables, x_ref, *refs):
    w_refs, refs = refs[:len(parts)], refs[len(parts):]
    tables = None
    if has_tables:
        tables, refs = tuple(t[...] for t in refs[:3]), refs[3:]
    x = x_ref[...]
    for (mode, scale, tk), w_ref in zip(parts, w_refs):
        outs, refs = refs[:PROJ_OUTPUTS[mode]], refs[PROJ_OUTPUTS[mode]:]
        _proj_epilogue(mode, scale, tk, jnp.dot(x, w_ref[...], preferred_element_type=F32), tables, outs)


def _proj(xn, parts, *, tables=None, seq=None, tile=PROJ_TOKEN_TILE):
    n, d = xn.shape
    tm = _tile(n if seq is None else seq, tile)
    row = lambda wd: pl.BlockSpec((tm, wd), lambda i: (i, 0))
    has_tables = any(mode in PROJ_ROTARY for _, mode, _, _ in parts)
    in_specs = [row(d)] + [pl.BlockSpec(w.shape, lambda i: (0, 0)) for w, _, _, _ in parts]
    args = [xn] + [w for w, _, _, _ in parts]
    if has_tables:
        in_specs += [row(HEAD_W)] * 3
        args += list(tables)
    out_shape, out_specs = [], []
    for w, mode, _, tk in parts:
        width = w.shape[1]
        if mode in ("q", "q_rope"):
            out_shape += [jax.ShapeDtypeStruct((n, width), BF16)]
            out_specs += [row(width)]
        elif mode in ("gate", "kv", "kv_rope"):
            out_shape += [jax.ShapeDtypeStruct((n, width), F32)]
            out_specs += [row(width)]
        elif mode == "k":
            out_shape += [jax.ShapeDtypeStruct((n, width), F32), jax.ShapeDtypeStruct((n, width), BF16)]
            out_specs += [row(width), row(width)]
        elif mode == "k_rope":
            assert seq % tm == 0
            per_seq = seq // tm
            out_shape += [jax.ShapeDtypeStruct((n // seq, N_HEADS, HEAD_W, seq), F32),
                          jax.ShapeDtypeStruct((n, width), BF16)]
            out_specs += [pl.BlockSpec((1, N_HEADS, HEAD_W, tm),
                                       lambda i, per_seq=per_seq: (i // per_seq, 0, 0, i % per_seq)),
                          row(width)]
        else:
            assert mode == "v_t" and tm % tk == 0 and seq % tm == 0
            per_seq = seq // tm
            out_shape += [jax.ShapeDtypeStruct((n, width), F32),
                          jax.ShapeDtypeStruct((n // seq, N_HEADS, seq // tk, HEAD_W, tk), BF16)]
            out_specs += [row(width),
                          pl.BlockSpec((1, N_HEADS, tm // tk, HEAD_W, tk),
                                       lambda i, per_seq=per_seq: (i // per_seq, 0, i % per_seq, 0, 0))]
    res = pl.pallas_call(
        functools.partial(_proj_kernel, tuple((mode, scale, tk) for _, mode, scale, tk in parts), has_tables),
        grid=(n // tm,),
        in_specs=in_specs,
        out_specs=out_specs,
        out_shape=out_shape,
        compiler_params=_params("parallel"),
        name="proj_" + "_".join(mode for _, mode, _, _ in parts),
    )(*args)
    out, k = [], 0
    for _, mode, _, _ in parts:
        out.append(tuple(res[k:k + PROJ_OUTPUTS[mode]]))
        k += PROJ_OUTPUTS[mode]
    return out


def _kv_range(kind, i, tq, tk, n_k, q_off):
    qmin = q_off + i * tq
    qmax = qmin + tq - 1
    if kind == "diff":
        n_int = ((qmin // CHUNK + 1) * CHUNK) // tk
        last = ((qmax // CHUNK + 1) * CHUNK - 1) // tk
    else:
        n_int = qmin // tk
        last = (qmax - 1) // tk
    smaller = min if isinstance(i, int) else jnp.minimum
    return smaller(n_int, n_k), smaller(last + 1, n_k), qmin


def _split_maps(q_ref, qm_ref, hb):
    lane = lax.broadcasted_iota(jnp.int32, (1, HEAD_W), 1)
    for h in range(hb):
        q = q_ref[0, :, h * HEAD_W:(h + 1) * HEAD_W]
        zero = jnp.zeros_like(q)
        qm_ref[2 * h] = jnp.where(lane < MAP_W, q, zero)
        qm_ref[2 * h + 1] = jnp.where(lane >= MAP_W, q, zero)


def _diff_block(qm_ref, acc_ref, kbs, vtbs, valid, carry):
    carry, probs, alphas = _diff_softmax(_diff_scores(qm_ref, kbs), valid, carry)
    _diff_values(acc_ref, vtbs, probs, alphas)
    return carry


def _diff_scores(qm_ref, kbs):
    return [lax.dot_general(kbs[c // 2], qm_ref[c], (((1,), (1,)), ((), ())), preferred_element_type=F32)
            for c in range(2 * len(kbs))]


def _diff_softmax(scores, valid, carry):
    new, probs, alphas = [], [], []
    for c, s in enumerate(scores):
        m, l = carry[2 * c], carry[2 * c + 1]
        if valid is not None:
            s = jnp.where(valid, s, NEG)
        m_new = jnp.maximum(m, jnp.max(s, axis=0, keepdims=True))
        alpha = jnp.exp2(m - m_new)
        p = jnp.exp2(s - m_new)
        new += [m_new, alpha * l + jnp.sum(p, axis=0, keepdims=True)]
        probs.append(p.astype(BF16))
        alphas.append(alpha)
    return tuple(new), probs, alphas


def _diff_values(acc_ref, vtbs, probs, alphas):
    for c, (p, alpha) in enumerate(zip(probs, alphas)):
        acc_ref[c] = alpha * acc_ref[c] + jnp.dot(vtbs[c // 2], p, preferred_element_type=F32)


def _diff_init(acc_ref, tq, hb):
    acc_ref[...] = jnp.zeros_like(acc_ref)
    return (jnp.full((1, tq), NEG, F32), jnp.zeros((1, tq), F32)) * (2 * hb)


def _diff_finish(carry, acc_ref, lam_refs, g_ref, lam_init, o_ref, hb):
    lq1_ref, lk1_ref, lq2_ref, lk2_ref = lam_refs
    lam = (jnp.exp(jnp.sum(lq1_ref[...] * lk1_ref[...], keepdims=True))
           - jnp.exp(jnp.sum(lq2_ref[...] * lk2_ref[...], keepdims=True)) + lam_init)
    for h in range(hb):
        l0, l1 = carry[4 * h + 1], carry[4 * h + 3]
        ot = acc_ref[2 * h] * (1.0 / l0) - lam * (acc_ref[2 * h + 1] * (1.0 / l1))
        ms = jnp.mean(ot * ot, axis=0, keepdims=True)
        ot = ot * lax.rsqrt(ms + EPS) * g_ref[...] * (1.0 - lam_init)
        o_ref[0, :, h * HEAD_W:(h + 1) * HEAD_W] = ot.T.astype(BF16)


def _diff_attn_kernel(tq, tk, n_k, q_off, lam_init, hb,
                      q_ref, k_ref, vt_ref, lq1_ref, lk1_ref, lq2_ref, lk2_ref, g_ref,
                      o_ref, qm_ref, acc_ref):
    n_int, n_tot, qmin = _kv_range("diff", pl.program_id(2), tq, tk, n_k, q_off)
    _split_maps(q_ref, qm_ref, hb)
    q_chunk = (qmin + lax.broadcasted_iota(jnp.int32, (1, tq), 1)) >> CHUNK_SHIFT

    def step(masked, j, carry):
        k0 = pl.multiple_of(j * tk, tk)
        valid = None
        if masked:
            k_chunk = (k0 + lax.broadcasted_iota(jnp.int32, (tk, 1), 0)) >> CHUNK_SHIFT
            valid = k_chunk <= q_chunk
        kbs = [k_ref[0, pl.ds(k0, tk), h * HEAD_W:(h + 1) * HEAD_W] for h in range(hb)]
        vtbs = [vt_ref[0, h, j] for h in range(hb)]
        return _diff_block(qm_ref, acc_ref, kbs, vtbs, valid, carry)

    carry = _diff_init(acc_ref, tq, hb)
    carry = lax.fori_loop(0, n_int, functools.partial(step, False), carry)
    carry = lax.fori_loop(n_int, n_tot, functools.partial(step, True), carry)
    _diff_finish(carry, acc_ref, (lq1_ref, lk1_ref, lq2_ref, lk2_ref), g_ref, lam_init, o_ref, hb)


def _suffix_matrix(tk):
    s = (lax.broadcasted_iota(jnp.int32, (tk, tk), 1)
         >= lax.broadcasted_iota(jnp.int32, (tk, tk), 0)).astype(BF16)
    return jnp.concatenate([s, s], axis=1)


def _sb_block(q_ref, acc_ref, suffix2, kbs, vtbs, valid, later):
    heads = range(len(kbs))
    ws = [lax.dot_general(kbs[h], q_ref[0, :, h * HEAD_W:(h + 1) * HEAD_W], (((1,), (1,)), ((), ())),
                          preferred_element_type=F32) for h in heads]
    tails = []
    for h in heads:
        sp = jnp.maximum(ws[h], 0.0) + jnp.log(1.0 + jnp.exp2(-jnp.abs(ws[h]))) * LOG2E
        if valid is not None:
            sp = jnp.where(valid, sp, 0.0)
        hi = sp.astype(BF16)
        lo = (sp - hi.astype(F32)).astype(BF16)
        tails.append(jnp.dot(suffix2, jnp.concatenate([hi, lo], axis=0),
                             preferred_element_type=F32))
    weights = []
    for h in heads:
        logit = ws[h] - tails[h] - later[h]
        if valid is not None:
            logit = jnp.where(valid, logit, NEG)
        weights.append(jnp.exp2(logit).astype(BF16))
    for h in heads:
        acc_ref[h] += jnp.dot(vtbs[h], weights[h], preferred_element_type=F32)
    return tuple(later[h] + tails[h][0:1, :] for h in heads)


def _sb_walk(step, first, last, later):
    def alive(later):
        return (jnp.min(functools.reduce(jnp.minimum, later)) < SB_DEAD_LOG2).astype(jnp.int32)

    def cond(state):
        return jnp.logical_and(state[0] < last, state[1] > 0)

    def body(state):
        later = step(state[0], state[2])
        return state[0] + 1, alive(later), later

    return lax.while_loop(cond, body, (jnp.asarray(first, jnp.int32), alive(later), later))[2]


def _sb_finish(acc_ref, o_ref, hb):
    for h in range(hb):
        o_ref[0, :, h * HEAD_W:(h + 1) * HEAD_W] = acc_ref[h].T.astype(BF16)


def _sb_attn_kernel(tq, tk, n_k, q_off, hb, q_ref, k_ref, vt_ref, o_ref, acc_ref):
    n_int, n_tot, qmin = _kv_range("sb", pl.program_id(2), tq, tk, n_k, q_off)
    q_pos = qmin + lax.broadcasted_iota(jnp.int32, (1, tq), 1)
    suffix2 = _suffix_matrix(tk)
    acc_ref[...] = jnp.zeros_like(acc_ref)

    def step(masked, t, later):
        j = n_tot - 1 - t
        k0 = pl.multiple_of(j * tk, tk)
        valid = (k0 + lax.broadcasted_iota(jnp.int32, (tk, 1), 0)) < q_pos if masked else None
        kbs = [k_ref[0, pl.ds(k0, tk), h * HEAD_W:(h + 1) * HEAD_W] for h in range(hb)]
        vtbs = [vt_ref[0, h, j] for h in range(hb)]
        return _sb_block(q_ref, acc_ref, suffix2, kbs, vtbs, valid, later)

    later = (jnp.zeros((1, tq), F32),) * hb
    later = lax.fori_loop(0, n_tot - n_int, functools.partial(step, True), later)
    _sb_walk(functools.partial(step, False), n_tot - n_int, n_tot, later)
    _sb_finish(acc_ref, o_ref, hb)


def _attention(kind, q, k, vt, q_off, extra, cfg):
    b, t_q, width = q.shape
    n_k, tk = vt.shape[2], vt.shape[4]
    t_k = k.shape[1]
    assert t_k == n_k * tk
    tq = _tile(t_q, DIFF_Q_TILE if kind == "diff" else SB_Q_TILE)
    hb = DIFF_HEADS_PER_STEP if kind == "diff" else SB_HEADS_PER_STEP
    kv_mode = dict(pipeline_mode=pl.Buffered(1)) if hb == N_HEADS else {}
    q_spec = pl.BlockSpec((1, tq, hb * HEAD_W), lambda bi, h, i: (bi, i, h))
    in_specs = [q_spec,
                pl.BlockSpec((1, t_k, hb * HEAD_W), lambda bi, h, i: (bi, 0, h), **kv_mode),
                pl.BlockSpec((1, hb, n_k, HEAD_W, tk), lambda bi, h, i: (bi, h, 0, 0, 0), **kv_mode)]
    for e in extra:
        in_specs.append(pl.BlockSpec(e.shape, lambda bi, h, i: (0, 0)))
    if kind == "diff":
        body = functools.partial(_diff_attn_kernel, tq, tk, n_k, q_off, cfg, hb)
        scratch = [pltpu.VMEM((2 * hb, tq, HEAD_W), BF16), pltpu.VMEM((2 * hb, HEAD_W, tq), F32)]
    else:
        body = functools.partial(_sb_attn_kernel, tq, tk, n_k, q_off, hb)
        scratch = [pltpu.VMEM((hb, HEAD_W, tq), F32)]
    return pl.pallas_call(
        body,
        grid=(b, N_HEADS // hb, t_q // tq),
        in_specs=in_specs,
        out_specs=q_spec,
        out_shape=jax.ShapeDtypeStruct((b, t_q, width), BF16),
        scratch_shapes=scratch,
        compiler_params=_params("parallel", "parallel", "arbitrary"),
        name=kind + "_attn",
    )(q, k, vt, *extra)


def _past_operands(kp_ref, vp_ref, k0, tk):
    heads = range(N_HEADS)
    rows = lambda h: pl.ds(k0 * N_HEADS + h, tk, stride=N_HEADS)
    kbs = [kp_ref[0, rows(h), :].astype(BF16) for h in heads]
    vtbs = [vp_ref[0, rows(h), :].T.astype(BF16) for h in heads]
    return kbs, vtbs


def _new_operands(kn_ref, vn_ref):
    heads = range(N_HEADS)
    kbs = [kn_ref[0, :, h * HEAD_W:(h + 1) * HEAD_W].astype(BF16) for h in heads]
    vtbs = [vn_ref[0, :, h * HEAD_W:(h + 1) * HEAD_W].T.astype(BF16) for h in heads]
    return kbs, vtbs


def _new_positions(q_off, t_new, tq):
    k_idx = lax.broadcasted_iota(jnp.int32, (t_new, 1), 0)
    q_idx = lax.broadcasted_iota(jnp.int32, (1, tq), 1)
    return k_idx, q_off + k_idx, q_off + q_idx


def _diff_cached_kernel(tq, tk, n_past, q_off, t_real, lam_init,
                        q_ref, kp_ref, vp_ref, kn_ref, vn_ref, lq1_ref, lk1_ref, lq2_ref, lk2_ref, g_ref,
                        o_ref, qm_ref, acc_ref):
    _split_maps(q_ref, qm_ref, N_HEADS)
    carry = _diff_init(acc_ref, tq, N_HEADS)
    heads = range(N_HEADS)
    for j in range(n_past):
        kbs = [kp_ref[0, h, :, j * tk:(j + 1) * tk].T.astype(BF16) for h in heads]
        vtbs = [vp_ref[0, pl.ds(j * tk * N_HEADS + h, tk, stride=N_HEADS), :].T.astype(BF16) for h in heads]
        carry = _diff_block(qm_ref, acc_ref, kbs, vtbs, None, carry)
    k_idx, k_pos, q_pos = _new_positions(q_off, kn_ref.shape[1], tq)
    valid = ((k_pos >> CHUNK_SHIFT) <= (q_pos >> CHUNK_SHIFT)) & (k_idx < t_real)
    kbs, vtbs = _new_operands(kn_ref, vn_ref)
    carry = _diff_block(qm_ref, acc_ref, kbs, vtbs, valid, carry)
    _diff_finish(carry, acc_ref, (lq1_ref, lk1_ref, lq2_ref, lk2_ref), g_ref, lam_init, o_ref, N_HEADS)


def _sb_cached_kernel(tq, tk, n_past, q_off, t_real, q_ref, kp_ref, vp_ref, kn_ref, vn_ref, o_ref, acc_ref):
    acc_ref[...] = jnp.zeros_like(acc_ref)
    t_new = kn_ref.shape[1]
    k_idx, k_pos, q_pos = _new_positions(q_off, t_new, tq)
    valid = (k_pos < q_pos) & (k_idx < t_real)
    kbs, vtbs = _new_operands(kn_ref, vn_ref)
    later = (jnp.zeros((1, tq), F32),) * N_HEADS
    later = _sb_block(q_ref, acc_ref, _suffix_matrix(t_new), kbs, vtbs, valid, later)
    suffix2 = _suffix_matrix(tk)

    def past(t, later):
        kbs, vtbs = _past_operands(kp_ref, vp_ref, pl.multiple_of((n_past - 1 - t) * tk, tk), tk)
        return _sb_block(q_ref, acc_ref, suffix2, kbs, vtbs, None, later)

    _sb_walk(past, 0, n_past, later)
    _sb_finish(acc_ref, o_ref, N_HEADS)


def _attention_cached(kind, q, k_past, v_past, k_new, v_new, t_real, extra, lam_init):
    b, tq, width = q.shape
    p_len, t_new = v_past.shape[1] // N_HEADS, k_new.shape[1]
    tk = _tile(p_len, SB_K_TILE)
    n_past = p_len // tk
    row3 = lambda t: pl.BlockSpec((1, t, width), lambda bi: (bi, 0, 0))
    past_spec = pl.BlockSpec((1, p_len * N_HEADS, HEAD_W), lambda bi: (bi, 0, 0))
    k_spec = pl.BlockSpec((1, N_HEADS, HEAD_W, p_len), lambda bi: (bi, 0, 0, 0)) if kind == "diff" else past_spec
    in_specs = [row3(tq), k_spec, past_spec, row3(t_new), row3(t_new)]
    for e in extra:
        in_specs.append(pl.BlockSpec(e.shape, lambda bi: (0, 0)))
    if kind == "diff":
        body = functools.partial(_diff_cached_kernel, tq, tk, n_past, p_len, t_real, lam_init)
        scratch = [pltpu.VMEM((2 * N_HEADS, tq, HEAD_W), BF16), pltpu.VMEM((2 * N_HEADS, HEAD_W, tq), F32)]
    else:
        body = functools.partial(_sb_cached_kernel, tq, tk, n_past, p_len, t_real)
        scratch = [pltpu.VMEM((N_HEADS, HEAD_W, tq), F32)]
    return pl.pallas_call(
        body,
        grid=(b,),
        in_specs=in_specs,
        out_specs=row3(tq),
        out_shape=jax.ShapeDtypeStruct((b, tq, width), BF16),
        scratch_shapes=scratch,
        compiler_params=_params("parallel"),
        name=kind + "_attn_cached",
    )(q, k_past, v_past, k_new, v_new, *extra)


def _out_kernel(oa_ref, ob_ref, ga_ref, gb_ref, h_ref, wua_ref, wub_ref, wo_ref, n_ref, o_ref):
    pa = jnp.dot(oa_ref[...], wua_ref[...], preferred_element_type=F32)
    pb = jnp.dot(ob_ref[...], wub_ref[...], preferred_element_type=F32)
    merged = (ga_ref[...] * pa + gb_ref[...] * pb).astype(BF16)
    r = jnp.dot(merged, wo_ref[...], preferred_element_type=F32)
    o_ref[...] = h_ref[...] + _rms(r, n_ref[...])


def _merge_out(oa, ob, ga, gb, h, wua, wub, wo, n_post):
    n, d = h.shape
    w_attn = oa.shape[1]
    tm = _tile(n, OUT_TOKEN_TILE)
    row = lambda wd: pl.BlockSpec((tm, wd), lambda i: (i, 0))
    const = lambda shape: pl.BlockSpec(shape, lambda i: (0, 0), pipeline_mode=pl.Buffered(1))
    return pl.pallas_call(
        _out_kernel,
        grid=(n // tm,),
        in_specs=[row(w_attn), row(w_attn), row(d), row(d), row(d),
                  const(wua.shape), const(wub.shape), const(wo.shape), const(n_post.shape)],
        out_specs=row(d),
        out_shape=jax.ShapeDtypeStruct((n, d), F32),
        compiler_params=_params("parallel"),
        name="merge_out",
    )(oa, ob, ga, gb, h, wua, wub, wo, n_post)


def _rope_tables(pos):
    half = ROT_DIM // 2
    inv_freq = jnp.power(ROPE_THETA, -jnp.arange(0, ROT_DIM, 2, dtype=F32) / ROT_DIM)
    lane = jnp.arange(HEAD_W) % MAP_W
    ang = pos.astype(F32)[:, None] * inv_freq[lane % half][None, :]
    cos, sin = jnp.cos(ang), jnp.sin(ang)
    c = jnp.where(lane < ROT_DIM, cos, 1.0)
    sa = jnp.where(lane < half, -sin, 0.0)
    sb = jnp.where((lane >= half) & (lane < ROT_DIM), sin, 0.0)
    return c, sa, sb


def _cast_pad_kernel(f, axis, x_ref, o_ref):
    x = x_ref[...].astype(BF16)
    if axis == 1:
        o_ref[:, :f] = x
        o_ref[:, f:] = jnp.zeros((o_ref.shape[0], o_ref.shape[1] - f), BF16)
    else:
        o_ref[:f, :] = x
        o_ref[f:, :] = jnp.zeros((o_ref.shape[0] - f, o_ref.shape[1]), BF16)


def _pad_ffn(w, axis):
    f, other = w.shape[axis], w.shape[1 - axis]
    fp = -(-f // FFN_HIDDEN_TILE) * FFN_HIDDEN_TILE
    t = _tile(other, CAST_TILE)
    if axis == 1:
        in_block, out_block, index, out = (t, f), (t, fp), (lambda i: (i, 0)), (other, fp)
    else:
        in_block, out_block, index, out = (f, t), (fp, t), (lambda i: (0, i)), (fp, other)
    return pl.pallas_call(
        functools.partial(_cast_pad_kernel, f, axis),
        grid=(other // t,),
        in_specs=[pl.BlockSpec(in_block, index)],
        out_specs=pl.BlockSpec(out_block, index),
        out_shape=jax.ShapeDtypeStruct(out, BF16),
        compiler_params=_params("parallel"),
        name="cast_pad",
    )(w)


def _layer(x, pos, past, lam_init, p):
    b, t, d = x.shape
    n = b * t
    w_a = N_HEADS * HEAD_W
    vec = lambda v: v.reshape(1, -1)
    w_in = p["w_in"].astype(BF16)
    seg = lambda lo, hi: w_in[:, lo:hi]

    h1, xn = _ffn(x.reshape(n, d), vec(p["n1a"]), vec(p["n1b"]), vec(p["nma"]),
                  p["f1g"], p["f1u"], p["f1d"])

    tables = tuple(jnp.tile(m, (b, 1)) for m in _rope_tables(pos))
    w_qa, w_ka, w_va, w_qb, w_kb, w_vb = (seg(k * w_a, (k + 1) * w_a) for k in range(6))
    qa_part = (w_qa, "q_rope", MAP_W ** -0.5 * LOG2E, None)
    qb_part = (w_qb, "q", HEAD_W ** -0.5 * LOG2E, None)
    (g_a,), = _proj(xn, [(seg(6 * w_a, 6 * w_a + d), "gate", None, None)], tile=GATE_TOKEN_TILE)
    (g_b,), = _proj(xn, [(seg(6 * w_a + d, 6 * w_a + 2 * d), "gate", None, None)], tile=GATE_TOKEN_TILE)

    shape3 = lambda a: a.reshape(b, t, w_a)
    diff_extra = [vec(p["lq1"]), vec(p["lk1"]), vec(p["lq2"]), vec(p["lk2"]), p["subln_g"].reshape(-1, 1)]
    if past is None:
        (q_a,), (kat, ka16) = _proj(xn, [qa_part, (w_ka, "k_rope", None, None)], tables=tables, seq=t,
                                    tile=PAIR_TOKEN_TILE)
        (q_b,), (kb32, kb16) = _proj(xn, [qb_part, (w_kb, "k", None, None)], tile=PAIR_TOKEN_TILE)
        (va32, vat), (vb32, vbt) = _proj(xn, [(w_va, "v_t", None, _tile(t, DIFF_K_TILE)),
                                              (w_vb, "v_t", None, _tile(t, SB_K_TILE))],
                                         seq=t, tile=PAIR_TOKEN_TILE)
        k_a_rows = kat.reshape(b, N_HEADS, 2, MAP_W, t).transpose(0, 4, 1, 2, 3)
        o_a = _attention("diff", shape3(q_a), shape3(ka16), vat, 0, diff_extra, lam_init)
        o_b = _attention("sb", shape3(q_b), shape3(kb16), vbt, 0, [], None)
    else:
        (q_a,), (ka32,) = _proj(xn, [qa_part, (w_ka, "kv_rope", None, None)], tables=tables,
                                tile=PAIR_TOKEN_TILE)
        (q_b,), (kb32,) = _proj(xn, [qb_part, (w_kb, "kv", None, None)], tile=PAIR_TOKEN_TILE)
        (va32,), (vb32,) = _proj(xn, [(w_va, "kv", None, None), (w_vb, "kv", None, None)],
                                 tile=PAIR_TOKEN_TILE)
        k_a_rows = ka32.reshape(b, t, N_HEADS, 2, MAP_W)
        pka, pva, pkb, pvb = past
        t_p = -(-t // V7X_LANES) * V7X_LANES
        pad_rows = lambda a: jnp.pad(shape3(a), ((0, 0), (0, t_p - t), (0, 0)))
        by_head = lambda c: c.reshape(c.shape[0], c.shape[1] * N_HEADS, HEAD_W)
        pka_t = pka.transpose(0, 2, 3, 4, 1).reshape(pka.shape[0], N_HEADS, HEAD_W, pka.shape[1])
        o_a = _attention_cached("diff", pad_rows(q_a), pka_t, by_head(pva), pad_rows(ka32),
                                pad_rows(va32), t, diff_extra, lam_init)
        o_b = _attention_cached("sb", pad_rows(q_b), by_head(pkb), by_head(pvb), pad_rows(kb32),
                                pad_rows(vb32), t, [], None)
    o_a = o_a[:, :t].reshape(n, w_a)
    o_b = o_b[:, :t].reshape(n, w_a)

    h2 = _merge_out(o_a, o_b, g_a, g_b, h1, p["w_up_a"].astype(BF16), p["w_up_b"].astype(BF16),
                    p["w_o"].astype(BF16), vec(p["nmb"]))
    y = _ffn(h2, vec(p["n2a"]), vec(p["n2b"]), None,
             p["f2g"], p["f2u"], p["f2d"])
    rows = (k_a_rows, va32.reshape(b, t, N_HEADS, HEAD_W),
            kb32.reshape(b, t, N_HEADS, HEAD_W), vb32.reshape(b, t, N_HEADS, HEAD_W))
    return y.reshape(b, t, d), rows


def _stack(xs):
    return xs[0][None] if len(xs) == 1 else jnp.stack(xs)


def kernel(x_prompt, x_sample, cache_diff_k, cache_diff_v, cache_sb_k, cache_sb_v, w_in, w_up_a, w_up_b, w_o, lam_q1, lam_k1, lam_q2, lam_k2, subln_g, norm_ffn1_pre, norm_ffn1_post, norm_mix_pre, norm_mix_post, norm_ffn2_pre, norm_ffn2_post, ffn1_w_gate, ffn1_w_up, ffn1_w_down, ffn2_w_gate, ffn2_w_up, ffn2_w_down):
    depth = w_in.shape[0]
    pos_p = jnp.arange(x_prompt.shape[1], dtype=jnp.int32)
    pos_s = cache_diff_k.shape[2] + jnp.arange(x_sample.shape[1], dtype=jnp.int32)
    hp, hs = x_prompt, x_sample
    rows_p, rows_s = [], []
    for l in range(depth):
        lam_init = 0.8 - 0.6 * math.exp(-0.3 * l)
        p = dict(w_in=w_in[l], w_up_a=w_up_a[l], w_up_b=w_up_b[l], w_o=w_o[l],
                 lq1=lam_q1[l], lk1=lam_k1[l], lq2=lam_q2[l], lk2=lam_k2[l], subln_g=subln_g[l],
                 n1a=norm_ffn1_pre[l], n1b=norm_ffn1_post[l], nma=norm_mix_pre[l], nmb=norm_mix_post[l],
                 n2a=norm_ffn2_pre[l], n2b=norm_ffn2_post[l],
                 f1g=_pad_ffn(ffn1_w_gate[l], 1), f1u=_pad_ffn(ffn1_w_up[l], 1), f1d=_pad_ffn(ffn1_w_down[l], 0),
                 f2g=_pad_ffn(ffn2_w_gate[l], 1), f2u=_pad_ffn(ffn2_w_up[l], 1), f2d=_pad_ffn(ffn2_w_down[l], 0))
        hp, rp = _layer(hp, pos_p, None, lam_init, p)
        past = (cache_diff_k[l], cache_diff_v[l], cache_sb_k[l], cache_sb_v[l])
        hs, rs = _layer(hs, pos_s, past, lam_init, p)
        rows_p.append(rp)
        rows_s.append(rs)
    outs_p = [_stack([r[i] for r in rows_p]) for i in range(4)]
    outs_s = [_stack([r[i] for r in rows_s]) for i in range(4)]
    return (hp, hs, *outs_p, *outs_s)
```

```python
import functools
import math

import jax
import jax.numpy as jnp
from jax import lax
from jax.experimental import pallas as pl
from jax.experimental.pallas import tpu as pltpu

F32 = jnp.float32
BF16 = jnp.bfloat16

EPS = 1e-6
CHUNK = 64
CHUNK_SHIFT = CHUNK.bit_length() - 1
assert CHUNK == 1 << CHUNK_SHIFT
N_HEADS = 8
HEAD_W = 128
MAP_W = 64
ROT_DIM = 16
ROPE_THETA = 500000.0
NEG = -1e30

V7X_LANES = 128
V7X_MXU_DIM = 256
V7X_VMEM_BYTES = 64 * 1024 * 1024
VMEM_LIMIT_BYTES = V7X_VMEM_BYTES - 8 * 1024 * 1024

FFN_TOKEN_TILE = 512
FFN_HIDDEN_TILE = 512
FFN_WEIGHT_SLOTS = 3
PROJ_TOKEN_TILE = 1024
PAIR_TOKEN_TILE = 512
GATE_TOKEN_TILE = 1024
CAST_TILE = 256
OUT_TOKEN_TILE = 256
DIFF_Q_TILE = 512
SB_Q_TILE = 256
DIFF_K_TILE = 512
SB_K_TILE = V7X_MXU_DIM
DIFF_HEADS_PER_STEP = 4
SB_HEADS_PER_STEP = 8
LOG2E = math.log2(math.e)
F32_MIN_DENORMAL_LOG2 = -149
SB_DEAD_LOG2 = float(-F32_MIN_DENORMAL_LOG2 + 11)


def _tile(n, target):
    t = min(n, target)
    while n % t:
        t -= 1
    return t


def _params(*sem):
    return pltpu.CompilerParams(dimension_semantics=sem, vmem_limit_bytes=VMEM_LIMIT_BYTES)


def _rms(x, g):
    return x * lax.rsqrt(jnp.mean(x * x, axis=-1, keepdims=True) + EPS) * g


def _ffn_kernel(n_f, n_steps, has_next, x_ref, npre_ref, npost_ref, *refs):
    if has_next:
        nnext_ref, wg_hbm, wu_hbm, wd_hbm, h_ref, xn_ref, xs_ref, acc_ref, wg_buf, wu_buf, wd_buf, sem = refs
    else:
        wg_hbm, wu_hbm, wd_hbm, h_ref, xs_ref, acc_ref, wg_buf, wu_buf, wd_buf, sem = refs
    f = pl.program_id(1)
    step = pl.program_id(0) * n_f + f
    tf = wg_buf.shape[2]
    lookahead = FFN_WEIGHT_SLOTS - 1

    def copies(s):
        slot = lax.rem(s, FFN_WEIGHT_SLOTS)
        c0 = pl.multiple_of(lax.rem(s, n_f) * tf, tf)
        return (pltpu.make_async_copy(wg_hbm.at[:, pl.ds(c0, tf)], wg_buf.at[slot], sem.at[0, slot]),
                pltpu.make_async_copy(wu_hbm.at[:, pl.ds(c0, tf)], wu_buf.at[slot], sem.at[1, slot]),
                pltpu.make_async_copy(wd_hbm.at[pl.ds(c0, tf), :], wd_buf.at[slot], sem.at[2, slot]))

    @pl.when(step == 0)
    def _():
        for s in range(lookahead):
            for c in copies(jnp.int32(s)):
                c.start()

    @pl.when(step + lookahead < n_steps)
    def _():
        for c in copies(step + lookahead):
            c.start()

    @pl.when(f == 0)
    def _():
        xs_ref[...] = _rms(x_ref[...], npre_ref[...]).astype(BF16)
        acc_ref[...] = jnp.zeros_like(acc_ref)

    for c in copies(step):
        c.wait()
    slot = lax.rem(step, FFN_WEIGHT_SLOTS)
    xs = xs_ref[...]
    g = jnp.dot(xs, wg_buf[slot], preferred_element_type=F32)
    u = jnp.dot(xs, wu_buf[slot], preferred_element_type=F32)
    a = (g * jax.nn.sigmoid(g) * u).astype(BF16)
    acc_ref[...] += jnp.dot(a, wd_buf[slot], preferred_element_type=F32)

    @pl.when(f == n_f - 1)
    def _():
        h = x_ref[...] + 0.5 * _rms(acc_ref[...], npost_ref[...])
        h_ref[...] = h
        if has_next:
            xn_ref[...] = _rms(h, nnext_ref[...]).astype(BF16)


def _ffn(x, n_pre, n_post, n_next, wg, wu, wd):
    n, d = x.shape
    fp = wg.shape[1]
    tm = _tile(n, FFN_TOKEN_TILE)
    tf = _tile(fp, FFN_HIDDEN_TILE)
    n_f = fp // tf
    has_next = n_next is not None
    row = pl.BlockSpec((tm, d), lambda i, f: (i, 0))
    vec = pl.BlockSpec((1, d), lambda i, f: (0, 0))
    n_steps = (n // tm) * n_f
    assert n_steps >= FFN_WEIGHT_SLOTS
    hbm = pl.BlockSpec(memory_space=pl.ANY)
    in_specs = [row, vec, vec] + ([vec] if has_next else []) + [hbm, hbm, hbm]
    out_shape = [jax.ShapeDtypeStruct((n, d), F32)]
    out_specs = [row]
    if has_next:
        out_shape.append(jax.ShapeDtypeStruct((n, d), BF16))
        out_specs.append(row)
    args = [x, n_pre, n_post] + ([n_next] if has_next else []) + [wg, wu, wd]
    res = pl.pallas_call(
        functools.partial(_ffn_kernel, n_f, n_steps, has_next),
        grid=(n // tm, n_f),
        in_specs=in_specs,
        out_specs=out_specs,
        out_shape=out_shape,
        scratch_shapes=[pltpu.VMEM((tm, d), BF16), pltpu.VMEM((tm, d), F32),
                        pltpu.VMEM((FFN_WEIGHT_SLOTS, d, tf), BF16), pltpu.VMEM((FFN_WEIGHT_SLOTS, d, tf), BF16),
                        pltpu.VMEM((FFN_WEIGHT_SLOTS, tf, d), BF16),
                        pltpu.SemaphoreType.DMA((3, FFN_WEIGHT_SLOTS))],
        compiler_params=_params("arbitrary", "arbitrary"),
        name="ffn_next" if has_next else "ffn",
    )(*args)
    return res if has_next else res[0]


def _rope(r, c, sa, sb):
    heads = []
    for h in range(r.shape[1] // HEAD_W):
        x = r[:, h * HEAD_W:(h + 1) * HEAD_W]
        up = pltpu.roll(x, HEAD_W - ROT_DIM // 2, 1)
        dn = pltpu.roll(x, ROT_DIM // 2, 1)
        heads.append(x * c + up * sa + dn * sb)
    return jnp.concatenate(heads, axis=1)


PROJ_OUTPUTS = {"q": 1, "q_rope": 1, "gate": 1, "k": 2, "k_rope": 2, "kv": 1, "kv_rope": 1, "v_t": 2}
PROJ_ROTARY = ("q_rope", "k_rope", "kv_rope")


def _proj_epilogue(mode, scale, tk, r, tables, refs):
    if mode in PROJ_ROTARY:
        r = _rope(r, *tables)
    if mode in ("q", "q_rope"):
        (q_ref,) = refs
        q_ref[...] = (r * scale).astype(BF16)
    elif mode == "gate":
        (g_ref,) = refs
        g_ref[...] = jax.nn.sigmoid(r)
    elif mode == "k":
        k32_ref, k16_ref = refs
        k32_ref[...] = r
        k16_ref[...] = r.astype(BF16)
    elif mode == "k_rope":
        kt_ref, k16_ref = refs
        k16_ref[...] = r.astype(BF16)
        for h in range(N_HEADS):
            kt_ref[0, h] = r[:, h * HEAD_W:(h + 1) * HEAD_W].T
    elif mode in ("kv", "kv_rope"):
        (r32_ref,) = refs
        r32_ref[...] = r
    else:
        v32_ref, vt_ref = refs
        v32_ref[...] = r
        for h in range(N_HEADS):
            for jb in range(r.shape[0] // tk):
                blk = r[jb * tk:(jb + 1) * tk, h * HEAD_W:(h + 1) * HEAD_W]
                vt_ref[0, h, jb] = blk.T.astype(BF16)


def _proj_kernel(parts, has_tables, x_ref, *refs):
    w_refs, refs = refs[:len(parts)], refs[len(parts):]
    tables = None
    if has_tables:
        tables, refs = tuple(t[...] for t in refs[:3]), refs[3:]
    x = x_ref[...]
    for (mode, scale, tk), w_ref in zip(parts, w_refs):
        outs, refs = refs[:PROJ_OUTPUTS[mode]], refs[PROJ_OUTPUTS[mode]:]
        _proj_epilogue(mode, scale, tk, jnp.dot(x, w_ref[...], preferred_element_type=F32), tables, outs)


def _proj(xn, parts, *, tables=None, seq=None, tile=PROJ_TOKEN_TILE):
    n, d = xn.shape
    tm = _tile(n if seq is None else seq, tile)
    row = lambda wd: pl.BlockSpec((tm, wd), lambda i: (i, 0))
    has_tables = any(mode in PROJ_ROTARY for _, mode, _, _ in parts)
    in_specs = [row(d)] + [pl.BlockSpec(w.shape, lambda i: (0, 0)) for w, _, _, _ in parts]
    args = [xn] + [w for w, _, _, _ in parts]
    if has_tables:
        in_specs += [row(HEAD_W)] * 3
        args += list(tables)
    out_shape, out_specs = [], []
    for w, mode, _, tk in parts:
        width = w.shape[1]
        if mode in ("q", "q_rope"):
            out_shape += [jax.ShapeDtypeStruct((n, width), BF16)]
            out_specs += [row(width)]
        elif mode in ("gate", "kv", "kv_rope"):
            out_shape += [jax.ShapeDtypeStruct((n, width), F32)]
            out_specs += [row(width)]
        elif mode == "k":
            out_shape += [jax.ShapeDtypeStruct((n, width), F32), jax.ShapeDtypeStruct((n, width), BF16)]
            out_specs += [row(width), row(width)]
        elif mode == "k_rope":
            assert seq % tm == 0
            per_seq = seq // tm
            out_shape += [jax.ShapeDtypeStruct((n // seq, N_HEADS, HEAD_W, seq), F32),
                          jax.ShapeDtypeStruct((n, width), BF16)]
            out_specs += [pl.BlockSpec((1, N_HEADS, HEAD_W, tm),
                                       lambda i, per_seq=per_seq: (i // per_seq, 0, 0, i % per_seq)),
                          row(width)]
        else:
            assert mode == "v_t" and tm % tk == 0 and seq % tm == 0
            per_seq = seq // tm
            out_shape += [jax.ShapeDtypeStruct((n, width), F32),
                          jax.ShapeDtypeStruct((n // seq, N_HEADS, seq // tk, HEAD_W, tk), BF16)]
            out_specs += [row(width),
                          pl.BlockSpec((1, N_HEADS, tm // tk, HEAD_W, tk),
                                       lambda i, per_seq=per_seq: (i // per_seq, 0, i % per_seq, 0, 0))]
    res = pl.pallas_call(
        functools.partial(_proj_kernel, tuple((mode, scale, tk) for _, mode, scale, tk in parts), has_tables),
        grid=(n // tm,),
        in_specs=in_specs,
        out_specs=out_specs,
        out_shape=out_shape,
        compiler_params=_params("parallel"),
        name="proj_" + "_".join(mode for _, mode, _, _ in parts),
    )(*args)
    out, k = [], 0
    for _, mode, _, _ in parts:
        out.append(tuple(res[k:k + PROJ_OUTPUTS[mode]]))
        k += PROJ_OUTPUTS[mode]
    return out


def _kv_range(kind, i, tq, tk, n_k, q_off):
    qmin = q_off + i * tq
    qmax = qmin + tq - 1
    if kind == "diff":
        n_int = ((qmin // CHUNK + 1) * CHUNK) // tk
        last = ((qmax // CHUNK + 1) * CHUNK - 1) // tk
    else:
        n_int = qmin // tk
        last = (qmax - 1) // tk
    smaller = min if isinstance(i, int) else jnp.minimum
    return smaller(n_int, n_k), smaller(last + 1, n_k), qmin


def _split_maps(q_ref, qm_ref, hb):
    lane = lax.broadcasted_iota(jnp.int32, (1, HEAD_W), 1)
    for h in range(hb):
        q = q_ref[0, :, h * HEAD_W:(h + 1) * HEAD_W]
        zero = jnp.zeros_like(q)
        qm_ref[2 * h] = jnp.where(lane < MAP_W, q, zero)
        qm_ref[2 * h + 1] = jnp.where(lane >= MAP_W, q, zero)


def _diff_block(qm_ref, acc_ref, kbs, vtbs, valid, carry):
    carry, probs, alphas = _diff_softmax(_diff_scores(qm_ref, kbs), valid, carry)
    _diff_values(acc_ref, vtbs, probs, alphas)
    return carry


def _diff_scores(qm_ref, kbs):
    return [lax.dot_general(kbs[c // 2], qm_ref[c], (((1,), (1,)), ((), ())), preferred_element_type=F32)
            for c in range(2 * len(kbs))]


def _diff_softmax(scores, valid, carry):
    new, probs, alphas = [], [], []
    for c, s in enumerate(scores):
        m, l = carry[2 * c], carry[2 * c + 1]
        if valid is not None:
            s = jnp.where(valid, s, NEG)
        m_new = jnp.maximum(m, jnp.max(s, axis=0, keepdims=True))
        alpha = jnp.exp2(m - m_new)
        p = jnp.exp2(s - m_new)
        new += [m_new, alpha * l + jnp.sum(p, axis=0, keepdims=True)]
        probs.append(p.astype(BF16))
        alphas.append(alpha)
    return tuple(new), probs, alphas


def _diff_values(acc_ref, vtbs, probs, alphas):
    for c, (p, alpha) in enumerate(zip(probs, alphas)):
        acc_ref[c] = alpha * acc_ref[c] + jnp.dot(vtbs[c // 2], p, preferred_element_type=F32)


def _diff_init(acc_ref, tq, hb):
    acc_ref[...] = jnp.zeros_like(acc_ref)
    return (jnp.full((1, tq), NEG, F32), jnp.zeros((1, tq), F32)) * (2 * hb)


def _diff_finish(carry, acc_ref, lam_refs, g_ref, lam_init, o_ref, hb):
    lq1_ref, lk1_ref, lq2_ref, lk2_ref = lam_refs
    lam = (jnp.exp(jnp.sum(lq1_ref[...] * lk1_ref[...], keepdims=True))
           - jnp.exp(jnp.sum(lq2_ref[...] * lk2_ref[...], keepdims=True)) + lam_init)
    for h in range(hb):
        l0, l1 = carry[4 * h + 1], carry[4 * h + 3]
        ot = acc_ref[2 * h] * (1.0 / l0) - lam * (acc_ref[2 * h + 1] * (1.0 / l1))
        ms = jnp.mean(ot * ot, axis=0, keepdims=True)
        ot = ot * lax.rsqrt(ms + EPS) * g_ref[...] * (1.0 - lam_init)
        o_ref[0, :, h * HEAD_W:(h + 1) * HEAD_W] = ot.T.astype(BF16)


def _diff_attn_kernel(tq, tk, n_k, q_off, lam_init, hb,
                      q_ref, k_ref, vt_ref, lq1_ref, lk1_ref, lq2_ref, lk2_ref, g_ref,
                      o_ref, qm_ref, acc_ref):
    n_int, n_tot, qmin = _kv_range("diff", pl.program_id(2), tq, tk, n_k, q_off)
    _split_maps(q_ref, qm_ref, hb)
    q_chunk = (qmin + lax.broadcasted_iota(jnp.int32, (1, tq), 1)) >> CHUNK_SHIFT

    def step(masked, j, carry):
        k0 = pl.multiple_of(j * tk, tk)
        valid = None
        if masked:
            k_chunk = (k0 + lax.broadcasted_iota(jnp.int32, (tk, 1), 0)) >> CHUNK_SHIFT
            valid = k_chunk <= q_chunk
        kbs = [k_ref[0, pl.ds(k0, tk), h * HEAD_W:(h + 1) * HEAD_W] for h in range(hb)]
        vtbs = [vt_ref[0, h, j] for h in range(hb)]
        return _diff_block(qm_ref, acc_ref, kbs, vtbs, valid, carry)

    carry = _diff_init(acc_ref, tq, hb)
    carry = lax.fori_loop(0, n_int, functools.partial(step, False), carry)
    carry = lax.fori_loop(n_int, n_tot, functools.partial(step, True), carry)
    _diff_finish(carry, acc_ref, (lq1_ref, lk1_ref, lq2_ref, lk2_ref), g_ref, lam_init, o_ref, hb)


def _suffix_matrix(tk):
    s = (lax.broadcasted_iota(jnp.int32, (tk, tk), 1)
         >= lax.broadcasted_iota(jnp.int32, (tk, tk), 0)).astype(BF16)
    return jnp.concatenate([s, s], axis=1)


def _sb_block(q_ref, acc_ref, suffix2, kbs, vtbs, valid, later):
    heads = range(len(kbs))
    ws = [lax.dot_general(kbs[h], q_ref[0, :, h * HEAD_W:(h + 1) * HEAD_W], (((1,), (1,)), ((), ())),
                          preferred_element_type=F32) for h in heads]
    tails = []
    for h in heads:
        sp = jnp.maximum(ws[h], 0.0) + jnp.log(1.0 + jnp.exp2(-jnp.abs(ws[h]))) * LOG2E
        if valid is not None:
            sp = jnp.where(valid, sp, 0.0)
        hi = sp.astype(BF16)
        lo = (sp - hi.astype(F32)).astype(BF16)
        tails.append(jnp.dot(suffix2, jnp.concatenate([hi, lo], axis=0),
                             preferred_element_type=F32))
    weights = []
    for h in heads:
        logit = ws[h] - tails[h] - later[h]
        if valid is not None:
            logit = jnp.where(valid, logit, NEG)
        weights.append(jnp.exp2(logit).astype(BF16))
    for h in heads:
        acc_ref[h] += jnp.dot(vtbs[h], weights[h], preferred_element_type=F32)
    return tuple(later[h] + tails[h][0:1, :] for h in heads)


def _sb_walk(step, first, last, later):
    def alive(later):
        return (jnp.min(functools.reduce(jnp.minimum, later)) < SB_DEAD_LOG2).astype(jnp.int32)

    def cond(state):
        return jnp.logical_and(state[0] < last, state[1] > 0)

    def body(state):
        later = step(state[0], state[2])
        return state[0] + 1, alive(later), later

    return lax.while_loop(cond, body, (jnp.asarray(first, jnp.int32), alive(later), later))[2]


def _sb_finish(acc_ref, o_ref, hb):
    for h in range(hb):
        o_ref[0, :, h * HEAD_W:(h + 1) * HEAD_W] = acc_ref[h].T.astype(BF16)


def _sb_attn_kernel(tq, tk, n_k, q_off, hb, q_ref, k_ref, vt_ref, o_ref, acc_ref):
    n_int, n_tot, qmin = _kv_range("sb", pl.program_id(2), tq, tk, n_k, q_off)
    q_pos = qmin + lax.broadcasted_iota(jnp.int32, (1, tq), 1)
    suffix2 = _suffix_matrix(tk)
    acc_ref[...] = jnp.zeros_like(acc_ref)

    def step(masked, t, later):
        j = n_tot - 1 - t
        k0 = pl.multiple_of(j * tk, tk)
        valid = (k0 + lax.broadcasted_iota(jnp.int32, (tk, 1), 0)) < q_pos if masked else None
        kbs = [k_ref[0, pl.ds(k0, tk), h * HEAD_W:(h + 1) * HEAD_W] for h in range(hb)]
        vtbs = [vt_ref[0, h, j] for h in range(hb)]
        return _sb_block(q_ref, acc_ref, suffix2, kbs, vtbs, valid, later)

    later = (jnp.zeros((1, tq), F32),) * hb
    later = lax.fori_loop(0, n_tot - n_int, functools.partial(step, True), later)
    _sb_walk(functools.partial(step, False), n_tot - n_int, n_tot, later)
    _sb_finish(acc_ref, o_ref, hb)


def _attention(kind, q, k, vt, q_off, extra, cfg):
    b, t_q, width = q.shape
    n_k, tk = vt.shape[2], vt.shape[4]
    t_k = k.shape[1]
    assert t_k == n_k * tk
    tq = _tile(t_q, DIFF_Q_TILE if kind == "diff" else SB_Q_TILE)
    hb = DIFF_HEADS_PER_STEP if kind == "diff" else SB_HEADS_PER_STEP
    kv_mode = dict(pipeline_mode=pl.Buffered(1)) if hb == N_HEADS else {}
    q_spec = pl.BlockSpec((1, tq, hb * HEAD_W), lambda bi, h, i: (bi, i, h))
    in_specs = [q_spec,
                pl.BlockSpec((1, t_k, hb * HEAD_W), lambda bi, h, i: (bi, 0, h), **kv_mode),
                pl.BlockSpec((1, hb, n_k, HEAD_W, tk), lambda bi, h, i: (bi, h, 0, 0, 0), **kv_mode)]
    for e in extra:
        in_specs.append(pl.BlockSpec(e.shape, lambda bi, h, i: (0, 0)))
    if kind == "diff":
        body = functools.partial(_diff_attn_kernel, tq, tk, n_k, q_off, cfg, hb)
        scratch = [pltpu.VMEM((2 * hb, tq, HEAD_W), BF16), pltpu.VMEM((2 * hb, HEAD_W, tq), F32)]
    else:
        body = functools.partial(_sb_attn_kernel, tq, tk, n_k, q_off, hb)
        scratch = [pltpu.VMEM((hb, HEAD_W, tq), F32)]
    return pl.pallas_call(
        body,
        grid=(b, N_HEADS // hb, t_q // tq),
        in_specs=in_specs,
        out_specs=q_spec,
        out_shape=jax.ShapeDtypeStruct((b, t_q, width), BF16),
        scratch_shapes=scratch,
        compiler_params=_params("parallel", "parallel", "arbitrary"),
        name=kind + "_attn",
    )(q, k, vt, *extra)


def _past_operands(kp_ref, vp_ref, k0, tk):
    heads = range(N_HEADS)
    rows = lambda h: pl.ds(k0 * N_HEADS + h, tk, stride=N_HEADS)
    kbs = [kp_ref[0, rows(h), :].astype(BF16) for h in heads]
    vtbs = [vp_ref[0, rows(h), :].T.astype(BF16) for h in heads]
    return kbs, vtbs


def _new_operands(kn_ref, vn_ref):
    heads = range(N_HEADS)
    kbs = [kn_ref[0, :, h * HEAD_W:(h + 1) * HEAD_W].astype(BF16) for h in heads]
    vtbs = [vn_ref[0, :, h * HEAD_W:(h + 1) * HEAD_W].T.astype(BF16) for h in heads]
    return kbs, vtbs


def _new_positions(q_off, t_new, tq):
    k_idx = lax.broadcasted_iota(jnp.int32, (t_new, 1), 0)
    q_idx = lax.broadcasted_iota(jnp.int32, (1, tq), 1)
    return k_idx, q_off + k_idx, q_off + q_idx


def _diff_cached_kernel(tq, tk, n_past, q_off, t_real, lam_init,
                        q_ref, kp_ref, vp_ref, kn_ref, vn_ref, lq1_ref, lk1_ref, lq2_ref, lk2_ref, g_ref,
                        o_ref, qm_ref, acc_ref):
    _split_maps(q_ref, qm_ref, N_HEADS)
    carry = _diff_init(acc_ref, tq, N_HEADS)
    heads = range(N_HEADS)
    for j in range(n_past):
        kbs = [kp_ref[0, h, :, j * tk:(j + 1) * tk].T.astype(BF16) for h in heads]
        vtbs = [vp_ref[0, pl.ds(j * tk * N_HEADS + h, tk, stride=N_HEADS), :].T.astype(BF16) for h in heads]
        carry = _diff_block(qm_ref, acc_ref, kbs, vtbs, None, carry)
    k_idx, k_pos, q_pos = _new_positions(q_off, kn_ref.shape[1], tq)
    valid = ((k_pos >> CHUNK_SHIFT) <= (q_pos >> CHUNK_SHIFT)) & (k_idx < t_real)
    kbs, vtbs = _new_operands(kn_ref, vn_ref)
    carry = _diff_block(qm_ref, acc_ref, kbs, vtbs, valid, carry)
    _diff_finish(carry, acc_ref, (lq1_ref, lk1_ref, lq2_ref, lk2_ref), g_ref, lam_init, o_ref, N_HEADS)


def _sb_cached_kernel(tq, tk, n_past, q_off, t_real, q_ref, kp_ref, vp_ref, kn_ref, vn_ref, o_ref, acc_ref):
    acc_ref[...] = jnp.zeros_like(acc_ref)
    t_new = kn_ref.shape[1]
    k_idx, k_pos, q_pos = _new_positions(q_off, t_new, tq)
    valid = (k_pos < q_pos) & (k_idx < t_real)
    kbs, vtbs = _new_operands(kn_ref, vn_ref)
    later = (jnp.zeros((1, tq), F32),) * N_HEADS
    later = _sb_block(q_ref, acc_ref, _suffix_matrix(t_new), kbs, vtbs, valid, later)
    suffix2 = _suffix_matrix(tk)

    def past(t, later):
        kbs, vtbs = _past_operands(kp_ref, vp_ref, pl.multiple_of((n_past - 1 - t) * tk, tk), tk)
        return _sb_block(q_ref, acc_ref, suffix2, kbs, vtbs, None, later)

    _sb_walk(past, 0, n_past, later)
    _sb_finish(acc_ref, o_ref, N_HEADS)


def _attention_cached(kind, q, k_past, v_past, k_new, v_new, t_real, extra, lam_init):
    b, tq, width = q.shape
    p_len, t_new = v_past.shape[1] // N_HEADS, k_new.shape[1]
    tk = _tile(p_len, SB_K_TILE)
    n_past = p_len // tk
    row3 = lambda t: pl.BlockSpec((1, t, width), lambda bi: (bi, 0, 0))
    past_spec = pl.BlockSpec((1, p_len * N_HEADS, HEAD_W), lambda bi: (bi, 0, 0))
    k_spec = pl.BlockSpec((1, N_HEADS, HEAD_W, p_len), lambda bi: (bi, 0, 0, 0)) if kind == "diff" else past_spec
    in_specs = [row3(tq), k_spec, past_spec, row3(t_new), row3(t_new)]
    for e in extra:
        in_specs.append(pl.BlockSpec(e.shape, lambda bi: (0, 0)))
    if kind == "diff":
        body = functools.partial(_diff_cached_kernel, tq, tk, n_past, p_len, t_real, lam_init)
        scratch = [pltpu.VMEM((2 * N_HEADS, tq, HEAD_W), BF16), pltpu.VMEM((2 * N_HEADS, HEAD_W, tq), F32)]
    else:
        body = functools.partial(_sb_cached_kernel, tq, tk, n_past, p_len, t_real)
        scratch = [pltpu.VMEM((N_HEADS, HEAD_W, tq), F32)]
    return pl.pallas_call(
        body,
        grid=(b,),
        in_specs=in_specs,
        out_specs=row3(tq),
        out_shape=jax.ShapeDtypeStruct((b, tq, width), BF16),
        scratch_shapes=scratch,
        compiler_params=_params("parallel"),
        name=kind + "_attn_cached",
    )(q, k_past, v_past, k_new, v_new, *extra)


def _out_kernel(oa_ref, ob_ref, ga_ref, gb_ref, h_ref, wua_ref, wub_ref, wo_ref, n_ref, o_ref):
    pa = jnp.dot(oa_ref[...], wua_ref[...], preferred_element_type=F32)
    pb = jnp.dot(ob_ref[...], wub_ref[...], preferred_element_type=F32)
    merged = (ga_ref[...] * pa + gb_ref[...] * pb).astype(BF16)
    r = jnp.dot(merged, wo_ref[...], preferred_element_type=F32)
    o_ref[...] = h_ref[...] + _rms(r, n_ref[...])


def _merge_out(oa, ob, ga, gb, h, wua, wub, wo, n_post):
    n, d = h.shape
    w_attn = oa.shape[1]
    tm = _tile(n, OUT_TOKEN_TILE)
    row = lambda wd: pl.BlockSpec((tm, wd), lambda i: (i, 0))
    const = lambda shape: pl.BlockSpec(shape, lambda i: (0, 0), pipeline_mode=pl.Buffered(1))
    return pl.pallas_call(
        _out_kernel,
        grid=(n // tm,),
        in_specs=[row(w_attn), row(w_attn), row(d), row(d), row(d),
                  const(wua.shape), const(wub.shape), const(wo.shape), const(n_post.shape)],
        out_specs=row(d),
        out_shape=jax.ShapeDtypeStruct((n, d), F32),
        compiler_params=_params("parallel"),
        name="merge_out",
    )(oa, ob, ga, gb, h, wua, wub, wo, n_post)


def _rope_tables(pos):
    half = ROT_DIM // 2
    inv_freq = jnp.power(ROPE_THETA, -jnp.arange(0, ROT_DIM, 2, dtype=F32) / ROT_DIM)
    lane = jnp.arange(HEAD_W) % MAP_W
    ang = pos.astype(F32)[:, None] * inv_freq[lane % half][None, :]
    cos, sin = jnp.cos(ang), jnp.sin(ang)
    c = jnp.where(lane < ROT_DIM, cos, 1.0)
    sa = jnp.where(lane < half, -sin, 0.0)
    sb = jnp.where((lane >= half) & (lane < ROT_DIM), sin, 0.0)
    return c, sa, sb


def _cast_pad_kernel(f, axis, x_ref, o_ref):
    x = x_ref[...].astype(BF16)
    if axis == 1:
        o_ref[:, :f] = x
        o_ref[:, f:] = jnp.zeros((o_ref.shape[0], o_ref.shape[1] - f), BF16)
    else:
        o_ref[:f, :] = x
        o_ref[f:, :] = jnp.zeros((o_ref.shape[0] - f, o_ref.shape[1]), BF16)


def _pad_ffn(w, axis):
    f, other = w.shape[axis], w.shape[1 - axis]
    fp = -(-f // FFN_HIDDEN_TILE) * FFN_HIDDEN_TILE
    t = _tile(other, CAST_TILE)
    if axis == 1:
        in_block, out_block, index, out = (t, f), (t, fp), (lambda i: (i, 0)), (other, fp)
    else:
        in_block, out_block, index, out = (f, t), (fp, t), (lambda i: (0, i)), (fp, other)
    return pl.pallas_call(
        functools.partial(_cast_pad_kernel, f, axis),
        grid=(other // t,),
        in_specs=[pl.BlockSpec(in_block, index)],
        out_specs=pl.BlockSpec(out_block, index),
        out_shape=jax.ShapeDtypeStruct(out, BF16),
        compiler_params=_params("parallel"),
        name="cast_pad",
    )(w)


def _layer(x, pos, past, lam_init, p):
    b, t, d = x.shape
    n = b * t
    w_a = N_HEADS * HEAD_W
    vec = lambda v: v.reshape(1, -1)
    w_in = p["w_in"].astype(BF16)
    seg = lambda lo, hi: w_in[:, lo:hi]

    h1, xn = _ffn(x.reshape(n, d), vec(p["n1a"]), vec(p["n1b"]), vec(p["nma"]),
                  p["f1g"], p["f1u"], p["f1d"])

    tables = tuple(jnp.tile(m, (b, 1)) for m in _rope_tables(pos))
    w_qa, w_ka, w_va, w_qb, w_kb, w_vb = (seg(k * w_a, (k + 1) * w_a) for k in range(6))
    qa_part = (w_qa, "q_rope", MAP_W ** -0.5 * LOG2E, None)
    qb_part = (w_qb, "q", HEAD_W ** -0.5 * LOG2E, None)
    (g_a,), = _proj(xn, [(seg(6 * w_a, 6 * w_a + d), "gate", None, None)], tile=GATE_TOKEN_TILE)
    (g_b,), = _proj(xn, [(seg(6 * w_a + d, 6 * w_a + 2 * d), "gate", None, None)], tile=GATE_TOKEN_TILE)

    shape3 = lambda a: a.reshape(b, t, w_a)
    diff_extra = [vec(p["lq1"]), vec(p["lk1"]), vec(p["lq2"]), vec(p["lk2"]), p["subln_g"].reshape(-1, 1)]
    if past is None:
        (q_a,), (kat, ka16) = _proj(xn, [qa_part, (w_ka, "k_rope", None, None)], tables=tables, seq=t,
                                    tile=PAIR_TOKEN_TILE)
        (q_b,), (kb32, kb16) = _proj(xn, [qb_part, (w_kb, "k", None, None)], tile=PAIR_TOKEN_TILE)
        (va32, vat), (vb32, vbt) = _proj(xn, [(w_va, "v_t", None, _tile(t, DIFF_K_TILE)),
                                              (w_vb, "v_t", None, _tile(t, SB_K_TILE))],
                                         seq=t, tile=PAIR_TOKEN_TILE)
        k_a_rows = kat.reshape(b, N_HEADS, 2, MAP_W, t).transpose(0, 4, 1, 2, 3)
        o_a = _attention("diff", shape3(q_a), shape3(ka16), vat, 0, diff_extra, lam_init)
        o_b = _attention("sb", shape3(q_b), shape3(kb16), vbt, 0, [], None)
    else:
        (q_a,), (ka32,) = _proj(xn, [qa_part, (w_ka, "kv_rope", None, None)], tables=tables,
                                tile=PAIR_TOKEN_TILE)
        (q_b,), (kb32,) = _proj(xn, [qb_part, (w_kb, "kv", None, None)], tile=PAIR_TOKEN_TILE)
        (va32,), (vb32,) = _proj(xn, [(w_va, "kv", None, None), (w_vb, "kv", None, None)],
                                 tile=PAIR_TOKEN_TILE)
        k_a_rows = ka32.reshape(b, t, N_HEADS, 2, MAP_W)
        pka, pva, pkb, pvb = past
        t_p = -(-t // V7X_LANES) * V7X_LANES
        pad_rows = lambda a: jnp.pad(shape3(a), ((0, 0), (0, t_p - t), (0, 0)))
        by_head = lambda c: c.reshape(c.shape[0], c.shape[1] * N_HEADS, HEAD_W)
        pka_t = pka.transpose(0, 2, 3, 4, 1).reshape(pka.shape[0], N_HEADS, HEAD_W, pka.shape[1])
        o_a = _attention_cached("diff", pad_rows(q_a), pka_t, by_head(pva), pad_rows(ka32),
                                pad_rows(va32), t, diff_extra, lam_init)
        o_b = _attention_cached("sb", pad_rows(q_b), by_head(pkb), by_head(pvb), pad_rows(kb32),
                                pad_rows(vb32), t, [], None)
    o_a = o_a[:, :t].reshape(n, w_a)
    o_b = o_b[:, :t].reshape(n, w_a)

    h2 = _merge_out(o_a, o_b, g_a, g_b, h1, p["w_up_a"].astype(BF16), p["w_up_b"].astype(BF16),
                    p["w_o"].astype(BF16), vec(p["nmb"]))
    y = _ffn(h2, vec(p["n2a"]), vec(p["n2b"]), None,
             p["f2g"], p["f2u"], p["f2d"])
    rows = (k_a_rows, va32.reshape(b, t, N_HEADS, HEAD_W),
            kb32.reshape(b, t, N_HEADS, HEAD_W), vb32.reshape(b, t, N_HEADS, HEAD_W))
    return y.reshape(b, t, d), rows


def _stack(xs):
    return xs[0][None] if len(xs) == 1 else jnp.stack(xs)


def kernel(x_prompt, x_sample, cache_diff_k, cache_diff_v, cache_sb_k, cache_sb_v, w_in, w_up_a, w_up_b, w_o, lam_q1, lam_k1, lam_q2, lam_k2, subln_g, norm_ffn1_pre, norm_ffn1_post, norm_mix_pre, norm_mix_post, norm_ffn2_pre, norm_ffn2_post, ffn1_w_gate, ffn1_w_up, ffn1_w_down, ffn2_w_gate, ffn2_w_up, ffn2_w_down):
    depth = w_in.shape[0]
    pos_p = jnp.arange(x_prompt.shape[1], dtype=jnp.int32)
    pos_s = cache_diff_k.shape[2] + jnp.arange(x_sample.shape[1], dtype=jnp.int32)
    hp, hs = x_prompt, x_sample
    rows_p, rows_s = [], []
    for l in range(depth):
        lam_init = 0.8 - 0.6 * math.exp(-0.3 * l)
        p = dict(w_in=w_in[l], w_up_a=w_up_a[l], w_up_b=w_up_b[l], w_o=w_o[l],
                 lq1=lam_q1[l], lk1=lam_k1[l], lq2=lam_q2[l], lk2=lam_k2[l], subln_g=subln_g[l],
                 n1a=norm_ffn1_pre[l], n1b=norm_ffn1_post[l], nma=norm_mix_pre[l], nmb=norm_mix_post[l],
                 n2a=norm_ffn2_pre[l], n2b=norm_ffn2_post[l],
                 f1g=_pad_ffn(ffn1_w_gate[l], 1), f1u=_pad_ffn(ffn1_w_up[l], 1), f1d=_pad_ffn(ffn1_w_down[l], 0),
                 f2g=_pad_ffn(ffn2_w_gate[l], 1), f2u=_pad_ffn(ffn2_w_up[l], 1), f2d=_pad_ffn(ffn2_w_down[l], 0))
        hp, rp = _layer(hp, pos_p, None, lam_init, p)
        past = (cache_diff_k[l], cache_diff_v[l], cache_sb_k[l], cache_sb_v[l])
        hs, rs = _layer(hs, pos_s, past, lam_init, p)
        rows_p.append(rp)
        rows_s.append(rs)
    outs_p = [_stack([r[i] for r in rows_p]) for i in range(4)]
    outs_s = [_stack([r[i] for r in rows_s]) for i in range(4)]
    return (hp, hs, *outs_p, *outs_s)
```

```python
import functools
import math

import jax
import jax.numpy as jnp
from jax import lax
from jax.experimental import pallas as pl
from jax.experimental.pallas import tpu as pltpu

F32 = jnp.float32
BF16 = jnp.bfloat16

EPS = 1e-6
CHUNK = 64
CHUNK_SHIFT = CHUNK.bit_length() - 1
assert CHUNK == 1 << CHUNK_SHIFT
N_HEADS = 8
HEAD_W = 128
MAP_W = 64
ROT_DIM = 16
ROPE_THETA = 500000.0
NEG = -1e30

V7X_LANES = 128
V7X_MXU_DIM = 256
V7X_VMEM_BYTES = 64 * 1024 * 1024
VMEM_LIMIT_BYTES = V7X_VMEM_BYTES - 8 * 1024 * 1024

FFN_TOKEN_TILE = 512
FFN_HIDDEN_TILE = 512
PROJ_TOKEN_TILE = 1024
PAIR_TOKEN_TILE = 512
GATE_TOKEN_TILE = 1024
CAST_TILE = 256
OUT_TOKEN_TILE = 256
DIFF_Q_TILE = 512
SB_Q_TILE = 256
DIFF_K_TILE = 512
SB_K_TILE = V7X_MXU_DIM
DIFF_HEADS_PER_STEP = 4
SB_HEADS_PER_STEP = 8
LOG2E = math.log2(math.e)
F32_MIN_DENORMAL_LOG2 = -149
SB_DEAD_LOG2 = float(-F32_MIN_DENORMAL_LOG2 + 11)


def _tile(n, target):
    t = min(n, target)
    while n % t:
        t -= 1
    return t


def _params(*sem):
    return pltpu.CompilerParams(dimension_semantics=sem, vmem_limit_bytes=VMEM_LIMIT_BYTES)


def _rms(x, g):
    return x * lax.rsqrt(jnp.mean(x * x, axis=-1, keepdims=True) + EPS) * g


def _ffn_kernel(n_f, has_next, x_ref, npre_ref, npost_ref, *refs):
    if has_next:
        nnext_ref, wg_ref, wu_ref, wd_ref, h_ref, xn_ref, xs_ref, acc_ref = refs
    else:
        wg_ref, wu_ref, wd_ref, h_ref, xs_ref, acc_ref = refs
    f = pl.program_id(1)

    @pl.when(f == 0)
    def _():
        xs_ref[...] = _rms(x_ref[...], npre_ref[...]).astype(BF16)
        acc_ref[...] = jnp.zeros_like(acc_ref)

    xs = xs_ref[...]
    g = jnp.dot(xs, wg_ref[...], preferred_element_type=F32)
    u = jnp.dot(xs, wu_ref[...], preferred_element_type=F32)
    a = (g * jax.nn.sigmoid(g) * u).astype(BF16)
    acc_ref[...] += jnp.dot(a, wd_ref[...], preferred_element_type=F32)

    @pl.when(f == n_f - 1)
    def _():
        h = x_ref[...] + 0.5 * _rms(acc_ref[...], npost_ref[...])
        h_ref[...] = h
        if has_next:
            xn_ref[...] = _rms(h, nnext_ref[...]).astype(BF16)


def _ffn(x, n_pre, n_post, n_next, wg, wu, wd):
    n, d = x.shape
    fp = wg.shape[1]
    tm = _tile(n, FFN_TOKEN_TILE)
    tf = _tile(fp, FFN_HIDDEN_TILE)
    n_f = fp // tf
    has_next = n_next is not None
    row = pl.BlockSpec((tm, d), lambda i, f: (i, 0))
    vec = pl.BlockSpec((1, d), lambda i, f: (0, 0))
    in_specs = [row, vec, vec] + ([vec] if has_next else []) + [
        pl.BlockSpec((d, tf), lambda i, f: (0, f)),
        pl.BlockSpec((d, tf), lambda i, f: (0, f)),
        pl.BlockSpec((tf, d), lambda i, f: (f, 0)),
    ]
    out_shape = [jax.ShapeDtypeStruct((n, d), F32)]
    out_specs = [row]
    if has_next:
        out_shape.append(jax.ShapeDtypeStruct((n, d), BF16))
        out_specs.append(row)
    args = [x, n_pre, n_post] + ([n_next] if has_next else []) + [wg, wu, wd]
    res = pl.pallas_call(
        functools.partial(_ffn_kernel, n_f, has_next),
        grid=(n // tm, n_f),
        in_specs=in_specs,
        out_specs=out_specs,
        out_shape=out_shape,
        scratch_shapes=[pltpu.VMEM((tm, d), BF16), pltpu.VMEM((tm, d), F32)],
        compiler_params=_params("parallel", "arbitrary"),
        name="ffn_next" if has_next else "ffn",
    )(*args)
    return res if has_next else res[0]


def _rope(r, c, sa, sb):
    heads = []
    for h in range(r.shape[1] // HEAD_W):
        x = r[:, h * HEAD_W:(h + 1) * HEAD_W]
        up = pltpu.roll(x, HEAD_W - ROT_DIM // 2, 1)
        dn = pltpu.roll(x, ROT_DIM // 2, 1)
        heads.append(x * c + up * sa + dn * sb)
    return jnp.concatenate(heads, axis=1)


PROJ_OUTPUTS = {"q": 1, "q_rope": 1, "gate": 1, "k": 2, "k_rope": 2, "kv": 1, "kv_rope": 1, "v_t": 2}
PROJ_ROTARY = ("q_rope", "k_rope", "kv_rope")


def _proj_epilogue(mode, scale, tk, r, tables, refs):
    if mode in PROJ_ROTARY:
        r = _rope(r, *tables)
    if mode in ("q", "q_rope"):
        (q_ref,) = refs
        q_ref[...] = (r * scale).astype(BF16)
    elif mode == "gate":
        (g_ref,) = refs
        g_ref[...] = jax.nn.sigmoid(r)
    elif mode == "k":
        k32_ref, k16_ref = refs
        k32_ref[...] = r
        k16_ref[...] = r.astype(BF16)
    elif mode == "k_rope":
        kt_ref, k16_ref = refs
        k16_ref[...] = r.astype(BF16)
        for h in range(N_HEADS):
            kt_ref[0, h] = r[:, h * HEAD_W:(h + 1) * HEAD_W].T
    elif mode in ("kv", "kv_rope"):
        (r32_ref,) = refs
        r32_ref[...] = r
    else:
        v32_ref, vt_ref = refs
        v32_ref[...] = r
        for h in range(N_HEADS):
            for jb in range(r.shape[0] // tk):
                blk = r[jb * tk:(jb + 1) * tk, h * HEAD_W:(h + 1) * HEAD_W]
                vt_ref[0, h, jb] = blk.T.astype(BF16)


def _proj_kernel(parts, has_tables, x_ref, *refs):
    w_refs, refs = refs[:len(parts)], refs[len(parts):]
    tables = None
    if has_tables:
        tables, refs = tuple(t[...] for t in refs[:3]), refs[3:]
    x = x_ref[...]
    for (mode, scale, tk), w_ref in zip(parts, w_refs):
        outs, refs = refs[:PROJ_OUTPUTS[mode]], refs[PROJ_OUTPUTS[mode]:]
        _proj_epilogue(mode, scale, tk, jnp.dot(x, w_ref[...], preferred_element_type=F32), tables, outs)


def _proj(xn, parts, *, tables=None, seq=None, tile=PROJ_TOKEN_TILE):
    n, d = xn.shape
    tm = _tile(n if seq is None else seq, tile)
    row = lambda wd: pl.BlockSpec((tm, wd), lambda i: (i, 0))
    has_tables = any(mode in PROJ_ROTARY for _, mode, _, _ in parts)
    in_specs = [row(d)] + [pl.BlockSpec(w.shape, lambda i: (0, 0)) for w, _, _, _ in parts]
    args = [xn] + [w for w, _, _, _ in parts]
    if has_tables:
        in_specs += [row(HEAD_W)] * 3
        args += list(tables)
    out_shape, out_specs = [], []
    for w, mode, _, tk in parts:
        width = w.shape[1]
        if mode in ("q", "q_rope"):
            out_shape += [jax.ShapeDtypeStruct((n, width), BF16)]
            out_specs += [row(width)]
        elif mode in ("gate", "kv", "kv_rope"):
            out_shape += [jax.ShapeDtypeStruct((n, width), F32)]
            out_specs += [row(width)]
        elif mode == "k":
            out_shape += [jax.ShapeDtypeStruct((n, width), F32), jax.ShapeDtypeStruct((n, width), BF16)]
            out_specs += [row(width), row(width)]
        elif mode == "k_rope":
            assert seq % tm == 0
            per_seq = seq // tm
            out_shape += [jax.ShapeDtypeStruct((n // seq, N_HEADS, HEAD_W, seq), F32),
                          jax.ShapeDtypeStruct((n, width), BF16)]
            out_specs += [pl.BlockSpec((1, N_HEADS, HEAD_W, tm),
                                       lambda i, per_seq=per_seq: (i // per_seq, 0, 0, i % per_seq)),
                          row(width)]
        else:
            assert mode == "v_t" and tm % tk == 0 and seq % tm == 0
            per_seq = seq // tm
            out_shape += [jax.ShapeDtypeStruct((n, width), F32),
                          jax.ShapeDtypeStruct((n // seq, N_HEADS, seq // tk, HEAD_W, tk), BF16)]
            out_specs += [row(width),
                          pl.BlockSpec((1, N_HEADS, tm // tk, HEAD_W, tk),
                                       lambda i, per_seq=per_seq: (i // per_seq, 0, i % per_seq, 0, 0))]
    res = pl.pallas_call(
        functools.partial(_proj_kernel, tuple((mode, scale, tk) for _, mode, scale, tk in parts), has_tables),
        grid=(n // tm,),
        in_specs=in_specs,
        out_specs=out_specs,
        out_shape=out_shape,
        compiler_params=_params("parallel"),
        name="proj_" + "_".join(mode for _, mode, _, _ in parts),
    )(*args)
    out, k = [], 0
    for _, mode, _, _ in parts:
        out.append(tuple(res[k:k + PROJ_OUTPUTS[mode]]))
        k += PROJ_OUTPUTS[mode]
    return out


def _kv_range(kind, i, tq, tk, n_k, q_off):
    qmin = q_off + i * tq
    qmax = qmin + tq - 1
    if kind == "diff":
        n_int = ((qmin // CHUNK + 1) * CHUNK) // tk
        last = ((qmax // CHUNK + 1) * CHUNK - 1) // tk
    else:
        n_int = qmin // tk
        last = (qmax - 1) // tk
    smaller = min if isinstance(i, int) else jnp.minimum
    return smaller(n_int, n_k), smaller(last + 1, n_k), qmin


def _split_maps(q_ref, qm_ref, hb):
    lane = lax.broadcasted_iota(jnp.int32, (1, HEAD_W), 1)
    for h in range(hb):
        q = q_ref[0, :, h * HEAD_W:(h + 1) * HEAD_W]
        zero = jnp.zeros_like(q)
        qm_ref[2 * h] = jnp.where(lane < MAP_W, q, zero)
        qm_ref[2 * h + 1] = jnp.where(lane >= MAP_W, q, zero)


def _diff_block(qm_ref, acc_ref, kbs, vtbs, valid, carry):
    carry, probs, alphas = _diff_softmax(_diff_scores(qm_ref, kbs), valid, carry)
    _diff_values(acc_ref, vtbs, probs, alphas)
    return carry


def _diff_block_from(lo, qm_ref, acc_ref, kbs, vtbs, valid, carry):
    chains = range(2 * len(kbs))
    scores = [lax.dot_general(kbs[c // 2], qm_ref[c, lo:, :], (((1,), (1,)), ((), ())),
                              preferred_element_type=F32) for c in chains]
    part, probs, alphas = _diff_softmax(scores, valid, tuple(x[:, lo:] for x in carry))
    for c in chains:
        acc_ref[c, :, lo:] = alphas[c] * acc_ref[c, :, lo:] + jnp.dot(vtbs[c // 2], probs[c],
                                                                      preferred_element_type=F32)
    return tuple(jnp.concatenate([old[:, :lo], new], axis=1) for old, new in zip(carry, part))


def _diff_scores(qm_ref, kbs):
    return [lax.dot_general(kbs[c // 2], qm_ref[c], (((1,), (1,)), ((), ())), preferred_element_type=F32)
            for c in range(2 * len(kbs))]


def _diff_softmax(scores, valid, carry):
    new, probs, alphas = [], [], []
    for c, s in enumerate(scores):
        m, l = carry[2 * c], carry[2 * c + 1]
        if valid is not None:
            s = jnp.where(valid, s, NEG)
        m_new = jnp.maximum(m, jnp.max(s, axis=0, keepdims=True))
        alpha = jnp.exp2(m - m_new)
        p = jnp.exp2(s - m_new)
        new += [m_new, alpha * l + jnp.sum(p, axis=0, keepdims=True)]
        probs.append(p.astype(BF16))
        alphas.append(alpha)
    return tuple(new), probs, alphas


def _diff_values(acc_ref, vtbs, probs, alphas):
    for c, (p, alpha) in enumerate(zip(probs, alphas)):
        acc_ref[c] = alpha * acc_ref[c] + jnp.dot(vtbs[c // 2], p, preferred_element_type=F32)


def _diff_init(acc_ref, tq, hb):
    acc_ref[...] = jnp.zeros_like(acc_ref)
    return (jnp.full((1, tq), NEG, F32), jnp.zeros((1, tq), F32)) * (2 * hb)


def _diff_finish(carry, acc_ref, lam_refs, g_ref, lam_init, o_ref, hb):
    lq1_ref, lk1_ref, lq2_ref, lk2_ref = lam_refs
    lam = (jnp.exp(jnp.sum(lq1_ref[...] * lk1_ref[...], keepdims=True))
           - jnp.exp(jnp.sum(lq2_ref[...] * lk2_ref[...], keepdims=True)) + lam_init)
    for h in range(hb):
        l0, l1 = carry[4 * h + 1], carry[4 * h + 3]
        ot = acc_ref[2 * h] * (1.0 / l0) - lam * (acc_ref[2 * h + 1] * (1.0 / l1))
        ms = jnp.mean(ot * ot, axis=0, keepdims=True)
        ot = ot * lax.rsqrt(ms + EPS) * g_ref[...] * (1.0 - lam_init)
        o_ref[0, :, h * HEAD_W:(h + 1) * HEAD_W] = ot.T.astype(BF16)


def _diff_attn_kernel(tq, tk, n_k, q_off, lam_init, hb, own_block,
                      q_ref, k_ref, vt_ref, lq1_ref, lk1_ref, lq2_ref, lk2_ref, g_ref,
                      o_ref, qm_ref, acc_ref):
    n_int, n_tot, qmin = _kv_range("diff", pl.program_id(2), tq, tk, n_k, q_off)
    _split_maps(q_ref, qm_ref, hb)
    q_chunk = (qmin + lax.broadcasted_iota(jnp.int32, (1, tq), 1)) >> CHUNK_SHIFT

    def step(masked, j, carry):
        k0 = pl.multiple_of(j * tk, tk)
        valid = None
        if masked:
            k_chunk = (k0 + lax.broadcasted_iota(jnp.int32, (tk, 1), 0)) >> CHUNK_SHIFT
            valid = k_chunk <= q_chunk
        kbs = [k_ref[0, pl.ds(k0, tk), h * HEAD_W:(h + 1) * HEAD_W] for h in range(hb)]
        vtbs = [vt_ref[0, h, j] for h in range(hb)]
        return _diff_block(qm_ref, acc_ref, kbs, vtbs, valid, carry)

    def own(j, carry):
        half = tk // 2
        for part, lo in ((0, 0), (1, half)):
            k0 = pl.multiple_of(j * tk + part * half, half)
            k_chunk = (k0 + lax.broadcasted_iota(jnp.int32, (half, 1), 0)) >> CHUNK_SHIFT
            valid = k_chunk <= q_chunk[:, lo:]
            kbs = [k_ref[0, pl.ds(k0, half), h * HEAD_W:(h + 1) * HEAD_W] for h in range(hb)]
            vtbs = [vt_ref[0, h, j, :, part * half:(part + 1) * half] for h in range(hb)]
            if lo:
                carry = _diff_block_from(lo, qm_ref, acc_ref, kbs, vtbs, valid, carry)
            else:
                carry = _diff_block(qm_ref, acc_ref, kbs, vtbs, valid, carry)
        return carry

    carry = _diff_init(acc_ref, tq, hb)
    carry = lax.fori_loop(0, n_int, functools.partial(step, False), carry)
    if own_block:
        carry = own(n_int, carry)
    else:
        carry = lax.fori_loop(n_int, n_tot, functools.partial(step, True), carry)
    _diff_finish(carry, acc_ref, (lq1_ref, lk1_ref, lq2_ref, lk2_ref), g_ref, lam_init, o_ref, hb)


def _suffix_matrix(tk):
    s = (lax.broadcasted_iota(jnp.int32, (tk, tk), 1)
         >= lax.broadcasted_iota(jnp.int32, (tk, tk), 0)).astype(BF16)
    return jnp.concatenate([s, s], axis=1)


def _sb_block(q_ref, acc_ref, suffix2, kbs, vtbs, valid, later):
    heads = range(len(kbs))
    ws = [lax.dot_general(kbs[h], q_ref[0, :, h * HEAD_W:(h + 1) * HEAD_W], (((1,), (1,)), ((), ())),
                          preferred_element_type=F32) for h in heads]
    tails = []
    for h in heads:
        sp = jnp.maximum(ws[h], 0.0) + jnp.log(1.0 + jnp.exp2(-jnp.abs(ws[h]))) * LOG2E
        if valid is not None:
            sp = jnp.where(valid, sp, 0.0)
        hi = sp.astype(BF16)
        lo = (sp - hi.astype(F32)).astype(BF16)
        tails.append(jnp.dot(suffix2, jnp.concatenate([hi, lo], axis=0),
                             preferred_element_type=F32))
    weights = []
    for h in heads:
        logit = ws[h] - tails[h] - later[h]
        if valid is not None:
            logit = jnp.where(valid, logit, NEG)
        weights.append(jnp.exp2(logit).astype(BF16))
    for h in heads:
        acc_ref[h] += jnp.dot(vtbs[h], weights[h], preferred_element_type=F32)
    return tuple(later[h] + tails[h][0:1, :] for h in heads)


def _sb_walk(step, first, last, later):
    def alive(later):
        return (jnp.min(functools.reduce(jnp.minimum, later)) < SB_DEAD_LOG2).astype(jnp.int32)

    def cond(state):
        return jnp.logical_and(state[0] < last, state[1] > 0)

    def body(state):
        later = step(state[0], state[2])
        return state[0] + 1, alive(later), later

    return lax.while_loop(cond, body, (jnp.asarray(first, jnp.int32), alive(later), later))[2]


def _sb_finish(acc_ref, o_ref, hb):
    for h in range(hb):
        o_ref[0, :, h * HEAD_W:(h + 1) * HEAD_W] = acc_ref[h].T.astype(BF16)


def _sb_attn_kernel(tq, tk, n_k, q_off, hb, q_ref, k_ref, vt_ref, o_ref, acc_ref):
    n_int, n_tot, qmin = _kv_range("sb", pl.program_id(2), tq, tk, n_k, q_off)
    q_pos = qmin + lax.broadcasted_iota(jnp.int32, (1, tq), 1)
    suffix2 = _suffix_matrix(tk)
    acc_ref[...] = jnp.zeros_like(acc_ref)

    def step(masked, t, later):
        j = n_tot - 1 - t
        k0 = pl.multiple_of(j * tk, tk)
        valid = (k0 + lax.broadcasted_iota(jnp.int32, (tk, 1), 0)) < q_pos if masked else None
        kbs = [k_ref[0, pl.ds(k0, tk), h * HEAD_W:(h + 1) * HEAD_W] for h in range(hb)]
        vtbs = [vt_ref[0, h, j] for h in range(hb)]
        return _sb_block(q_ref, acc_ref, suffix2, kbs, vtbs, valid, later)

    later = (jnp.zeros((1, tq), F32),) * hb
    later = lax.fori_loop(0, n_tot - n_int, functools.partial(step, True), later)
    _sb_walk(functools.partial(step, False), n_tot - n_int, n_tot, later)
    _sb_finish(acc_ref, o_ref, hb)


def _attention(kind, q, k, vt, q_off, extra, cfg):
    b, t_q, width = q.shape
    n_k, tk = vt.shape[2], vt.shape[4]
    t_k = k.shape[1]
    assert t_k == n_k * tk
    tq = _tile(t_q, DIFF_Q_TILE if kind == "diff" else SB_Q_TILE)
    hb = DIFF_HEADS_PER_STEP if kind == "diff" else SB_HEADS_PER_STEP
    kv_mode = dict(pipeline_mode=pl.Buffered(1)) if hb == N_HEADS else {}
    q_spec = pl.BlockSpec((1, tq, hb * HEAD_W), lambda bi, h, i: (bi, i, h))
    in_specs = [q_spec,
                pl.BlockSpec((1, t_k, hb * HEAD_W), lambda bi, h, i: (bi, 0, h), **kv_mode),
                pl.BlockSpec((1, hb, n_k, HEAD_W, tk), lambda bi, h, i: (bi, h, 0, 0, 0), **kv_mode)]
    for e in extra:
        in_specs.append(pl.BlockSpec(e.shape, lambda bi, h, i: (0, 0)))
    if kind == "diff":
        def is_own(i):
            n_int, n_tot, qmin = _kv_range("diff", i, tq, tk, n_k, q_off)
            return n_tot - n_int == 1 and n_int * tk == qmin

        own_block = (tk == tq and (tk // 2) % CHUNK == 0 and all(is_own(i) for i in range(t_q // tq)))
        body = functools.partial(_diff_attn_kernel, tq, tk, n_k, q_off, cfg, hb, own_block)
        scratch = [pltpu.VMEM((2 * hb, tq, HEAD_W), BF16), pltpu.VMEM((2 * hb, HEAD_W, tq), F32)]
    else:
        body = functools.partial(_sb_attn_kernel, tq, tk, n_k, q_off, hb)
        scratch = [pltpu.VMEM((hb, HEAD_W, tq), F32)]
    return pl.pallas_call(
        body,
        grid=(b, N_HEADS // hb, t_q // tq),
        in_specs=in_specs,
        out_specs=q_spec,
        out_shape=jax.ShapeDtypeStruct((b, t_q, width), BF16),
        scratch_shapes=scratch,
        compiler_params=_params("parallel", "parallel", "arbitrary"),
        name=kind + "_attn",
    )(q, k, vt, *extra)


def _past_operands(kp_ref, vp_ref, k0, tk):
    heads = range(N_HEADS)
    rows = lambda h: pl.ds(k0 * N_HEADS + h, tk, stride=N_HEADS)
    kbs = [kp_ref[0, rows(h), :].astype(BF16) for h in heads]
    vtbs = [vp_ref[0, rows(h), :].T.astype(BF16) for h in heads]
    return kbs, vtbs


def _new_operands(kn_ref, vn_ref):
    heads = range(N_HEADS)
    kbs = [kn_ref[0, :, h * HEAD_W:(h + 1) * HEAD_W].astype(BF16) for h in heads]
    vtbs = [vn_ref[0, :, h * HEAD_W:(h + 1) * HEAD_W].T.astype(BF16) for h in heads]
    return kbs, vtbs


def _new_positions(q_off, t_new, tq):
    k_idx = lax.broadcasted_iota(jnp.int32, (t_new, 1), 0)
    q_idx = lax.broadcasted_iota(jnp.int32, (1, tq), 1)
    return k_idx, q_off + k_idx, q_off + q_idx


def _diff_cached_kernel(tq, tk, n_past, q_off, t_real, lam_init,
                        q_ref, kp_ref, vp_ref, kn_ref, vn_ref, lq1_ref, lk1_ref, lq2_ref, lk2_ref, g_ref,
                        o_ref, qm_ref, acc_ref):
    _split_maps(q_ref, qm_ref, N_HEADS)
    carry = _diff_init(acc_ref, tq, N_HEADS)
    heads = range(N_HEADS)
    for j in range(n_past):
        kbs = [kp_ref[0, h, :, j * tk:(j + 1) * tk].T.astype(BF16) for h in heads]
        vtbs = [vp_ref[0, pl.ds(j * tk * N_HEADS + h, tk, stride=N_HEADS), :].T.astype(BF16) for h in heads]
        carry = _diff_block(qm_ref, acc_ref, kbs, vtbs, None, carry)
    k_idx, k_pos, q_pos = _new_positions(q_off, kn_ref.shape[1], tq)
    valid = ((k_pos >> CHUNK_SHIFT) <= (q_pos >> CHUNK_SHIFT)) & (k_idx < t_real)
    kbs, vtbs = _new_operands(kn_ref, vn_ref)
    carry = _diff_block(qm_ref, acc_ref, kbs, vtbs, valid, carry)
    _diff_finish(carry, acc_ref, (lq1_ref, lk1_ref, lq2_ref, lk2_ref), g_ref, lam_init, o_ref, N_HEADS)


def _sb_cached_kernel(tq, tk, n_past, q_off, t_real, q_ref, kp_ref, vp_ref, kn_ref, vn_ref, o_ref, acc_ref):
    acc_ref[...] = jnp.zeros_like(acc_ref)
    t_new = kn_ref.shape[1]
    k_idx, k_pos, q_pos = _new_positions(q_off, t_new, tq)
    valid = (k_pos < q_pos) & (k_idx < t_real)
    kbs, vtbs = _new_operands(kn_ref, vn_ref)
    later = (jnp.zeros((1, tq), F32),) * N_HEADS
    later = _sb_block(q_ref, acc_ref, _suffix_matrix(t_new), kbs, vtbs, valid, later)
    suffix2 = _suffix_matrix(tk)

    def past(t, later):
        kbs, vtbs = _past_operands(kp_ref, vp_ref, pl.multiple_of((n_past - 1 - t) * tk, tk), tk)
        return _sb_block(q_ref, acc_ref, suffix2, kbs, vtbs, None, later)

    _sb_walk(past, 0, n_past, later)
    _sb_finish(acc_ref, o_ref, N_HEADS)


def _attention_cached(kind, q, k_past, v_past, k_new, v_new, t_real, extra, lam_init):
    b, tq, width = q.shape
    p_len, t_new = v_past.shape[1] // N_HEADS, k_new.shape[1]
    tk = _tile(p_len, SB_K_TILE)
    n_past = p_len // tk
    row3 = lambda t: pl.BlockSpec((1, t, width), lambda bi: (bi, 0, 0))
    past_spec = pl.BlockSpec((1, p_len * N_HEADS, HEAD_W), lambda bi: (bi, 0, 0))
    k_spec = pl.BlockSpec((1, N_HEADS, HEAD_W, p_len), lambda bi: (bi, 0, 0, 0)) if kind == "diff" else past_spec
    in_specs = [row3(tq), k_spec, past_spec, row3(t_new), row3(t_new)]
    for e in extra:
        in_specs.append(pl.BlockSpec(e.shape, lambda bi: (0, 0)))
    if kind == "diff":
        body = functools.partial(_diff_cached_kernel, tq, tk, n_past, p_len, t_real, lam_init)
        scratch = [pltpu.VMEM((2 * N_HEADS, tq, HEAD_W), BF16), pltpu.VMEM((2 * N_HEADS, HEAD_W, tq), F32)]
    else:
        body = functools.partial(_sb_cached_kernel, tq, tk, n_past, p_len, t_real)
        scratch = [pltpu.VMEM((N_HEADS, HEAD_W, tq), F32)]
    return pl.pallas_call(
        body,
        grid=(b,),
        in_specs=in_specs,
        out_specs=row3(tq),
        out_shape=jax.ShapeDtypeStruct((b, tq, width), BF16),
        scratch_shapes=scratch,
        compiler_params=_params("parallel"),
        name=kind + "_attn_cached",
    )(q, k_past, v_past, k_new, v_new, *extra)


def _out_kernel(oa_ref, ob_ref, ga_ref, gb_ref, h_ref, wua_ref, wub_ref, wo_ref, n_ref, o_ref):
    pa = jnp.dot(oa_ref[...], wua_ref[...], preferred_element_type=F32)
    pb = jnp.dot(ob_ref[...], wub_ref[...], preferred_element_type=F32)
    merged = (ga_ref[...] * pa + gb_ref[...] * pb).astype(BF16)
    r = jnp.dot(merged, wo_ref[...], preferred_element_type=F32)
    o_ref[...] = h_ref[...] + _rms(r, n_ref[...])


def _merge_out(oa, ob, ga, gb, h, wua, wub, wo, n_post):
    n, d = h.shape
    w_attn = oa.shape[1]
    tm = _tile(n, OUT_TOKEN_TILE)
    row = lambda wd: pl.BlockSpec((tm, wd), lambda i: (i, 0))
    const = lambda shape: pl.BlockSpec(shape, lambda i: (0, 0), pipeline_mode=pl.Buffered(1))
    return pl.pallas_call(
        _out_kernel,
        grid=(n // tm,),
        in_specs=[row(w_attn), row(w_attn), row(d), row(d), row(d),
                  const(wua.shape), const(wub.shape), const(wo.shape), const(n_post.shape)],
        out_specs=row(d),
        out_shape=jax.ShapeDtypeStruct((n, d), F32),
        compiler_params=_params("parallel"),
        name="merge_out",
    )(oa, ob, ga, gb, h, wua, wub, wo, n_post)


def _rope_tables(pos):
    half = ROT_DIM // 2
    inv_freq = jnp.power(ROPE_THETA, -jnp.arange(0, ROT_DIM, 2, dtype=F32) / ROT_DIM)
    lane = jnp.arange(HEAD_W) % MAP_W
    ang = pos.astype(F32)[:, None] * inv_freq[lane % half][None, :]
    cos, sin = jnp.cos(ang), jnp.sin(ang)
    c = jnp.where(lane < ROT_DIM, cos, 1.0)
    sa = jnp.where(lane < half, -sin, 0.0)
    sb = jnp.where((lane >= half) & (lane < ROT_DIM), sin, 0.0)
    return c, sa, sb


def _cast_pad_kernel(f, axis, x_ref, o_ref):
    x = x_ref[...].astype(BF16)
    if axis == 1:
        o_ref[:, :f] = x
        o_ref[:, f:] = jnp.zeros((o_ref.shape[0], o_ref.shape[1] - f), BF16)
    else:
        o_ref[:f, :] = x
        o_ref[f:, :] = jnp.zeros((o_ref.shape[0] - f, o_ref.shape[1]), BF16)


def _pad_ffn(w, axis):
    f, other = w.shape[axis], w.shape[1 - axis]
    fp = -(-f // FFN_HIDDEN_TILE) * FFN_HIDDEN_TILE
    t = _tile(other, CAST_TILE)
    if axis == 1:
        in_block, out_block, index, out = (t, f), (t, fp), (lambda i: (i, 0)), (other, fp)
    else:
        in_block, out_block, index, out = (f, t), (fp, t), (lambda i: (0, i)), (fp, other)
    return pl.pallas_call(
        functools.partial(_cast_pad_kernel, f, axis),
        grid=(other // t,),
        in_specs=[pl.BlockSpec(in_block, index)],
        out_specs=pl.BlockSpec(out_block, index),
        out_shape=jax.ShapeDtypeStruct(out, BF16),
        compiler_params=_params("parallel"),
        name="cast_pad",
    )(w)


def _layer(x, pos, past, lam_init, p):
    b, t, d = x.shape
    n = b * t
    w_a = N_HEADS * HEAD_W
    vec = lambda v: v.reshape(1, -1)
    w_in = p["w_in"].astype(BF16)
    seg = lambda lo, hi: w_in[:, lo:hi]

    h1, xn = _ffn(x.reshape(n, d), vec(p["n1a"]), vec(p["n1b"]), vec(p["nma"]),
                  p["f1g"], p["f1u"], p["f1d"])

    tables = tuple(jnp.tile(m, (b, 1)) for m in _rope_tables(pos))
    w_qa, w_ka, w_va, w_qb, w_kb, w_vb = (seg(k * w_a, (k + 1) * w_a) for k in range(6))
    qa_part = (w_qa, "q_rope", MAP_W ** -0.5 * LOG2E, None)
    qb_part = (w_qb, "q", HEAD_W ** -0.5 * LOG2E, None)
    (g_a,), = _proj(xn, [(seg(6 * w_a, 6 * w_a + d), "gate", None, None)], tile=GATE_TOKEN_TILE)
    (g_b,), = _proj(xn, [(seg(6 * w_a + d, 6 * w_a + 2 * d), "gate", None, None)], tile=GATE_TOKEN_TILE)

    shape3 = lambda a: a.reshape(b, t, w_a)
    diff_extra = [vec(p["lq1"]), vec(p["lk1"]), vec(p["lq2"]), vec(p["lk2"]), p["subln_g"].reshape(-1, 1)]
    if past is None:
        (q_a,), (kat, ka16) = _proj(xn, [qa_part, (w_ka, "k_rope", None, None)], tables=tables, seq=t,
                                    tile=PAIR_TOKEN_TILE)
        (q_b,), (kb32, kb16) = _proj(xn, [qb_part, (w_kb, "k", None, None)], tile=PAIR_TOKEN_TILE)
        (va32, vat), (vb32, vbt) = _proj(xn, [(w_va, "v_t", None, _tile(t, DIFF_K_TILE)),
                                              (w_vb, "v_t", None, _tile(t, SB_K_TILE))],
                                         seq=t, tile=PAIR_TOKEN_TILE)
        k_a_rows = kat.reshape(b, N_HEADS, 2, MAP_W, t).transpose(0, 4, 1, 2, 3)
        o_a = _attention("diff", shape3(q_a), shape3(ka16), vat, 0, diff_extra, lam_init)
        o_b = _attention("sb", shape3(q_b), shape3(kb16), vbt, 0, [], None)
    else:
        (q_a,), (ka32,) = _proj(xn, [qa_part, (w_ka, "kv_rope", None, None)], tables=tables,
                                tile=PAIR_TOKEN_TILE)
        (q_b,), (kb32,) = _proj(xn, [qb_part, (w_kb, "kv", None, None)], tile=PAIR_TOKEN_TILE)
        (va32,), (vb32,) = _proj(xn, [(w_va, "kv", None, None), (w_vb, "kv", None, None)],
                                 tile=PAIR_TOKEN_TILE)
        k_a_rows = ka32.reshape(b, t, N_HEADS, 2, MAP_W)
        pka, pva, pkb, pvb = past
        t_p = -(-t // V7X_LANES) * V7X_LANES
        pad_rows = lambda a: jnp.pad(shape3(a), ((0, 0), (0, t_p - t), (0, 0)))
        by_head = lambda c: c.reshape(c.shape[0], c.shape[1] * N_HEADS, HEAD_W)
        pka_t = pka.transpose(0, 2, 3, 4, 1).reshape(pka.shape[0], N_HEADS, HEAD_W, pka.shape[1])
        o_a = _attention_cached("diff", pad_rows(q_a), pka_t, by_head(pva), pad_rows(ka32),
                                pad_rows(va32), t, diff_extra, lam_init)
        o_b = _attention_cached("sb", pad_rows(q_b), by_head(pkb), by_head(pvb), pad_rows(kb32),
                                pad_rows(vb32), t, [], None)
    o_a = o_a[:, :t].reshape(n, w_a)
    o_b = o_b[:, :t].reshape(n, w_a)

    h2 = _merge_out(o_a, o_b, g_a, g_b, h1, p["w_up_a"].astype(BF16), p["w_up_b"].astype(BF16),
                    p["w_o"].astype(BF16), vec(p["nmb"]))
    y = _ffn(h2, vec(p["n2a"]), vec(p["n2b"]), None,
             p["f2g"], p["f2u"], p["f2d"])
    rows = (k_a_rows, va32.reshape(b, t, N_HEADS, HEAD_W),
            kb32.reshape(b, t, N_HEADS, HEAD_W), vb32.reshape(b, t, N_HEADS, HEAD_W))
    return y.reshape(b, t, d), rows


def _stack(xs):
    return xs[0][None] if len(xs) == 1 else jnp.stack(xs)


def kernel(x_prompt, x_sample, cache_diff_k, cache_diff_v, cache_sb_k, cache_sb_v, w_in, w_up_a, w_up_b, w_o, lam_q1, lam_k1, lam_q2, lam_k2, subln_g, norm_ffn1_pre, norm_ffn1_post, norm_mix_pre, norm_mix_post, norm_ffn2_pre, norm_ffn2_post, ffn1_w_gate, ffn1_w_up, ffn1_w_down, ffn2_w_gate, ffn2_w_up, ffn2_w_down):
    depth = w_in.shape[0]
    pos_p = jnp.arange(x_prompt.shape[1], dtype=jnp.int32)
    pos_s = cache_diff_k.shape[2] + jnp.arange(x_sample.shape[1], dtype=jnp.int32)
    hp, hs = x_prompt, x_sample
    rows_p, rows_s = [], []
    for l in range(depth):
        lam_init = 0.8 - 0.6 * math.exp(-0.3 * l)
        p = dict(w_in=w_in[l], w_up_a=w_up_a[l], w_up_b=w_up_b[l], w_o=w_o[l],
                 lq1=lam_q1[l], lk1=lam_k1[l], lq2=lam_q2[l], lk2=lam_k2[l], subln_g=subln_g[l],
                 n1a=norm_ffn1_pre[l], n1b=norm_ffn1_post[l], nma=norm_mix_pre[l], nmb=norm_mix_post[l],
                 n2a=norm_ffn2_pre[l], n2b=norm_ffn2_post[l],
                 f1g=_pad_ffn(ffn1_w_gate[l], 1), f1u=_pad_ffn(ffn1_w_up[l], 1), f1d=_pad_ffn(ffn1_w_down[l], 0),
                 f2g=_pad_ffn(ffn2_w_gate[l], 1), f2u=_pad_ffn(ffn2_w_up[l], 1), f2d=_pad_ffn(ffn2_w_down[l], 0))
        hp, rp = _layer(hp, pos_p, None, lam_init, p)
        past = (cache_diff_k[l], cache_diff_v[l], cache_sb_k[l], cache_sb_v[l])
        hs, rs = _layer(hs, pos_s, past, lam_init, p)
        rows_p.append(rp)
        rows_s.append(rs)
    outs_p = [_stack([r[i] for r in rows_p]) for i in range(4)]
    outs_s = [_stack([r[i] for r in rows_s]) for i in range(4)]
    return (hp, hs, *outs_p, *outs_s)
```

```python
import functools
import math

import jax
import jax.numpy as jnp
from jax import lax
from jax.experimental import pallas as pl
from jax.experimental.pallas import tpu as pltpu

F32 = jnp.float32
BF16 = jnp.bfloat16

EPS = 1e-6
CHUNK = 64
CHUNK_SHIFT = CHUNK.bit_length() - 1
assert CHUNK == 1 << CHUNK_SHIFT
N_HEADS = 8
HEAD_W = 128
MAP_W = 64
ROT_DIM = 16
ROPE_THETA = 500000.0
NEG = -1e30

V7X_LANES = 128
V7X_MXU_DIM = 256
V7X_VMEM_BYTES = 64 * 1024 * 1024
VMEM_LIMIT_BYTES = V7X_VMEM_BYTES - 8 * 1024 * 1024

FFN_TOKEN_TILE = 512
FFN_HIDDEN_TILE = 512
PROJ_TOKEN_TILE = 1024
PAIR_TOKEN_TILE = 512
GATE_TOKEN_TILE = 1024
CAST_TILE = 256
OUT_TOKEN_TILE = 256
DIFF_Q_TILE = 512
SB_Q_TILE = 256
DIFF_K_TILE = 512
SB_K_TILE = V7X_MXU_DIM
DIFF_HEADS_PER_STEP = 4
SB_HEADS_PER_STEP = 4
LOG2E = math.log2(math.e)
F32_MIN_DENORMAL_LOG2 = -149
SB_DEAD_LOG2 = float(-F32_MIN_DENORMAL_LOG2 + 11)


def _tile(n, target):
    t = min(n, target)
    while n % t:
        t -= 1
    return t


def _params(*sem):
    return pltpu.CompilerParams(dimension_semantics=sem, vmem_limit_bytes=VMEM_LIMIT_BYTES)


def _rms(x, g):
    return x * lax.rsqrt(jnp.mean(x * x, axis=-1, keepdims=True) + EPS) * g


def _ffn_kernel(n_f, has_next, x_ref, npre_ref, npost_ref, *refs):
    if has_next:
        nnext_ref, wg_ref, wu_ref, wd_ref, h_ref, xn_ref, xs_ref, acc_ref = refs
    else:
        wg_ref, wu_ref, wd_ref, h_ref, xs_ref, acc_ref = refs
    f = pl.program_id(1)

    @pl.when(f == 0)
    def _():
        xs_ref[...] = _rms(x_ref[...], npre_ref[...]).astype(BF16)
        acc_ref[...] = jnp.zeros_like(acc_ref)

    xs = xs_ref[...]
    g = jnp.dot(xs, wg_ref[...], preferred_element_type=F32)
    u = jnp.dot(xs, wu_ref[...], preferred_element_type=F32)
    a = (g * jax.nn.sigmoid(g) * u).astype(BF16)
    acc_ref[...] += jnp.dot(a, wd_ref[...], preferred_element_type=F32)

    @pl.when(f == n_f - 1)
    def _():
        h = x_ref[...] + 0.5 * _rms(acc_ref[...], npost_ref[...])
        h_ref[...] = h
        if has_next:
            xn_ref[...] = _rms(h, nnext_ref[...]).astype(BF16)


def _ffn(x, n_pre, n_post, n_next, wg, wu, wd):
    n, d = x.shape
    fp = wg.shape[1]
    tm = _tile(n, FFN_TOKEN_TILE)
    tf = _tile(fp, FFN_HIDDEN_TILE)
    n_f = fp // tf
    has_next = n_next is not None
    row = pl.BlockSpec((tm, d), lambda i, f: (i, 0))
    vec = pl.BlockSpec((1, d), lambda i, f: (0, 0))
    in_specs = [row, vec, vec] + ([vec] if has_next else []) + [
        pl.BlockSpec((d, tf), lambda i, f: (0, f)),
        pl.BlockSpec((d, tf), lambda i, f: (0, f)),
        pl.BlockSpec((tf, d), lambda i, f: (f, 0)),
    ]
    out_shape = [jax.ShapeDtypeStruct((n, d), F32)]
    out_specs = [row]
    if has_next:
        out_shape.append(jax.ShapeDtypeStruct((n, d), BF16))
        out_specs.append(row)
    args = [x, n_pre, n_post] + ([n_next] if has_next else []) + [wg, wu, wd]
    res = pl.pallas_call(
        functools.partial(_ffn_kernel, n_f, has_next),
        grid=(n // tm, n_f),
        in_specs=in_specs,
        out_specs=out_specs,
        out_shape=out_shape,
        scratch_shapes=[pltpu.VMEM((tm, d), BF16), pltpu.VMEM((tm, d), F32)],
        compiler_params=_params("parallel", "arbitrary"),
        name="ffn_next" if has_next else "ffn",
    )(*args)
    return res if has_next else res[0]


def _rope(r, c, sa, sb):
    heads = []
    for h in range(r.shape[1] // HEAD_W):
        x = r[:, h * HEAD_W:(h + 1) * HEAD_W]
        up = pltpu.roll(x, HEAD_W - ROT_DIM // 2, 1)
        dn = pltpu.roll(x, ROT_DIM // 2, 1)
        heads.append(x * c + up * sa + dn * sb)
    return jnp.concatenate(heads, axis=1)


PROJ_OUTPUTS = {"q": 1, "q_rope": 1, "gate": 1, "k": 2, "k_rope": 2, "kv": 1, "kv_rope": 1, "v_t": 2}
PROJ_ROTARY = ("q_rope", "k_rope", "kv_rope")


def _proj_epilogue(mode, scale, tk, r, tables, refs):
    if mode in PROJ_ROTARY:
        r = _rope(r, *tables)
    if mode in ("q", "q_rope"):
        (q_ref,) = refs
        q_ref[...] = (r * scale).astype(BF16)
    elif mode == "gate":
        (g_ref,) = refs
        g_ref[...] = jax.nn.sigmoid(r)
    elif mode == "k":
        k32_ref, k16_ref = refs
        k32_ref[...] = r
        k16_ref[...] = r.astype(BF16)
    elif mode == "k_rope":
        kt_ref, k16_ref = refs
        k16_ref[...] = r.astype(BF16)
        for h in range(N_HEADS):
            kt_ref[0, h] = r[:, h * HEAD_W:(h + 1) * HEAD_W].T
    elif mode in ("kv", "kv_rope"):
        (r32_ref,) = refs
        r32_ref[...] = r
    else:
        v32_ref, vt_ref = refs
        v32_ref[...] = r
        for h in range(N_HEADS):
            for jb in range(r.shape[0] // tk):
                blk = r[jb * tk:(jb + 1) * tk, h * HEAD_W:(h + 1) * HEAD_W]
                vt_ref[0, h, jb] = blk.T.astype(BF16)


def _proj_kernel(parts, has_tables, x_ref, *refs):
    w_refs, refs = refs[:len(parts)], refs[len(parts):]
    tables = None
    if has_tables:
        tables, refs = tuple(t[...] for t in refs[:3]), refs[3:]
    x = x_ref[...]
    for (mode, scale, tk), w_ref in zip(parts, w_refs):
        outs, refs = refs[:PROJ_OUTPUTS[mode]], refs[PROJ_OUTPUTS[mode]:]
        _proj_epilogue(mode, scale, tk, jnp.dot(x, w_ref[...], preferred_element_type=F32), tables, outs)


def _proj(xn, parts, *, tables=None, seq=None, tile=PROJ_TOKEN_TILE):
    n, d = xn.shape
    tm = _tile(n if seq is None else seq, tile)
    row = lambda wd: pl.BlockSpec((tm, wd), lambda i: (i, 0))
    has_tables = any(mode in PROJ_ROTARY for _, mode, _, _ in parts)
    in_specs = [row(d)] + [pl.BlockSpec(w.shape, lambda i: (0, 0)) for w, _, _, _ in parts]
    args = [xn] + [w for w, _, _, _ in parts]
    if has_tables:
        in_specs += [row(HEAD_W)] * 3
        args += list(tables)
    out_shape, out_specs = [], []
    for w, mode, _, tk in parts:
        width = w.shape[1]
        if mode in ("q", "q_rope"):
            out_shape += [jax.ShapeDtypeStruct((n, width), BF16)]
            out_specs += [row(width)]
        elif mode in ("gate", "kv", "kv_rope"):
            out_shape += [jax.ShapeDtypeStruct((n, width), F32)]
            out_specs += [row(width)]
        elif mode == "k":
            out_shape += [jax.ShapeDtypeStruct((n, width), F32), jax.ShapeDtypeStruct((n, width), BF16)]
            out_specs += [row(width), row(width)]
        elif mode == "k_rope":
            assert seq % tm == 0
            per_seq = seq // tm
            out_shape += [jax.ShapeDtypeStruct((n // seq, N_HEADS, HEAD_W, seq), F32),
                          jax.ShapeDtypeStruct((n, width), BF16)]
            out_specs += [pl.BlockSpec((1, N_HEADS, HEAD_W, tm),
                                       lambda i, per_seq=per_seq: (i // per_seq, 0, 0, i % per_seq)),
                          row(width)]
        else:
            assert mode == "v_t" and tm % tk == 0 and seq % tm == 0
            per_seq = seq // tm
            out_shape += [jax.ShapeDtypeStruct((n, width), F32),
                          jax.ShapeDtypeStruct((n // seq, N_HEADS, seq // tk, HEAD_W, tk), BF16)]
            out_specs += [row(width),
                          pl.BlockSpec((1, N_HEADS, tm // tk, HEAD_W, tk),
                                       lambda i, per_seq=per_seq: (i // per_seq, 0, i % per_seq, 0, 0))]
    res = pl.pallas_call(
        functools.partial(_proj_kernel, tuple((mode, scale, tk) for _, mode, scale, tk in parts), has_tables),
        grid=(n // tm,),
        in_specs=in_specs,
        out_specs=out_specs,
        out_shape=out_shape,
        compiler_params=_params("parallel"),
        name="proj_" + "_".join(mode for _, mode, _, _ in parts),
    )(*args)
    out, k = [], 0
    for _, mode, _, _ in parts:
        out.append(tuple(res[k:k + PROJ_OUTPUTS[mode]]))
        k += PROJ_OUTPUTS[mode]
    return out


def _kv_range(kind, i, tq, tk, n_k, q_off):
    qmin = q_off + i * tq
    qmax = qmin + tq - 1
    if kind == "diff":
        n_int = ((qmin // CHUNK + 1) * CHUNK) // tk
        last = ((qmax // CHUNK + 1) * CHUNK - 1) // tk
    else:
        n_int = qmin // tk
        last = (qmax - 1) // tk
    smaller = min if isinstance(i, int) else jnp.minimum
    return smaller(n_int, n_k), smaller(last + 1, n_k), qmin


def _split_maps(q_ref, qm_ref, hb):
    lane = lax.broadcasted_iota(jnp.int32, (1, HEAD_W), 1)
    for h in range(hb):
        q = q_ref[0, :, h * HEAD_W:(h + 1) * HEAD_W]
        zero = jnp.zeros_like(q)
        qm_ref[2 * h] = jnp.where(lane < MAP_W, q, zero)
        qm_ref[2 * h + 1] = jnp.where(lane >= MAP_W, q, zero)


def _diff_block(qm_ref, acc_ref, kbs, vtbs, valid, carry):
    carry, probs, alphas = _diff_softmax(_diff_scores(qm_ref, kbs), valid, carry)
    _diff_values(acc_ref, vtbs, probs, alphas)
    return carry


def _diff_block_from(lo, qm_ref, acc_ref, kbs, vtbs, valid, carry):
    chains = range(2 * len(kbs))
    scores = [lax.dot_general(kbs[c // 2], qm_ref[c, lo:, :], (((1,), (1,)), ((), ())),
                              preferred_element_type=F32) for c in chains]
    part, probs, alphas = _diff_softmax(scores, valid, tuple(x[:, lo:] for x in carry))
    for c in chains:
        acc_ref[c, :, lo:] = alphas[c] * acc_ref[c, :, lo:] + jnp.dot(vtbs[c // 2], probs[c],
                                                                      preferred_element_type=F32)
    return tuple(jnp.concatenate([old[:, :lo], new], axis=1) for old, new in zip(carry, part))


def _diff_scores(qm_ref, kbs):
    return [lax.dot_general(kbs[c // 2], qm_ref[c], (((1,), (1,)), ((), ())), preferred_element_type=F32)
            for c in range(2 * len(kbs))]


def _diff_softmax(scores, valid, carry):
    new, probs, alphas = [], [], []
    for c, s in enumerate(scores):
        m, l = carry[2 * c], carry[2 * c + 1]
        if valid is not None:
            s = jnp.where(valid, s, NEG)
        m_new = jnp.maximum(m, jnp.max(s, axis=0, keepdims=True))
        alpha = jnp.exp2(m - m_new)
        p = jnp.exp2(s - m_new)
        new += [m_new, alpha * l + jnp.sum(p, axis=0, keepdims=True)]
        probs.append(p.astype(BF16))
        alphas.append(alpha)
    return tuple(new), probs, alphas


def _diff_values(acc_ref, vtbs, probs, alphas):
    for c, (p, alpha) in enumerate(zip(probs, alphas)):
        acc_ref[c] = alpha * acc_ref[c] + jnp.dot(vtbs[c // 2], p, preferred_element_type=F32)


def _diff_init(acc_ref, tq, hb):
    acc_ref[...] = jnp.zeros_like(acc_ref)
    return (jnp.full((1, tq), NEG, F32), jnp.zeros((1, tq), F32)) * (2 * hb)


def _diff_finish(carry, acc_ref, lam_refs, g_ref, lam_init, o_ref, hb):
    lq1_ref, lk1_ref, lq2_ref, lk2_ref = lam_refs
    lam = (jnp.exp(jnp.sum(lq1_ref[...] * lk1_ref[...], keepdims=True))
           - jnp.exp(jnp.sum(lq2_ref[...] * lk2_ref[...], keepdims=True)) + lam_init)
    for h in range(hb):
        l0, l1 = carry[4 * h + 1], carry[4 * h + 3]
        ot = acc_ref[2 * h] * (1.0 / l0) - lam * (acc_ref[2 * h + 1] * (1.0 / l1))
        ms = jnp.mean(ot * ot, axis=0, keepdims=True)
        ot = ot * lax.rsqrt(ms + EPS) * g_ref[...] * (1.0 - lam_init)
        o_ref[0, :, h * HEAD_W:(h + 1) * HEAD_W] = ot.T.astype(BF16)


def _diff_attn_kernel(tq, tk, n_k, q_off, lam_init, hb, own_block,
                      q_ref, k_ref, vt_ref, lq1_ref, lk1_ref, lq2_ref, lk2_ref, g_ref,
                      o_ref, qm_ref, acc_ref):
    n_int, n_tot, qmin = _kv_range("diff", pl.program_id(2), tq, tk, n_k, q_off)
    _split_maps(q_ref, qm_ref, hb)
    q_chunk = (qmin + lax.broadcasted_iota(jnp.int32, (1, tq), 1)) >> CHUNK_SHIFT

    def step(masked, j, carry):
        k0 = pl.multiple_of(j * tk, tk)
        valid = None
        if masked:
            k_chunk = (k0 + lax.broadcasted_iota(jnp.int32, (tk, 1), 0)) >> CHUNK_SHIFT
            valid = k_chunk <= q_chunk
        kbs = [k_ref[0, pl.ds(k0, tk), h * HEAD_W:(h + 1) * HEAD_W] for h in range(hb)]
        vtbs = [vt_ref[0, h, j] for h in range(hb)]
        return _diff_block(qm_ref, acc_ref, kbs, vtbs, valid, carry)

    def own(j, carry):
        half = tk // 2
        for part, lo in ((0, 0), (1, half)):
            k0 = pl.multiple_of(j * tk + part * half, half)
            k_chunk = (k0 + lax.broadcasted_iota(jnp.int32, (half, 1), 0)) >> CHUNK_SHIFT
            valid = k_chunk <= q_chunk[:, lo:]
            kbs = [k_ref[0, pl.ds(k0, half), h * HEAD_W:(h + 1) * HEAD_W] for h in range(hb)]
            vtbs = [vt_ref[0, h, j, :, part * half:(part + 1) * half] for h in range(hb)]
            if lo:
                carry = _diff_block_from(lo, qm_ref, acc_ref, kbs, vtbs, valid, carry)
            else:
                carry = _diff_block(qm_ref, acc_ref, kbs, vtbs, valid, carry)
        return carry

    carry = _diff_init(acc_ref, tq, hb)
    carry = lax.fori_loop(0, n_int, functools.partial(step, False), carry)
    if own_block:
        carry = own(n_int, carry)
    else:
        carry = lax.fori_loop(n_int, n_tot, functools.partial(step, True), carry)
    _diff_finish(carry, acc_ref, (lq1_ref, lk1_ref, lq2_ref, lk2_ref), g_ref, lam_init, o_ref, hb)


def _suffix_matrix(tk):
    s = (lax.broadcasted_iota(jnp.int32, (tk, tk), 1)
         >= lax.broadcasted_iota(jnp.int32, (tk, tk), 0)).astype(BF16)
    return jnp.concatenate([s, s], axis=1)


def _sb_block(q_ref, acc_ref, suffix2, kbs, vtbs, valid, later):
    heads = range(len(kbs))
    ws = [lax.dot_general(kbs[h], q_ref[0, :, h * HEAD_W:(h + 1) * HEAD_W], (((1,), (1,)), ((), ())),
                          preferred_element_type=F32) for h in heads]
    tails = []
    for h in heads:
        sp = jnp.maximum(ws[h], 0.0) + jnp.log(1.0 + jnp.exp2(-jnp.abs(ws[h]))) * LOG2E
        if valid is not None:
            sp = jnp.where(valid, sp, 0.0)
        hi = sp.astype(BF16)
        lo = (sp - hi.astype(F32)).astype(BF16)
        tails.append(jnp.dot(suffix2, jnp.concatenate([hi, lo], axis=0),
                             preferred_element_type=F32))
    weights = []
    for h in heads:
        logit = ws[h] - tails[h] - later[h]
        if valid is not None:
            logit = jnp.where(valid, logit, NEG)
        weights.append(jnp.exp2(logit).astype(BF16))
    for h in heads:
        acc_ref[h] += jnp.dot(vtbs[h], weights[h], preferred_element_type=F32)
    return tuple(later[h] + tails[h][0:1, :] for h in heads)


def _sb_walk(step, first, last, later):
    def alive(later):
        return (jnp.min(functools.reduce(jnp.minimum, later)) < SB_DEAD_LOG2).astype(jnp.int32)

    def cond(state):
        return jnp.logical_and(state[0] < last, state[1] > 0)

    def body(state):
        later = step(state[0], state[2])
        return state[0] + 1, alive(later), later

    return lax.while_loop(cond, body, (jnp.asarray(first, jnp.int32), alive(later), later))[2]


def _sb_finish(acc_ref, o_ref, hb):
    for h in range(hb):
        o_ref[0, :, h * HEAD_W:(h + 1) * HEAD_W] = acc_ref[h].T.astype(BF16)


def _sb_attn_kernel(tq, tk, n_k, q_off, hb, q_ref, k_ref, vt_ref, o_ref, acc_ref):
    n_int, n_tot, qmin = _kv_range("sb", pl.program_id(2), tq, tk, n_k, q_off)
    q_pos = qmin + lax.broadcasted_iota(jnp.int32, (1, tq), 1)
    suffix2 = _suffix_matrix(tk)
    acc_ref[...] = jnp.zeros_like(acc_ref)

    def step(masked, t, later):
        j = n_tot - 1 - t
        k0 = pl.multiple_of(j * tk, tk)
        valid = (k0 + lax.broadcasted_iota(jnp.int32, (tk, 1), 0)) < q_pos if masked else None
        kbs = [k_ref[0, pl.ds(k0, tk), h * HEAD_W:(h + 1) * HEAD_W] for h in range(hb)]
        vtbs = [vt_ref[0, h, j] for h in range(hb)]
        return _sb_block(q_ref, acc_ref, suffix2, kbs, vtbs, valid, later)

    later = (jnp.zeros((1, tq), F32),) * hb
    later = lax.fori_loop(0, n_tot - n_int, functools.partial(step, True), later)
    _sb_walk(functools.partial(step, False), n_tot - n_int, n_tot, later)
    _sb_finish(acc_ref, o_ref, hb)


def _attention(kind, q, k, vt, q_off, extra, cfg):
    b, t_q, width = q.shape
    n_k, tk = vt.shape[2], vt.shape[4]
    t_k = k.shape[1]
    assert t_k == n_k * tk
    tq = _tile(t_q, DIFF_Q_TILE if kind == "diff" else SB_Q_TILE)
    hb = DIFF_HEADS_PER_STEP if kind == "diff" else SB_HEADS_PER_STEP
    kv_mode = dict(pipeline_mode=pl.Buffered(1)) if hb == N_HEADS else {}
    q_spec = pl.BlockSpec((1, tq, hb * HEAD_W), lambda bi, h, i: (bi, i, h))
    in_specs = [q_spec,
                pl.BlockSpec((1, t_k, hb * HEAD_W), lambda bi, h, i: (bi, 0, h), **kv_mode),
                pl.BlockSpec((1, hb, n_k, HEAD_W, tk), lambda bi, h, i: (bi, h, 0, 0, 0), **kv_mode)]
    for e in extra:
        in_specs.append(pl.BlockSpec(e.shape, lambda bi, h, i: (0, 0)))
    if kind == "diff":
        def is_own(i):
            n_int, n_tot, qmin = _kv_range("diff", i, tq, tk, n_k, q_off)
            return n_tot - n_int == 1 and n_int * tk == qmin

        own_block = (tk == tq and (tk // 2) % CHUNK == 0 and all(is_own(i) for i in range(t_q // tq)))
        body = functools.partial(_diff_attn_kernel, tq, tk, n_k, q_off, cfg, hb, own_block)
        scratch = [pltpu.VMEM((2 * hb, tq, HEAD_W), BF16), pltpu.VMEM((2 * hb, HEAD_W, tq), F32)]
    else:
        body = functools.partial(_sb_attn_kernel, tq, tk, n_k, q_off, hb)
        scratch = [pltpu.VMEM((hb, HEAD_W, tq), F32)]
    return pl.pallas_call(
        body,
        grid=(b, N_HEADS // hb, t_q // tq),
        in_specs=in_specs,
        out_specs=q_spec,
        out_shape=jax.ShapeDtypeStruct((b, t_q, width), BF16),
        scratch_shapes=scratch,
        compiler_params=_params("parallel", "parallel", "arbitrary"),
        name=kind + "_attn",
    )(q, k, vt, *extra)


def _past_operands(kp_ref, vp_ref, k0, tk):
    heads = range(N_HEADS)
    rows = lambda h: pl.ds(k0 * N_HEADS + h, tk, stride=N_HEADS)
    kbs = [kp_ref[0, rows(h), :].astype(BF16) for h in heads]
    vtbs = [vp_ref[0, rows(h), :].T.astype(BF16) for h in heads]
    return kbs, vtbs


def _new_operands(kn_ref, vn_ref):
    heads = range(N_HEADS)
    kbs = [kn_ref[0, :, h * HEAD_W:(h + 1) * HEAD_W].astype(BF16) for h in heads]
    vtbs = [vn_ref[0, :, h * HEAD_W:(h + 1) * HEAD_W].T.astype(BF16) for h in heads]
    return kbs, vtbs


def _new_positions(q_off, t_new, tq):
    k_idx = lax.broadcasted_iota(jnp.int32, (t_new, 1), 0)
    q_idx = lax.broadcasted_iota(jnp.int32, (1, tq), 1)
    return k_idx, q_off + k_idx, q_off + q_idx


def _diff_cached_kernel(tq, tk, n_past, q_off, t_real, lam_init,
                        q_ref, kp_ref, vp_ref, kn_ref, vn_ref, lq1_ref, lk1_ref, lq2_ref, lk2_ref, g_ref,
                        o_ref, qm_ref, acc_ref):
    _split_maps(q_ref, qm_ref, N_HEADS)
    carry = _diff_init(acc_ref, tq, N_HEADS)
    heads = range(N_HEADS)
    for j in range(n_past):
        kbs = [kp_ref[0, h, :, j * tk:(j + 1) * tk].T.astype(BF16) for h in heads]
        vtbs = [vp_ref[0, pl.ds(j * tk * N_HEADS + h, tk, stride=N_HEADS), :].T.astype(BF16) for h in heads]
        carry = _diff_block(qm_ref, acc_ref, kbs, vtbs, None, carry)
    k_idx, k_pos, q_pos = _new_positions(q_off, kn_ref.shape[1], tq)
    valid = ((k_pos >> CHUNK_SHIFT) <= (q_pos >> CHUNK_SHIFT)) & (k_idx < t_real)
    kbs, vtbs = _new_operands(kn_ref, vn_ref)
    carry = _diff_block(qm_ref, acc_ref, kbs, vtbs, valid, carry)
    _diff_finish(carry, acc_ref, (lq1_ref, lk1_ref, lq2_ref, lk2_ref), g_ref, lam_init, o_ref, N_HEADS)


def _sb_cached_kernel(tq, tk, n_past, q_off, t_real, q_ref, kp_ref, vp_ref, kn_ref, vn_ref, o_ref, acc_ref):
    acc_ref[...] = jnp.zeros_like(acc_ref)
    t_new = kn_ref.shape[1]
    k_idx, k_pos, q_pos = _new_positions(q_off, t_new, tq)
    valid = (k_pos < q_pos) & (k_idx < t_real)
    kbs, vtbs = _new_operands(kn_ref, vn_ref)
    later = (jnp.zeros((1, tq), F32),) * N_HEADS
    later = _sb_block(q_ref, acc_ref, _suffix_matrix(t_new), kbs, vtbs, valid, later)
    suffix2 = _suffix_matrix(tk)

    def past(t, later):
        kbs, vtbs = _past_operands(kp_ref, vp_ref, pl.multiple_of((n_past - 1 - t) * tk, tk), tk)
        return _sb_block(q_ref, acc_ref, suffix2, kbs, vtbs, None, later)

    _sb_walk(past, 0, n_past, later)
    _sb_finish(acc_ref, o_ref, N_HEADS)


def _attention_cached(kind, q, k_past, v_past, k_new, v_new, t_real, extra, lam_init):
    b, tq, width = q.shape
    p_len, t_new = v_past.shape[1] // N_HEADS, k_new.shape[1]
    tk = _tile(p_len, SB_K_TILE)
    n_past = p_len // tk
    row3 = lambda t: pl.BlockSpec((1, t, width), lambda bi: (bi, 0, 0))
    past_spec = pl.BlockSpec((1, p_len * N_HEADS, HEAD_W), lambda bi: (bi, 0, 0))
    k_spec = pl.BlockSpec((1, N_HEADS, HEAD_W, p_len), lambda bi: (bi, 0, 0, 0)) if kind == "diff" else past_spec
    in_specs = [row3(tq), k_spec, past_spec, row3(t_new), row3(t_new)]
    for e in extra:
        in_specs.append(pl.BlockSpec(e.shape, lambda bi: (0, 0)))
    if kind == "diff":
        body = functools.partial(_diff_cached_kernel, tq, tk, n_past, p_len, t_real, lam_init)
        scratch = [pltpu.VMEM((2 * N_HEADS, tq, HEAD_W), BF16), pltpu.VMEM((2 * N_HEADS, HEAD_W, tq), F32)]
    else:
        body = functools.partial(_sb_cached_kernel, tq, tk, n_past, p_len, t_real)
        scratch = [pltpu.VMEM((N_HEADS, HEAD_W, tq), F32)]
    return pl.pallas_call(
        body,
        grid=(b,),
        in_specs=in_specs,
        out_specs=row3(tq),
        out_shape=jax.ShapeDtypeStruct((b, tq, width), BF16),
        scratch_shapes=scratch,
        compiler_params=_params("parallel"),
        name=kind + "_attn_cached",
    )(q, k_past, v_past, k_new, v_new, *extra)


def _out_kernel(oa_ref, ob_ref, ga_ref, gb_ref, h_ref, wua_ref, wub_ref, wo_ref, n_ref, o_ref):
    pa = jnp.dot(oa_ref[...], wua_ref[...], preferred_element_type=F32)
    pb = jnp.dot(ob_ref[...], wub_ref[...], preferred_element_type=F32)
    merged = (ga_ref[...] * pa + gb_ref[...] * pb).astype(BF16)
    r = jnp.dot(merged, wo_ref[...], preferred_element_type=F32)
    o_ref[...] = h_ref[...] + _rms(r, n_ref[...])


def _merge_out(oa, ob, ga, gb, h, wua, wub, wo, n_post):
    n, d = h.shape
    w_attn = oa.shape[1]
    tm = _tile(n, OUT_TOKEN_TILE)
    row = lambda wd: pl.BlockSpec((tm, wd), lambda i: (i, 0))
    const = lambda shape: pl.BlockSpec(shape, lambda i: (0, 0), pipeline_mode=pl.Buffered(1))
    return pl.pallas_call(
        _out_kernel,
        grid=(n // tm,),
        in_specs=[row(w_attn), row(w_attn), row(d), row(d), row(d),
                  const(wua.shape), const(wub.shape), const(wo.shape), const(n_post.shape)],
        out_specs=row(d),
        out_shape=jax.ShapeDtypeStruct((n, d), F32),
        compiler_params=_params("parallel"),
        name="merge_out",
    )(oa, ob, ga, gb, h, wua, wub, wo, n_post)


def _rope_tables(pos):
    half = ROT_DIM // 2
    inv_freq = jnp.power(ROPE_THETA, -jnp.arange(0, ROT_DIM, 2, dtype=F32) / ROT_DIM)
    lane = jnp.arange(HEAD_W) % MAP_W
    ang = pos.astype(F32)[:, None] * inv_freq[lane % half][None, :]
    cos, sin = jnp.cos(ang), jnp.sin(ang)
    c = jnp.where(lane < ROT_DIM, cos, 1.0)
    sa = jnp.where(lane < half, -sin, 0.0)
    sb = jnp.where((lane >= half) & (lane < ROT_DIM), sin, 0.0)
    return c, sa, sb


def _cast_pad_kernel(f, axis, x_ref, o_ref):
    x = x_ref[...].astype(BF16)
    if axis == 1:
        o_ref[:, :f] = x
        o_ref[:, f:] = jnp.zeros((o_ref.shape[0], o_ref.shape[1] - f), BF16)
    else:
        o_ref[:f, :] = x
        o_ref[f:, :] = jnp.zeros((o_ref.shape[0] - f, o_ref.shape[1]), BF16)


def _pad_ffn(w, axis):
    f, other = w.shape[axis], w.shape[1 - axis]
    fp = -(-f // FFN_HIDDEN_TILE) * FFN_HIDDEN_TILE
    t = _tile(other, CAST_TILE)
    if axis == 1:
        in_block, out_block, index, out = (t, f), (t, fp), (lambda i: (i, 0)), (other, fp)
    else:
        in_block, out_block, index, out = (f, t), (fp, t), (lambda i: (0, i)), (fp, other)
    return pl.pallas_call(
        functools.partial(_cast_pad_kernel, f, axis),
        grid=(other // t,),
        in_specs=[pl.BlockSpec(in_block, index)],
        out_specs=pl.BlockSpec(out_block, index),
        out_shape=jax.ShapeDtypeStruct(out, BF16),
        compiler_params=_params("parallel"),
        name="cast_pad",
    )(w)


def _layer(x, pos, past, lam_init, p):
    b, t, d = x.shape
    n = b * t
    w_a = N_HEADS * HEAD_W
    vec = lambda v: v.reshape(1, -1)
    w_in = p["w_in"].astype(BF16)
    seg = lambda lo, hi: w_in[:, lo:hi]

    h1, xn = _ffn(x.reshape(n, d), vec(p["n1a"]), vec(p["n1b"]), vec(p["nma"]),
                  p["f1g"], p["f1u"], p["f1d"])

    tables = tuple(jnp.tile(m, (b, 1)) for m in _rope_tables(pos))
    w_qa, w_ka, w_va, w_qb, w_kb, w_vb = (seg(k * w_a, (k + 1) * w_a) for k in range(6))
    qa_part = (w_qa, "q_rope", MAP_W ** -0.5 * LOG2E, None)
    qb_part = (w_qb, "q", HEAD_W ** -0.5 * LOG2E, None)
    (g_a,), = _proj(xn, [(seg(6 * w_a, 6 * w_a + d), "gate", None, None)], tile=GATE_TOKEN_TILE)
    (g_b,), = _proj(xn, [(seg(6 * w_a + d, 6 * w_a + 2 * d), "gate", None, None)], tile=GATE_TOKEN_TILE)

    shape3 = lambda a: a.reshape(b, t, w_a)
    diff_extra = [vec(p["lq1"]), vec(p["lk1"]), vec(p["lq2"]), vec(p["lk2"]), p["subln_g"].reshape(-1, 1)]
    if past is None:
        (q_a,), (kat, ka16) = _proj(xn, [qa_part, (w_ka, "k_rope", None, None)], tables=tables, seq=t,
                                    tile=PAIR_TOKEN_TILE)
        (q_b,), (kb32, kb16) = _proj(xn, [qb_part, (w_kb, "k", None, None)], tile=PAIR_TOKEN_TILE)
        (va32, vat), (vb32, vbt) = _proj(xn, [(w_va, "v_t", None, _tile(t, DIFF_K_TILE)),
                                              (w_vb, "v_t", None, _tile(t, SB_K_TILE))],
                                         seq=t, tile=PAIR_TOKEN_TILE)
        k_a_rows = kat.reshape(b, N_HEADS, 2, MAP_W, t).transpose(0, 4, 1, 2, 3)
        o_a = _attention("diff", shape3(q_a), shape3(ka16), vat, 0, diff_extra, lam_init)
        o_b = _attention("sb", shape3(q_b), shape3(kb16), vbt, 0, [], None)
    else:
        (q_a,), (ka32,) = _proj(xn, [qa_part, (w_ka, "kv_rope", None, None)], tables=tables,
                                tile=PAIR_TOKEN_TILE)
        (q_b,), (kb32,) = _proj(xn, [qb_part, (w_kb, "kv", None, None)], tile=PAIR_TOKEN_TILE)
        (va32,), (vb32,) = _proj(xn, [(w_va, "kv", None, None), (w_vb, "kv", None, None)],
                                 tile=PAIR_TOKEN_TILE)
        k_a_rows = ka32.reshape(b, t, N_HEADS, 2, MAP_W)
        pka, pva, pkb, pvb = past
        t_p = -(-t // V7X_LANES) * V7X_LANES
        pad_rows = lambda a: jnp.pad(shape3(a), ((0, 0), (0, t_p - t), (0, 0)))
        by_head = lambda c: c.reshape(c.shape[0], c.shape[1] * N_HEADS, HEAD_W)
        pka_t = pka.transpose(0, 2, 3, 4, 1).reshape(pka.shape[0], N_HEADS, HEAD_W, pka.shape[1])
        o_a = _attention_cached("diff", pad_rows(q_a), pka_t, by_head(pva), pad_rows(ka32),
                                pad_rows(va32), t, diff_extra, lam_init)
        o_b = _attention_cached("sb", pad_rows(q_b), by_head(pkb), by_head(pvb), pad_rows(kb32),
                                pad_rows(vb32), t, [], None)
    o_a = o_a[:, :t].reshape(n, w_a)
    o_b = o_b[:, :t].reshape(n, w_a)

    h2 = _merge_out(o_a, o_b, g_a, g_b, h1, p["w_up_a"].astype(BF16), p["w_up_b"].astype(BF16),
                    p["w_o"].astype(BF16), vec(p["nmb"]))
    y = _ffn(h2, vec(p["n2a"]), vec(p["n2b"]), None,
             p["f2g"], p["f2u"], p["f2d"])
    rows = (k_a_rows, va32.reshape(b, t, N_HEADS, HEAD_W),
            kb32.reshape(b, t, N_HEADS, HEAD_W), vb32.reshape(b, t, N_HEADS, HEAD_W))
    return y.reshape(b, t, d), rows


def _stack(xs):
    return xs[0][None] if len(xs) == 1 else jnp.stack(xs)


def kernel(x_prompt, x_sample, cache_diff_k, cache_diff_v, cache_sb_k, cache_sb_v, w_in, w_up_a, w_up_b, w_o, lam_q1, lam_k1, lam_q2, lam_k2, subln_g, norm_ffn1_pre, norm_ffn1_post, norm_mix_pre, norm_mix_post, norm_ffn2_pre, norm_ffn2_post, ffn1_w_gate, ffn1_w_up, ffn1_w_down, ffn2_w_gate, ffn2_w_up, ffn2_w_down):
    depth = w_in.shape[0]
    pos_p = jnp.arange(x_prompt.shape[1], dtype=jnp.int32)
    pos_s = cache_diff_k.shape[2] + jnp.arange(x_sample.shape[1], dtype=jnp.int32)
    hp, hs = x_prompt, x_sample
    rows_p, rows_s = [], []
    for l in range(depth):
        lam_init = 0.8 - 0.6 * math.exp(-0.3 * l)
        p = dict(w_in=w_in[l], w_up_a=w_up_a[l], w_up_b=w_up_b[l], w_o=w_o[l],
                 lq1=lam_q1[l], lk1=lam_k1[l], lq2=lam_q2[l], lk2=lam_k2[l], subln_g=subln_g[l],
                 n1a=norm_ffn1_pre[l], n1b=norm_ffn1_post[l], nma=norm_mix_pre[l], nmb=norm_mix_post[l],
                 n2a=norm_ffn2_pre[l], n2b=norm_ffn2_post[l],
                 f1g=_pad_ffn(ffn1_w_gate[l], 1), f1u=_pad_ffn(ffn1_w_up[l], 1), f1d=_pad_ffn(ffn1_w_down[l], 0),
                 f2g=_pad_ffn(ffn2_w_gate[l], 1), f2u=_pad_ffn(ffn2_w_up[l], 1), f2d=_pad_ffn(ffn2_w_down[l], 0))
        hp, rp = _layer(hp, pos_p, None, lam_init, p)
        past = (cache_diff_k[l], cache_diff_v[l], cache_sb_k[l], cache_sb_v[l])
        hs, rs = _layer(hs, pos_s, past, lam_init, p)
        rows_p.append(rp)
        rows_s.append(rs)
    outs_p = [_stack([r[i] for r in rows_p]) for i in range(4)]
    outs_s = [_stack([r[i] for r in rows_s]) for i in range(4)]
    return (hp, hs, *outs_p, *outs_s)
```
